```python
import math
import jax, jax.numpy as jnp
from jax import lax
import numpy as np

D_MODEL = 1024
BATCH = 16
SEQ = 4096
DEPTH = 4

GRID_W = 64
CTX_LEN = 256
N_MIXERS = 3
D_FF = 4 * D_MODEL
RMS_EPS = 1e-6
ROPE_BASE = 10000.0
Q_BLOCK = 128
FOURIER_GROUPS = 4
DIFF_HEADS = D_MODEL // 128
DIFF_HEAD_DIM = D_MODEL // (2 * DIFF_HEADS)
DIFF_V_DIM = 2 * DIFF_HEAD_DIM
MLA_HEADS = 16
MLA_NOPE_DIM = 64
MLA_ROPE_DIM = 32
MLA_V_DIM = 64
MLA_Q_RANK = 256
MLA_KV_RANK = 128
MLA_DOWN = MLA_Q_RANK + MLA_KV_RANK + MLA_ROPE_DIM
N_FOURIER = (DEPTH + 2) // 3
N_DIFF = (DEPTH + 1) // 3
N_MLA = DEPTH // 3

kernel_name = "hybrid_fourier_diffattn_mla_dit"


def rms_norm(x, g, eps=RMS_EPS):
    x32 = x.astype(jnp.float32)
    y = x32 * lax.rsqrt(jnp.mean(x32 * x32, axis=-1, keepdims=True) + eps)
    return y.astype(x.dtype) * g


def modulate(x, g, shift, scale):
    return rms_norm(x, g) * (1 + scale) + shift


def axial_rope_tables(rows, dim):
    n_freq = dim // 4
    inv_freq = ROPE_BASE ** (-jnp.arange(n_freq, dtype=jnp.float32) / n_freq)
    t = jnp.arange(rows * GRID_W)
    row = (t // GRID_W).astype(jnp.float32)
    col = (t % GRID_W).astype(jnp.float32)
    ang = jnp.stack([row[:, None] * inv_freq, col[:, None] * inv_freq], axis=1)
    return jnp.cos(ang), jnp.sin(ang)


def apply_axial_rope(x, cos, sin):
    shp = x.shape
    n_freq = shp[-1] // 4
    xs = x.reshape(shp[:-1] + (2, 2, n_freq))
    x1, x2 = xs[..., 0, :], xs[..., 1, :]
    cb = cos[None, :, None].astype(x.dtype)
    sb = sin[None, :, None].astype(x.dtype)
    out = jnp.stack([x1 * cb - x2 * sb, x2 * cb + x1 * sb], axis=-2)
    return out.reshape(shp)


def sweep_query_blocks(fn, *qs):
    b, s = qs[0].shape[:2]
    nb = s // Q_BLOCK
    blocks = tuple(jnp.moveaxis(q.reshape((b, nb, Q_BLOCK) + q.shape[2:]), 1, 0) for q in qs)
    out = lax.map(lambda blk: fn(*blk), blocks)
    return jnp.moveaxis(out, 0, 1).reshape((b, s) + out.shape[3:])


def sq_relu_mlp(h, w1, w2):
    return jnp.square(jax.nn.relu(h @ w1)) @ w2


def fourier_mixer(h, w, bias):
    b, s, d = h.shape
    hg = h.astype(jnp.float32).reshape(b, s, FOURIER_GROUPS, d // FOURIER_GROUPS)
    f = jnp.fft.fft2(hg, axes=(1, 3), norm="ortho").real
    return f.reshape(b, s, d).astype(h.dtype) @ w + bias


def diff_project(h, w_qkv, rope):
    b, s, _ = h.shape
    q, k, v = jnp.split(h @ w_qkv, 3, axis=-1)
    q = q.reshape(b, s, 2 * DIFF_HEADS, DIFF_HEAD_DIM)
    k = k.reshape(b, s, 2 * DIFF_HEADS, DIFF_HEAD_DIM)
    if rope is not None:
        q = apply_axial_rope(q, *rope)
        k = apply_axial_rope(k, *rope)
    q = q.reshape(b, s, DIFF_HEADS, 2, DIFF_HEAD_DIM)
    k = k.reshape(b, s, DIFF_HEADS, 2, DIFF_HEAD_DIM)
    v = v.reshape(b, s, DIFF_HEADS, DIFF_V_DIM)
    return q[..., 0, :], q[..., 1, :], k[..., 0, :], k[..., 1, :], v


def diff_attend(q1, q2, k1, k2, v, lam):
    scale = DIFF_HEAD_DIM ** -0.5
    s1 = jnp.einsum("bqhd,bkhd->bhqk", q1, k1).astype(jnp.float32) * scale
    s2 = jnp.einsum("bqhd,bkhd->bhqk", q2, k2).astype(jnp.float32) * scale
    p = jax.nn.softmax(s1, axis=-1) - lam * jax.nn.softmax(s2, axis=-1)
    return jnp.einsum("bhqk,bkhe->bqhe", p.astype(v.dtype), v)


def diff_attention_mixer(h_lat, h_ctx, w_qkv, lq1, lk1, lq2, lk2, subln_g, w_o, layer_idx, rope, with_ctx):
    lam_init = 0.8 - 0.6 * math.exp(-0.3 * layer_idx)
    lam = (jnp.exp(jnp.sum(lq1.astype(jnp.float32) * lk1.astype(jnp.float32)))
           - jnp.exp(jnp.sum(lq2.astype(jnp.float32) * lk2.astype(jnp.float32))) + lam_init)
    q1l, q2l, k1l, k2l, vl = diff_project(h_lat, w_qkv, rope)
    q1c, q2c, k1c, k2c, vc = diff_project(h_ctx, w_qkv, None)
    k1a = jnp.concatenate([k1c, k1l], axis=1)
    k2a = jnp.concatenate([k2c, k2l], axis=1)
    va = jnp.concatenate([vc, vl], axis=1)

    def finish(o):
        b, s = o.shape[:2]
        return (rms_norm(o, subln_g) * (1 - lam_init)).reshape(b, s, D_MODEL) @ w_o

    o_lat = sweep_query_blocks(lambda a, bq: diff_attend(a, bq, k1a, k2a, va, lam), q1l, q2l)
    y_lat = finish(o_lat)
    y_ctx = finish(diff_attend(q1c, q2c, k1c, k2c, vc, lam)) if with_ctx else None
    return y_lat, y_ctx


def mla_project(h, w_down, q_norm_g, kv_norm_g, w_uq, w_ukv, rope):
    b, s, _ = h.shape
    cq, ckv, k_rope = jnp.split(h @ w_down, [MLA_Q_RANK, MLA_Q_RANK + MLA_KV_RANK], axis=-1)
    q = (rms_norm(cq, q_norm_g) @ w_uq).reshape(b, s, MLA_HEADS, MLA_NOPE_DIM + MLA_ROPE_DIM)
    q_nope, q_rope = jnp.split(q, [MLA_NOPE_DIM], axis=-1)
    kv = (rms_norm(ckv, kv_norm_g) @ w_ukv).reshape(b, s, MLA_HEADS, MLA_NOPE_DIM + MLA_V_DIM)
    k_nope, v = jnp.split(kv, [MLA_NOPE_DIM], axis=-1)
    k_rope = k_rope[:, :, None, :]
    if rope is not None:
        q_rope = apply_axial_rope(q_rope, *rope)
        k_rope = apply_axial_rope(k_rope, *rope)
    return q_nope, q_rope, k_nope, k_rope[:, :, 0], v


def mla_attend(q_nope, q_rope, k_nope, k_rope, v):
    scale = (MLA_NOPE_DIM + MLA_ROPE_DIM) ** -0.5
    s = (jnp.einsum("bqhd,bkhd->bhqk", q_nope, k_nope)
         + jnp.einsum("bqhr,bkr->bhqk", q_rope, k_rope)).astype(jnp.float32) * scale
    p = jax.nn.softmax(s, axis=-1)
    return jnp.einsum("bhqk,bkhe->bqhe", p.astype(v.dtype), v)


def mla_mixer(h_lat, h_ctx, w_down, q_norm_g, kv_norm_g, w_uq, w_ukv, w_o, rope, with_ctx):
    qn_l, qr_l, kn_l, kr_l, v_l = mla_project(h_lat, w_down, q_norm_g, kv_norm_g, w_uq, w_ukv, rope)
    qn_c, qr_c, kn_c, kr_c, v_c = mla_project(h_ctx, w_down, q_norm_g, kv_norm_g, w_uq, w_ukv, None)
    kn_a = jnp.concatenate([kn_c, kn_l], axis=1)
    kr_a = jnp.concatenate([kr_c, kr_l], axis=1)
    v_a = jnp.concatenate([v_c, v_l], axis=1)
    o_lat = sweep_query_blocks(lambda qn, qr: mla_attend(qn, qr, kn_a, kr_a, v_a), qn_l, qr_l)
    b, s = o_lat.shape[:2]
    y_lat = o_lat.reshape(b, s, MLA_HEADS * MLA_V_DIM) @ w_o
    y_ctx = None
    if with_ctx:
        o_ctx = mla_attend(qn_c, qr_c, kn_c, kr_c, v_c)
        y_ctx = o_ctx.reshape(o_ctx.shape[0], o_ctx.shape[1], MLA_HEADS * MLA_V_DIM) @ w_o
    return y_lat, y_ctx


def setup_inputs(seed: int = 0) -> dict:
    key = jax.random.key(seed)
    ks = jax.random.split(key, 26)

    def nrm(k, shape, scale):
        return jax.random.normal(k, shape, jnp.float32) * scale

    def gain(k, shape):
        return 1.0 + 0.05 * jax.random.normal(k, shape, jnp.float32)

    d = D_MODEL
    return {
        "x": nrm(ks[0], (BATCH, SEQ, d), 1.0),
        "c": nrm(ks[1], (BATCH, d), 1.0),
        "ctx": nrm(ks[2], (BATCH, CTX_LEN, d), 1.0),
        "c_ctx": nrm(ks[3], (d,), 1.0),
        "mod_w": nrm(ks[4], (DEPTH, d, 6 * d), 0.5 * d ** -0.5),
        "mod_b": nrm(ks[5], (DEPTH, 6 * d), 0.02),
        "norm_mix_g": gain(ks[6], (DEPTH, d)),
        "norm_mlp_g": gain(ks[7], (DEPTH, d)),
        "final_g": gain(ks[8], (d,)),
        "mlp_w1": nrm(ks[9], (DEPTH, d, D_FF), d ** -0.5),
        "mlp_w2": nrm(ks[10], (DEPTH, D_FF, d), D_FF ** -0.5),
        "fourier_w": nrm(ks[11], (N_FOURIER, d, d), d ** -0.5),
        "fourier_b": nrm(ks[12], (N_FOURIER, d), 0.02),
        "diff_w_qkv": nrm(ks[13], (N_DIFF, d, 3 * d), d ** -0.5),
        "diff_lambda_q1": nrm(ks[14], (N_DIFF, DIFF_HEAD_DIM), 0.1),
        "diff_lambda_k1": nrm(ks[15], (N_DIFF, DIFF_HEAD_DIM), 0.1),
        "diff_lambda_q2": nrm(ks[16], (N_DIFF, DIFF_HEAD_DIM), 0.1),
        "diff_lambda_k2": nrm(ks[17], (N_DIFF, DIFF_HEAD_DIM), 0.1),
        "diff_subln_g": gain(ks[18], (N_DIFF, DIFF_V_DIM)),
        "diff_w_o": nrm(ks[19], (N_DIFF, d, d), d ** -0.5),
        "mla_w_down": nrm(ks[20], (N_MLA, d, MLA_DOWN), d ** -0.5),
        "mla_q_norm_g": gain(ks[21], (N_MLA, MLA_Q_RANK)),
        "mla_kv_norm_g": gain(ks[22], (N_MLA, MLA_KV_RANK)),
        "mla_w_uq": nrm(ks[23], (N_MLA, MLA_Q_RANK, MLA_HEADS * (MLA_NOPE_DIM + MLA_ROPE_DIM)), MLA_Q_RANK ** -0.5),
        "mla_w_ukv": nrm(ks[24], (N_MLA, MLA_KV_RANK, MLA_HEADS * (MLA_NOPE_DIM + MLA_V_DIM)), MLA_KV_RANK ** -0.5),
        "mla_w_o": nrm(ks[25], (N_MLA, MLA_HEADS * MLA_V_DIM, d), (MLA_HEADS * MLA_V_DIM) ** -0.5),
    }


def reference(x, c, ctx, c_ctx, mod_w, mod_b, norm_mix_g, norm_mlp_g, final_g, mlp_w1, mlp_w2,
              fourier_w, fourier_b, diff_w_qkv, diff_lambda_q1, diff_lambda_k1, diff_lambda_q2,
              diff_lambda_k2, diff_subln_g, diff_w_o, mla_w_down, mla_q_norm_g, mla_kv_norm_g,
              mla_w_uq, mla_w_ukv, mla_w_o):
    rows = x.shape[1] // GRID_W
    rope_diff = axial_rope_tables(rows, DIFF_HEAD_DIM)
    rope_mla = axial_rope_tables(rows, MLA_ROPE_DIM)
    silu_c = jax.nn.silu(c)[:, None, :]
    silu_cc = jax.nn.silu(c_ctx)
    x_lat, x_ctx = x, ctx
    for i in range(DEPTH):
        kind, j = i % N_MIXERS, i // N_MIXERS
        update_ctx = i < DEPTH - 1
        ctx_feeds_mixer = update_ctx or kind != 0
        sh_a, sc_a, g_a, sh_m, sc_m, g_m = jnp.split(silu_c @ mod_w[i] + mod_b[i], 6, axis=-1)
        h_lat = modulate(x_lat, norm_mix_g[i], sh_a, sc_a)
        if ctx_feeds_mixer:
            csh_a, csc_a, cg_a, csh_m, csc_m, cg_m = jnp.split(silu_cc @ mod_w[i] + mod_b[i], 6, axis=-1)
            h_ctx = modulate(x_ctx, norm_mix_g[i], csh_a, csc_a)
        if kind == 0:
            y_lat = fourier_mixer(h_lat, fourier_w[j], fourier_b[j])
            y_ctx = fourier_mixer(h_ctx, fourier_w[j], fourier_b[j]) if update_ctx else None
        elif kind == 1:
            y_lat, y_ctx = diff_attention_mixer(
                h_lat, h_ctx, diff_w_qkv[j], diff_lambda_q1[j], diff_lambda_k1[j], diff_lambda_q2[j],
                diff_lambda_k2[j], diff_subln_g[j], diff_w_o[j], i, rope_diff, update_ctx)
        else:
            y_lat, y_ctx = mla_mixer(
                h_lat, h_ctx, mla_w_down[j], mla_q_norm_g[j], mla_kv_norm_g[j], mla_w_uq[j],
                mla_w_ukv[j], mla_w_o[j], rope_mla, update_ctx)
        x_lat = x_lat + g_a * y_lat
        x_lat = x_lat + g_m * sq_relu_mlp(modulate(x_lat, norm_mlp_g[i], sh_m, sc_m), mlp_w1[i], mlp_w2[i])
        if update_ctx:
            x_ctx = x_ctx + cg_a * y_ctx
            x_ctx = x_ctx + cg_m * sq_relu_mlp(modulate(x_ctx, norm_mlp_g[i], csh_m, csc_m), mlp_w1[i], mlp_w2[i])
    return rms_norm(x_lat, final_g)
```

```python
import functools
import math

import numpy as np
import jax
import jax.numpy as jnp
from jax import lax
from jax.experimental import pallas as pl
from jax.experimental.pallas import tpu as pltpu

F32 = jnp.float32
BF16 = jnp.bfloat16

GRID_W = 64
RMS_EPS = 1e-6
ROPE_BASE = 10000.0
FOURIER_GROUPS = 4
DIFF_HEAD_DIM = 64
DIFF_V_DIM = 128
MLA_HEADS = 16
MLA_NOPE_DIM = 64
MLA_ROPE_DIM = 32
MLA_V_DIM = 64
MLA_Q_RANK = 256
MLA_KV_RANK = 128
N_MIXERS = 3

LANES = 128
FFT_RADIX = 64
VMEM_LIMIT = 56 * 1024 * 1024
LOG2E = 1.4426950408889634


def _cparams(n_axes):
    return pltpu.CompilerParams(
        dimension_semantics=("arbitrary",) * n_axes, vmem_limit_bytes=VMEM_LIMIT)


def _resident(shape):
    nd = len(shape)
    return pl.BlockSpec(shape, lambda *_: (0,) * nd, pipeline_mode=pl.Buffered(1))


def _norm_mod(x, g, sh, sc):
    r = lax.rsqrt(jnp.mean(x * x, axis=-1, keepdims=True) + RMS_EPS)
    return ((x * r) * g) * (1.0 + sc) + sh


def _rms(x, g):
    r = lax.rsqrt(jnp.mean(x * x, axis=-1, keepdims=True) + RMS_EPS)
    return (x * r) * g


def _rope_tile(x, cos, sin, half):
    lane = lax.broadcasted_iota(jnp.int32, x.shape, 1)
    fwd = pltpu.roll(x, LANES - half, 1)
    bwd = pltpu.roll(x, half, 1)
    partner = jnp.where((lane & half) == 0, fwd, bwd)
    return x * cos + partner * sin


def _mod_kernel(c_ref, w_ref, b_ref, o_ref):
    c = c_ref[...]
    s = (c / (1.0 + jnp.exp(-c))).astype(BF16)
    o_ref[0] = jnp.dot(s, w_ref[0].astype(BF16), preferred_element_type=F32) + b_ref[0]


def _modulation(cc, mod_w, mod_b):
    depth, d, n = mod_w.shape
    rows = cc.shape[0]
    tn = n // 4
    return pl.pallas_call(
        _mod_kernel,
        grid=(depth, n // tn),
        in_specs=[
            pl.BlockSpec((rows, d), lambda i, j: (0, 0)),
            pl.BlockSpec((1, d, tn), lambda i, j: (i, 0, j)),
            pl.BlockSpec((1, 1, tn), lambda i, j: (i, 0, j)),
        ],
        out_specs=pl.BlockSpec((1, rows, tn), lambda i, j: (i, 0, j)),
        out_shape=jax.ShapeDtypeStruct((depth, rows, n), F32),
        compiler_params=_cparams(2),
        name="modulation",
    )(cc, mod_w, mod_b.reshape(depth, 1, n))


def _mod_spec(tiles_per_batch, d):
    return pl.BlockSpec((1, 1, d), lambda i: (i // tiles_per_batch, 0, 0))


def _mlp_kernel(x_ref, g_ref, sh_ref, sc_ref, gt_ref, w1_ref, w2_ref, fg_ref, o_ref, *, ff_chunk,
                final_norm):
    x = x_ref[...]
    h = _norm_mod(x, g_ref[...], sh_ref[0], sc_ref[0]).astype(BF16)
    d_ff = w1_ref.shape[1]
    acc = jnp.zeros(x.shape, F32)
    for c in range(d_ff // ff_chunk):
        u = jnp.dot(h, w1_ref[:, c * ff_chunk:(c + 1) * ff_chunk], preferred_element_type=F32)
        u = jnp.square(jnp.maximum(u, 0.0)).astype(BF16)
        acc = acc + jnp.dot(u, w2_ref[c * ff_chunk:(c + 1) * ff_chunk, :],
                            preferred_element_type=F32)
    out = x + gt_ref[0] * acc
    if final_norm:
        out = _rms(out, fg_ref[...])
    o_ref[...] = out


def _mlp(x, g, sh, sc, gate, w1, w2, final_g, *, final_norm, tm):
    t, d = x.shape
    nb = sh.shape[0]
    tpb = t // nb // tm
    d_ff = w1.shape[1]
    return pl.pallas_call(
        functools.partial(_mlp_kernel, ff_chunk=1024, final_norm=final_norm),
        grid=(t // tm,),
        in_specs=[
            pl.BlockSpec((tm, d), lambda i: (i, 0)),
            _resident((1, d)),
            _mod_spec(tpb, d), _mod_spec(tpb, d), _mod_spec(tpb, d),
            _resident((d, d_ff)), _resident((d_ff, d)),
            _resident((1, d)),
        ],
        out_specs=pl.BlockSpec((tm, d), lambda i: (i, 0)),
        out_shape=jax.ShapeDtypeStruct((t, d), F32),
        compiler_params=_cparams(1),
        name="mlp",
    )(x, g, sh, sc, gate, w1, w2, final_g)


def _resid_kernel(a_ref, w_ref, b_ref, x_ref, gt_ref, o_ref):
    y = jnp.dot(a_ref[...], w_ref[...], preferred_element_type=F32) + b_ref[...]
    o_ref[...] = x_ref[...] + gt_ref[0] * y


def _resid_proj(a, w, bias, x, gate, *, tm):
    t, k = a.shape
    d = w.shape[1]
    nb = gate.shape[0]
    tpb = t // nb // tm
    return pl.pallas_call(
        _resid_kernel,
        grid=(t // tm,),
        in_specs=[
            pl.BlockSpec((tm, k), lambda i: (i, 0)),
            _resident((k, d)), _resident((1, d)),
            pl.BlockSpec((tm, d), lambda i: (i, 0)),
            _mod_spec(tpb, d),
        ],
        out_specs=pl.BlockSpec((tm, d), lambda i: (i, 0)),
        out_shape=jax.ShapeDtypeStruct((t, d), F32),
        compiler_params=_cparams(1),
        name="resid_proj",
    )(a, w, bias, x, gate)


def _dft_mats():
    cg = 256
    j = np.arange(cg)
    th = 2.0 * np.pi * np.outer(j, j) / cg
    wc = np.concatenate([np.cos(th), -np.sin(th)], axis=1) / 16.0
    ctx_seq = np.concatenate([np.cos(th), np.sin(th)], axis=1) / 16.0
    r = np.arange(FFT_RADIX)
    ph = 2.0 * np.pi * np.outer(r, r) / FFT_RADIX
    dr, di = np.cos(ph) / 8.0, -np.sin(ph) / 8.0
    m1 = np.block([[dr, -di], [di, dr]])
    m2 = np.concatenate([dr, -di], axis=1)
    tw = 2.0 * np.pi * np.outer(r, r) / (FFT_RADIX * FFT_RADIX)
    tr = np.repeat(np.cos(tw)[:, :, None], LANES, axis=2)
    ti = np.repeat(-np.sin(tw)[:, :, None], LANES, axis=2)
    as_bf = lambda a: jnp.asarray(a, F32).astype(BF16)
    return dict(wc=as_bf(wc), ctx_seq=as_bf(ctx_seq), m1=as_bf(m1), m2=as_bf(m2),
                tr=jnp.asarray(tr, F32), ti=jnp.asarray(ti, F32))


def _chan_dft(h, wc_ref):
    cg = wc_ref.shape[0]
    res, ims = [], []
    for grp in range(h.shape[1] // cg):
        z = jnp.dot(h[:, grp * cg:(grp + 1) * cg], wc_ref[...], preferred_element_type=F32)
        res.append(z[:, :cg])
        ims.append(z[:, cg:])
    return jnp.concatenate(res, axis=1), jnp.concatenate(ims, axis=1)


def _fourier_in_kernel(x_ref, g_ref, sh_ref, sc_ref, wc_ref, zr_ref, zi_ref):
    h = _norm_mod(x_ref[...], g_ref[...], sh_ref[0], sc_ref[0]).astype(BF16)
    zr, zi = _chan_dft(h, wc_ref)
    zr_ref[...] = zr.astype(BF16)
    zi_ref[...] = zi.astype(BF16)


def _fft_stage1_kernel(zr_ref, zi_ref, m1_ref, ar_ref, ai_ref):
    z = jnp.concatenate([zr_ref[0], zi_ref[0]], axis=0)
    a = jnp.dot(m1_ref[...], z, preferred_element_type=F32)
    r = a.shape[0] // 2
    ar_ref[0] = a[:r].astype(BF16)
    ai_ref[0] = a[r:].astype(BF16)


def _fft_stage2_kernel(ar_ref, ai_ref, tr_ref, ti_ref, m2_ref, o_ref):
    reps = ar_ref.shape[3] // LANES
    for j in range(ar_ref.shape[1]):
        re = ar_ref[0, j].astype(F32)
        im = ai_ref[0, j].astype(F32)
        tr = jnp.tile(tr_ref[j], (1, reps))
        ti = jnp.tile(ti_ref[j], (1, reps))
        z = jnp.concatenate([re * tr - im * ti, re * ti + im * tr], axis=0).astype(BF16)
        o_ref[0, j] = jnp.dot(m2_ref[...], z, preferred_element_type=F32).astype(BF16)


def _fourier_lat(x, g, sh, sc, mats, *, batch, tm):
    t, d = x.shape
    s = t // batch
    tpb = s // tm
    zr, zi = pl.pallas_call(
        _fourier_in_kernel,
        grid=(t // tm,),
        in_specs=[
            pl.BlockSpec((tm, d), lambda i: (i, 0)),
            _resident((1, d)), _mod_spec(tpb, d), _mod_spec(tpb, d),
            _resident(mats["wc"].shape),
        ],
        out_specs=[pl.BlockSpec((tm, d), lambda i: (i, 0))] * 2,
        out_shape=[jax.ShapeDtypeStruct((t, d), BF16)] * 2,
        compiler_params=_cparams(1),
        name="fourier_in",
    )(x, g, sh, sc, mats["wc"])

    r = FFT_RADIX
    cols = (s // r) * d
    tn = 8192
    blk = pl.BlockSpec((1, r, tn), lambda b, j: (b, 0, j))
    ar, ai = pl.pallas_call(
        _fft_stage1_kernel,
        grid=(batch, cols // tn),
        in_specs=[blk, blk, pl.BlockSpec(mats["m1"].shape, lambda b, j: (0, 0))],
        out_specs=[blk, blk],
        out_shape=[jax.ShapeDtypeStruct((batch, r, cols), BF16)] * 2,
        compiler_params=_cparams(2),
        name="fft_stage1",
    )(zr.reshape(batch, r, cols), zi.reshape(batch, r, cols), mats["m1"])

    kg = 8
    blk4 = pl.BlockSpec((1, kg, r, d), lambda b, j: (b, j, 0, 0))
    tw = pl.BlockSpec((kg, r, LANES), lambda b, j: (j, 0, 0))
    frev = pl.pallas_call(
        _fft_stage2_kernel,
        grid=(batch, r // kg),
        in_specs=[blk4, blk4, tw, tw, pl.BlockSpec(mats["m2"].shape, lambda b, j: (0, 0))],
        out_specs=blk4,
        out_shape=jax.ShapeDtypeStruct((batch, r, r, d), BF16),
        compiler_params=_cparams(2),
        name="fft_stage2",
    )(ar.reshape(batch, r, r, d), ai.reshape(batch, r, r, d), mats["tr"], mats["ti"], mats["m2"])
    return jnp.swapaxes(frev, 1, 2).reshape(t, d)


def _fourier_ctx_kernel(x_ref, g_ref, sh_ref, sc_ref, gt_ref, wc_ref, seq_ref, w_ref, b_ref, o_ref):
    x = x_ref[...]
    h = _norm_mod(x, g_ref[...], sh_ref[0], sc_ref[0]).astype(BF16)
    zr, zi = _chan_dft(h, wc_ref)
    z = jnp.concatenate([zr, zi], axis=0).astype(BF16)
    f = jnp.dot(seq_ref[...], z, preferred_element_type=F32).astype(BF16)
    y = jnp.dot(f, w_ref[...], preferred_element_type=F32) + b_ref[...]
    o_ref[...] = x + gt_ref[0] * y


def _fourier_ctx(x, g, sh, sc, gate, mats, w, bias, *, ctx_len):
    t, d = x.shape
    n_tiles = t // ctx_len
    return pl.pallas_call(
        _fourier_ctx_kernel,
        grid=(n_tiles,),
        in_specs=[
            pl.BlockSpec((ctx_len, d), lambda i: (i, 0)),
            _resident((1, d)), _mod_spec(n_tiles, d), _mod_spec(n_tiles, d), _mod_spec(n_tiles, d),
            _resident(mats["wc"].shape), _resident(mats["ctx_seq"].shape),
            _resident(w.shape), _resident((1, d)),
        ],
        out_specs=pl.BlockSpec((ctx_len, d), lambda i: (i, 0)),
        out_shape=jax.ShapeDtypeStruct((t, d), F32),
        compiler_params=_cparams(1),
        name="fourier_ctx",
    )(x, g, sh, sc, gate, mats["wc"], mats["ctx_seq"], w, bias)


def _rope_tables(rows, dim, lane_of_dim):
    n_freq = dim // 4
    inv_freq = ROPE_BASE ** (-jnp.arange(n_freq, dtype=F32) / n_freq)
    t = jnp.arange(rows * GRID_W)
    row = (t // GRID_W).astype(F32)
    col = (t % GRID_W).astype(F32)
    ang = jnp.stack([row[:, None] * inv_freq, col[:, None] * inv_freq], axis=1)
    cos, sin = jnp.cos(ang), jnp.sin(ang)
    dims = np.array([lane_of_dim(l) for l in range(LANES)])
    live = dims >= 0
    dd = np.where(live, dims, 0)
    axis, pair, freq = dd // (2 * n_freq), (dd % (2 * n_freq)) // n_freq, dd % n_freq
    sign = np.where(pair == 0, -1.0, 1.0).astype(np.float32)
    cos_t = jnp.where(live[None, :], cos[:, axis, freq], 1.0)
    sin_t = jnp.where(live[None, :], sin[:, axis, freq] * sign[None, :], 0.0)
    return cos_t.astype(F32), sin_t.astype(F32)


def _softmax_parts(parts):
    m = functools.reduce(jnp.maximum, [jnp.max(p, axis=-1, keepdims=True) for p in parts])
    es = [jnp.exp2(p - m) for p in parts]
    l = functools.reduce(jnp.add, [jnp.sum(e, axis=-1, keepdims=True) for e in es])
    inv = 1.0 / l
    return [e * inv for e in es]


def _scores(q, k_refs):
    nt = (((1,), (1,)), ((), ()))
    return [lax.dot_general(q, k[0], nt, preferred_element_type=F32) for k in k_refs]


def _diff_qkv_kernel(x_ref, g_ref, sh_ref, sc_ref, w_ref, cos_ref, sin_ref, o_ref, *, rope, qscale):
    h = _norm_mod(x_ref[...], g_ref[...], sh_ref[0], sc_ref[0]).astype(BF16)
    d = x_ref.shape[1]
    acc = jnp.dot(h, w_ref[...], preferred_element_type=F32)
    for tile in range(3 * d // LANES):
        a = acc[:, tile * LANES:(tile + 1) * LANES]
        if tile < 2 * d // LANES:
            if rope:
                a = _rope_tile(a, cos_ref[...], sin_ref[...], DIFF_HEAD_DIM // 4)
            if tile < d // LANES:
                a = a * qscale
        o_ref[:, tile * LANES:(tile + 1) * LANES] = a.astype(BF16)


def _diff_qkv(x, g, sh, sc, w, cos, sin, *, rope, tm):
    t, d = x.shape
    nb = sh.shape[0]
    tpb = t // nb // tm
    spb = t // nb // tm
    n = w.shape[1]
    qscale = DIFF_HEAD_DIM ** -0.5 * LOG2E
    tab = pl.BlockSpec((tm, LANES), (lambda i: (i % spb, 0)) if rope else (lambda i: (0, 0)))
    return pl.pallas_call(
        functools.partial(_diff_qkv_kernel, rope=rope, qscale=qscale),
        grid=(t // tm,),
        in_specs=[
            pl.BlockSpec((tm, d), lambda i: (i, 0)),
            _resident((1, d)), _mod_spec(tpb, d), _mod_spec(tpb, d),
            _resident((d, n)), tab, tab,
        ],
        out_specs=pl.BlockSpec((tm, n), lambda i: (i, 0)),
        out_shape=jax.ShapeDtypeStruct((t, n), BF16),
        compiler_params=_cparams(1),
        name="diff_qkv",
    )(x, g, sh, sc, w, cos, sin)


def _diff_attn_kernel(lq1_ref, lk1_ref, lq2_ref, lk2_ref, sg_ref, q_ref, *refs, n_seg, lam_init):
    k_refs, v_refs, o_ref = refs[:n_seg], refs[n_seg:2 * n_seg], refs[2 * n_seg]
    lam = (jnp.exp(jnp.sum(lq1_ref[...] * lk1_ref[...], axis=-1, keepdims=True))
           - jnp.exp(jnp.sum(lq2_ref[...] * lk2_ref[...], axis=-1, keepdims=True)) + lam_init)
    q = q_ref[0]
    lane = lax.broadcasted_iota(jnp.int32, q.shape, 1)
    zero = jnp.zeros_like(q)
    p1 = _softmax_parts(_scores(jnp.where(lane < DIFF_HEAD_DIM, q, zero), k_refs))
    p2 = _softmax_parts(_scores(jnp.where(lane >= DIFF_HEAD_DIM, q, zero), k_refs))
    o = None
    for a, b, v in zip(p1, p2, v_refs):
        part = jnp.dot((a - lam * b).astype(BF16), v[0], preferred_element_type=F32)
        o = part if o is None else o + part
    o_ref[0] = (_rms(o, sg_ref[...]) * (1.0 - lam_init)).astype(BF16)


def _diff_attn(lams, subln_g, q_arr, kv_arrs, *, heads, tq, lam_init):
    b, sq, _ = q_arr.shape
    n_seg = len(kv_arrs)
    lam_specs = [pl.BlockSpec(l.shape, lambda bb, h, i: (0, 0)) for l in lams]
    in_specs = lam_specs + [
        pl.BlockSpec((1, DIFF_V_DIM), lambda bb, h, i: (0, 0)),
        pl.BlockSpec((1, tq, LANES), lambda bb, h, i: (bb, i, h)),
    ]
    in_specs += [pl.BlockSpec((1, a.shape[1], LANES), lambda bb, h, i: (bb, 0, heads + h))
                 for a in kv_arrs]
    in_specs += [pl.BlockSpec((1, a.shape[1], LANES), lambda bb, h, i: (bb, 0, 2 * heads + h))
                 for a in kv_arrs]
    return pl.pallas_call(
        functools.partial(_diff_attn_kernel, n_seg=n_seg, lam_init=lam_init),
        grid=(b, heads, sq // tq),
        in_specs=in_specs,
        out_specs=pl.BlockSpec((1, tq, LANES), lambda bb, h, i: (bb, i, h)),
        out_shape=jax.ShapeDtypeStruct((b, sq, heads * DIFF_V_DIM), BF16),
        compiler_params=_cparams(3),
        name="diff_attn",
    )(*lams, subln_g, q_arr, *kv_arrs, *kv_arrs)


def _mla_down_kernel(x_ref, g_ref, sh_ref, sc_ref, w_ref, qg_ref, kvg_ref, cos_ref, sin_ref,
                     cq_ref, ckv_ref, kr_ref, *, rope):
    h = _norm_mod(x_ref[...], g_ref[...], sh_ref[0], sc_ref[0]).astype(BF16)
    acc = jnp.dot(h, w_ref[...], preferred_element_type=F32)
    cq_ref[...] = _rms(acc[:, :MLA_Q_RANK], qg_ref[...]).astype(BF16)
    ckv_ref[...] = _rms(acc[:, MLA_Q_RANK:MLA_Q_RANK + MLA_KV_RANK], kvg_ref[...]).astype(BF16)
    kr = acc[:, MLA_Q_RANK + MLA_KV_RANK:]
    if rope:
        kr = _rope_tile(kr, cos_ref[...], sin_ref[...], MLA_ROPE_DIM // 4)
    kr_ref[...] = kr.astype(BF16)


def _mla_up_kernel(cq_ref, ckv_ref, kr_ref, wq_ref, wk_ref, wv_ref, cos_ref, sin_ref,
                   q_ref, k_ref, v_ref, *, rope, qscale):
    q = jnp.dot(cq_ref[...], wq_ref[...], preferred_element_type=F32)
    for hd in range(MLA_HEADS):
        a = q[:, hd * LANES:(hd + 1) * LANES]
        if rope:
            a = _rope_tile(a, cos_ref[...], sin_ref[...], MLA_ROPE_DIM // 4)
        q_ref[:, hd * LANES:(hd + 1) * LANES] = (a * qscale).astype(BF16)
    k = jnp.dot(ckv_ref[...], wk_ref[...], preferred_element_type=F32)
    kr = kr_ref[...].astype(F32)
    for hd in range(MLA_HEADS):
        k_ref[:, hd * LANES:(hd + 1) * LANES] = (k[:, hd * LANES:(hd + 1) * LANES] + kr).astype(BF16)
    v_ref[...] = jnp.dot(ckv_ref[...], wv_ref[...], preferred_element_type=F32).astype(BF16)


def _mla_project(x, g, sh, sc, w_down, qg, kvg, wq, wk, wv, cos, sin, *, rope, tm):
    t, d = x.shape
    nb = sh.shape[0]
    tpb = t // nb // tm
    tab = pl.BlockSpec((tm, LANES), (lambda i: (i % tpb, 0)) if rope else (lambda i: (0, 0)))
    row = lambda n: pl.BlockSpec((tm, n), lambda i: (i, 0))
    cq, ckv, kr = pl.pallas_call(
        functools.partial(_mla_down_kernel, rope=rope),
        grid=(t // tm,),
        in_specs=[row(d), _resident((1, d)), _mod_spec(tpb, d), _mod_spec(tpb, d),
                  _resident(w_down.shape), _resident(qg.shape), _resident(kvg.shape), tab, tab],
        out_specs=[row(MLA_Q_RANK), row(MLA_KV_RANK), row(LANES)],
        out_shape=[jax.ShapeDtypeStruct((t, MLA_Q_RANK), BF16),
                   jax.ShapeDtypeStruct((t, MLA_KV_RANK), BF16),
                   jax.ShapeDtypeStruct((t, LANES), BF16)],
        compiler_params=_cparams(1),
        name="mla_down",
    )(x, g, sh, sc, w_down, qg, kvg, cos, sin)
    qscale = (MLA_NOPE_DIM + MLA_ROPE_DIM) ** -0.5 * LOG2E
    nq, nv = wq.shape[1], wv.shape[1]
    return pl.pallas_call(
        functools.partial(_mla_up_kernel, rope=rope, qscale=qscale),
        grid=(t // tm,),
        in_specs=[row(MLA_Q_RANK), row(MLA_KV_RANK), row(LANES),
                  _resident(wq.shape), _resident(wk.shape), _resident(wv.shape), tab, tab],
        out_specs=[row(nq), row(nq), row(nv)],
        out_shape=[jax.ShapeDtypeStruct((t, nq), BF16), jax.ShapeDtypeStruct((t, nq), BF16),
                   jax.ShapeDtypeStruct((t, nv), BF16)],
        compiler_params=_cparams(1),
        name="mla_up",
    )(cq, ckv, kr, wq, wk, wv, cos, sin)


def _mla_attn_kernel(q_ref, *refs, n_seg):
    k_refs, v_refs, o_ref = refs[:n_seg], refs[n_seg:2 * n_seg], refs[2 * n_seg]
    outs = []
    for hd in range(2):
        q = q_ref[0, :, hd * LANES:(hd + 1) * LANES]
        ks = [k.at[:, :, hd * LANES:(hd + 1) * LANES] for k in k_refs]
        ps = _softmax_parts(_scores(q, ks))
        o = None
        for p, v in zip(ps, v_refs):
            part = jnp.dot(p.astype(BF16), v[0], preferred_element_type=F32)
            o = part if o is None else o + part
        outs.append(o)
    lane = lax.broadcasted_iota(jnp.int32, outs[0].shape, 1)
    o_ref[0] = jnp.where(lane < MLA_V_DIM, outs[0], outs[1]).astype(BF16)


def _mla_attn(q, k_arrs, v_arrs, *, tq):
    b, sq, nq = q.shape
    pairs = nq // (2 * LANES)
    n_seg = len(k_arrs)
    in_specs = [pl.BlockSpec((1, tq, 2 * LANES), lambda bb, h, i: (bb, i, h))]
    in_specs += [pl.BlockSpec((1, a.shape[1], 2 * LANES), lambda bb, h, i: (bb, 0, h)) for a in k_arrs]
    in_specs += [pl.BlockSpec((1, a.shape[1], LANES), lambda bb, h, i: (bb, 0, h)) for a in v_arrs]
    return pl.pallas_call(
        functools.partial(_mla_attn_kernel, n_seg=n_seg),
        grid=(b, pairs, sq // tq),
        in_specs=in_specs,
        out_specs=pl.BlockSpec((1, tq, LANES), lambda bb, h, i: (bb, i, h)),
        out_shape=jax.ShapeDtypeStruct((b, sq, pairs * LANES), BF16),
        compiler_params=_cparams(3),
        name="mla_attn",
    )(q, *k_arrs, *v_arrs)


def _mla_weights(w_down, w_uq, w_ukv):
    d = w_down.shape[0]
    hq = MLA_NOPE_DIM + MLA_ROPE_DIM
    pad_r = LANES - hq
    lo = MLA_Q_RANK + MLA_KV_RANK
    zeros = lambda n: jnp.zeros((d, n), w_down.dtype)
    wd = jnp.concatenate([w_down[:, :lo], zeros(MLA_NOPE_DIM), w_down[:, lo:], zeros(pad_r)], axis=1)
    wq = w_uq.reshape(MLA_Q_RANK, MLA_HEADS, hq)
    wq = jnp.pad(wq, ((0, 0), (0, 0), (0, pad_r))).reshape(MLA_Q_RANK, MLA_HEADS * LANES)
    wkv = w_ukv.reshape(MLA_KV_RANK, MLA_HEADS, MLA_NOPE_DIM + MLA_V_DIM)
    wk = jnp.pad(wkv[:, :, :MLA_NOPE_DIM], ((0, 0), (0, 0), (0, LANES - MLA_NOPE_DIM)))
    wk = wk.reshape(MLA_KV_RANK, MLA_HEADS * LANES)
    wv = wkv[:, :, MLA_NOPE_DIM:].reshape(MLA_KV_RANK, MLA_HEADS * MLA_V_DIM)
    return wd.astype(BF16), wq.astype(BF16), wk.astype(BF16), wv.astype(BF16)


def kernel(x, c, ctx, c_ctx, mod_w, mod_b, norm_mix_g, norm_mlp_g, final_g, mlp_w1, mlp_w2, fourier_w, fourier_b, diff_w_qkv, diff_lambda_q1, diff_lambda_k1, diff_lambda_q2, diff_lambda_k2, diff_subln_g, diff_w_o, mla_w_down, mla_q_norm_g, mla_kv_norm_g, mla_w_uq, mla_w_ukv, mla_w_o):
    batch, seq, d = x.shape
    ctx_len = ctx.shape[1]
    depth = mod_w.shape[0]
    rows = seq // GRID_W
    assert seq == FFT_RADIX * FFT_RADIX and d == FOURIER_GROUPS * 256 and ctx_len == 256
    tm = 512

    cc_rows = -(-(batch + 1) // 8) * 8
    cc = jnp.zeros((cc_rows, d), F32).at[:batch].set(c).at[batch].set(c_ctx)
    mods = _modulation(cc, mod_w, mod_b).reshape(depth, cc_rows, 6, d)
    mats = _dft_mats()
    zero_bias = jnp.zeros((1, d), F32)
    final_g2 = final_g.reshape(1, d)

    lane_diff = lambda l: l % DIFF_HEAD_DIM
    cos_d, sin_d = _rope_tables(rows, DIFF_HEAD_DIM, lane_diff)
    lane_mla = lambda l: l - MLA_NOPE_DIM if MLA_NOPE_DIM <= l < MLA_NOPE_DIM + MLA_ROPE_DIM else -1
    cos_m, sin_m = _rope_tables(rows, MLA_ROPE_DIM, lane_mla)

    x_lat = x.reshape(batch * seq, d)
    x_ctx = ctx.reshape(batch * ctx_len, d)
    for i in range(depth):
        kind, j = i % N_MIXERS, i // N_MIXERS
        update_ctx = i < depth - 1
        ctx_feeds_mixer = update_ctx or kind != 0
        lat = [mods[i, :batch, k].reshape(batch, 1, d) for k in range(6)]
        cm = [mods[i, batch:batch + 1, k].reshape(1, 1, d) for k in range(6)]
        g_mix = norm_mix_g[i].reshape(1, d)
        g_mlp = norm_mlp_g[i].reshape(1, d)
        if kind == 0:
            w = fourier_w[j].astype(BF16)
            bias = fourier_b[j].reshape(1, d)
            f = _fourier_lat(x_lat, g_mix, lat[0], lat[1], mats, batch=batch, tm=tm)
            x_lat = _resid_proj(f, w, bias, x_lat, lat[2], tm=tm)
            if update_ctx:
                x_ctx = _fourier_ctx(x_ctx, g_mix, cm[0], cm[1], cm[2], mats, w, bias, ctx_len=ctx_len)
        elif kind == 1:
            heads = d // DIFF_V_DIM
            w_qkv = diff_w_qkv[j].astype(BF16)
            w_o = diff_w_o[j].astype(BF16)
            lam_init = 0.8 - 0.6 * math.exp(-0.3 * i)
            lams = [v[j].reshape(1, DIFF_HEAD_DIM) for v in
                    (diff_lambda_q1, diff_lambda_k1, diff_lambda_q2, diff_lambda_k2)]
            sg = diff_subln_g[j].reshape(1, DIFF_V_DIM)
            qkv_l = _diff_qkv(x_lat, g_mix, lat[0], lat[1], w_qkv, cos_d, sin_d, rope=True, tm=tm)
            qkv_l = qkv_l.reshape(batch, seq, 3 * d)
            qkv_c = _diff_qkv(x_ctx, g_mix, cm[0], cm[1], w_qkv, cos_d, sin_d, rope=False, tm=ctx_len)
            qkv_c = qkv_c.reshape(batch, ctx_len, 3 * d)
            o_l = _diff_attn(lams, sg, qkv_l, [qkv_c, qkv_l], heads=heads, tq=256, lam_init=lam_init)
            x_lat = _resid_proj(o_l.reshape(batch * seq, d), w_o, zero_bias, x_lat, lat[2], tm=tm)
            if update_ctx:
                o_c = _diff_attn(lams, sg, qkv_c, [qkv_c], heads=heads, tq=ctx_len, lam_init=lam_init)
                x_ctx = _resid_proj(o_c.reshape(batch * ctx_len, d), w_o, zero_bias, x_ctx, cm[2],
                                    tm=ctx_len)
        else:
            wd, wq, wk, wv = _mla_weights(mla_w_down[j], mla_w_uq[j], mla_w_ukv[j])
            w_o = mla_w_o[j].astype(BF16)
            qg = mla_q_norm_g[j].reshape(1, MLA_Q_RANK)
            kvg = mla_kv_norm_g[j].reshape(1, MLA_KV_RANK)
            q_l, k_l, v_l = _mla_project(x_lat, g_mix, lat[0], lat[1], wd, qg, kvg, wq, wk, wv,
                                         cos_m, sin_m, rope=True, tm=tm)
            q_c, k_c, v_c = _mla_project(x_ctx, g_mix, cm[0], cm[1], wd, qg, kvg, wq, wk, wv,
                                         cos_m, sin_m, rope=False, tm=ctx_len)
            r3 = lambda a, s: a.reshape(batch, s, a.shape[1])
            o_l = _mla_attn(r3(q_l, seq), [r3(k_c, ctx_len), r3(k_l, seq)],
                            [r3(v_c, ctx_len), r3(v_l, seq)], tq=256)
            x_lat = _resid_proj(o_l.reshape(batch * seq, d), w_o, zero_bias, x_lat, lat[2], tm=tm)
            if update_ctx:
                o_c = _mla_attn(r3(q_c, ctx_len), [r3(k_c, ctx_len)], [r3(v_c, ctx_len)], tq=ctx_len)
                x_ctx = _resid_proj(o_c.reshape(batch * ctx_len, d), w_o, zero_bias, x_ctx, cm[2],
                                    tm=ctx_len)
        w1 = mlp_w1[i].astype(BF16)
        w2 = mlp_w2[i].astype(BF16)
        x_lat = _mlp(x_lat, g_mlp, lat[3], lat[4], lat[5], w1, w2, final_g2,
                     final_norm=(i == depth - 1), tm=tm)
        if update_ctx:
            x_ctx = _mlp(x_ctx, g_mlp, cm[3], cm[4], cm[5], w1, w2, final_g2, final_norm=False,
                         tm=ctx_len)
    return x_lat.reshape(batch, seq, d)
```

```python
import functools
import math

import numpy as np
import jax
import jax.numpy as jnp
from jax import lax
from jax.experimental import pallas as pl
from jax.experimental.pallas import tpu as pltpu

F32 = jnp.float32
BF16 = jnp.bfloat16

GRID_W = 64
RMS_EPS = 1e-6
ROPE_BASE = 10000.0
FOURIER_GROUPS = 4
DIFF_HEAD_DIM = 64
DIFF_V_DIM = 128
MLA_HEADS = 16
MLA_NOPE_DIM = 64
MLA_ROPE_DIM = 32
MLA_V_DIM = 64
MLA_Q_RANK = 256
MLA_KV_RANK = 128
N_MIXERS = 3

LANES = 128
FFT_RADIX = 64
VMEM_LIMIT = 56 * 1024 * 1024
LOG2E = 1.4426950408889634


def _cparams(n_axes):
    return pltpu.CompilerParams(
        dimension_semantics=("arbitrary",) * n_axes, vmem_limit_bytes=VMEM_LIMIT)


def _resident(shape):
    nd = len(shape)
    return pl.BlockSpec(shape, lambda *_: (0,) * nd, pipeline_mode=pl.Buffered(1))


def _norm_mod(x, g, sh, sc):
    r = lax.rsqrt(jnp.mean(x * x, axis=-1, keepdims=True) + RMS_EPS)
    return ((x * r) * g) * (1.0 + sc) + sh


def _rms(x, g):
    r = lax.rsqrt(jnp.mean(x * x, axis=-1, keepdims=True) + RMS_EPS)
    return (x * r) * g


def _rope_tile(x, cos, sin, half):
    lane = lax.broadcasted_iota(jnp.int32, x.shape, 1)
    fwd = pltpu.roll(x, LANES - half, 1)
    bwd = pltpu.roll(x, half, 1)
    partner = jnp.where((lane & half) == 0, fwd, bwd)
    return x * cos + partner * sin


def _mod_kernel(c_ref, w_ref, b_ref, o_ref):
    c = c_ref[...]
    s = (c / (1.0 + jnp.exp(-c))).astype(BF16)
    o_ref[0] = jnp.dot(s, w_ref[0].astype(BF16), preferred_element_type=F32) + b_ref[0]


def _modulation(cc, mod_w, mod_b):
    depth, d, n = mod_w.shape
    rows = cc.shape[0]
    tn = n // 4
    return pl.pallas_call(
        _mod_kernel,
        grid=(depth, n // tn),
        in_specs=[
            pl.BlockSpec((rows, d), lambda i, j: (0, 0)),
            pl.BlockSpec((1, d, tn), lambda i, j: (i, 0, j)),
            pl.BlockSpec((1, 1, tn), lambda i, j: (i, 0, j)),
        ],
        out_specs=pl.BlockSpec((1, rows, tn), lambda i, j: (i, 0, j)),
        out_shape=jax.ShapeDtypeStruct((depth, rows, n), F32),
        compiler_params=_cparams(2),
        name="modulation",
    )(cc, mod_w, mod_b.reshape(depth, 1, n))


def _mod_spec(tiles_per_batch, d):
    return pl.BlockSpec((1, 1, d), lambda i: (i // tiles_per_batch, 0, 0))


def _mlp_kernel(x_ref, g_ref, sh_ref, sc_ref, gt_ref, w1_ref, w2_ref, fg_ref, o_ref, *, ff_chunk,
                final_norm):
    x = x_ref[...]
    h = _norm_mod(x, g_ref[...], sh_ref[0], sc_ref[0]).astype(BF16)
    d_ff = w1_ref.shape[1]
    acc = jnp.zeros(x.shape, F32)
    for c in range(d_ff // ff_chunk):
        u = jnp.dot(h, w1_ref[:, c * ff_chunk:(c + 1) * ff_chunk], preferred_element_type=F32)
        u = jnp.square(jnp.maximum(u, 0.0)).astype(BF16)
        acc = acc + jnp.dot(u, w2_ref[c * ff_chunk:(c + 1) * ff_chunk, :],
                            preferred_element_type=F32)
    out = x + gt_ref[0] * acc
    if final_norm:
        out = _rms(out, fg_ref[...])
    o_ref[...] = out


def _mlp(x, g, sh, sc, gate, w1, w2, final_g, *, final_norm, tm):
    t, d = x.shape
    nb = sh.shape[0]
    tpb = t // nb // tm
    d_ff = w1.shape[1]
    return pl.pallas_call(
        functools.partial(_mlp_kernel, ff_chunk=1024, final_norm=final_norm),
        grid=(t // tm,),
        in_specs=[
            pl.BlockSpec((tm, d), lambda i: (i, 0)),
            _resident((1, d)),
            _mod_spec(tpb, d), _mod_spec(tpb, d), _mod_spec(tpb, d),
            _resident((d, d_ff)), _resident((d_ff, d)),
            _resident((1, d)),
        ],
        out_specs=pl.BlockSpec((tm, d), lambda i: (i, 0)),
        out_shape=jax.ShapeDtypeStruct((t, d), F32),
        compiler_params=_cparams(1),
        name="mlp",
    )(x, g, sh, sc, gate, w1, w2, final_g)


def _resid_kernel(a_ref, w_ref, b_ref, x_ref, gt_ref, o_ref):
    y = jnp.dot(a_ref[...], w_ref[...], preferred_element_type=F32) + b_ref[...]
    o_ref[...] = x_ref[...] + gt_ref[0] * y


def _resid_proj(a, w, bias, x, gate, *, tm):
    t, k = a.shape
    d = w.shape[1]
    nb = gate.shape[0]
    tpb = t // nb // tm
    return pl.pallas_call(
        _resid_kernel,
        grid=(t // tm,),
        in_specs=[
            pl.BlockSpec((tm, k), lambda i: (i, 0)),
            _resident((k, d)), _resident((1, d)),
            pl.BlockSpec((tm, d), lambda i: (i, 0)),
            _mod_spec(tpb, d),
        ],
        out_specs=pl.BlockSpec((tm, d), lambda i: (i, 0)),
        out_shape=jax.ShapeDtypeStruct((t, d), F32),
        compiler_params=_cparams(1),
        name="resid_proj",
    )(a, w, bias, x, gate)


def _dft_mats():
    cg = 256
    j = np.arange(cg)
    th = 2.0 * np.pi * np.outer(j, j) / cg
    wc = np.concatenate([np.cos(th), -np.sin(th)], axis=1) / 16.0
    ctx_seq = np.concatenate([np.cos(th), np.sin(th)], axis=1) / 16.0
    r = np.arange(FFT_RADIX)
    ph = 2.0 * np.pi * np.outer(r, r) / FFT_RADIX
    dr, di = np.cos(ph) / 8.0, -np.sin(ph) / 8.0
    m1 = np.block([[dr, -di], [di, dr]])
    m2 = np.concatenate([dr, -di], axis=1)
    tw = 2.0 * np.pi * np.outer(r, r) / (FFT_RADIX * FFT_RADIX)
    tr = np.repeat(np.cos(tw)[:, :, None], LANES, axis=2)
    ti = np.repeat(-np.sin(tw)[:, :, None], LANES, axis=2)
    as_bf = lambda a: jnp.asarray(a, F32).astype(BF16)
    return dict(wc=as_bf(wc), ctx_seq=as_bf(ctx_seq), m1=as_bf(m1), m2=as_bf(m2),
                tr=jnp.asarray(tr, F32), ti=jnp.asarray(ti, F32))


def _chan_dft(h, wc_ref):
    cg = wc_ref.shape[0]
    res, ims = [], []
    for grp in range(h.shape[1] // cg):
        z = jnp.dot(h[:, grp * cg:(grp + 1) * cg], wc_ref[...], preferred_element_type=F32)
        res.append(z[:, :cg])
        ims.append(z[:, cg:])
    return jnp.concatenate(res, axis=1), jnp.concatenate(ims, axis=1)


def _fourier_in_kernel(x_ref, g_ref, sh_ref, sc_ref, wc_ref, zr_ref, zi_ref):
    h = _norm_mod(x_ref[...], g_ref[...], sh_ref[0], sc_ref[0]).astype(BF16)
    zr, zi = _chan_dft(h, wc_ref)
    zr_ref[...] = zr.astype(BF16)
    zi_ref[...] = zi.astype(BF16)


def _fft_stage1_kernel(zr_ref, zi_ref, m1_ref, ar_ref, ai_ref):
    z = jnp.concatenate([zr_ref[0], zi_ref[0]], axis=0)
    a = jnp.dot(m1_ref[...], z, preferred_element_type=F32)
    r = a.shape[0] // 2
    ar_ref[0] = a[:r].astype(BF16)
    ai_ref[0] = a[r:].astype(BF16)


def _fft_stage2_kernel(ar_ref, ai_ref, tr_ref, ti_ref, m2_ref, o_ref):
    reps = ar_ref.shape[3] // LANES
    for j in range(ar_ref.shape[1]):
        re = ar_ref[0, j].astype(F32)
        im = ai_ref[0, j].astype(F32)
        tr = jnp.tile(tr_ref[j], (1, reps))
        ti = jnp.tile(ti_ref[j], (1, reps))
        z = jnp.concatenate([re * tr - im * ti, re * ti + im * tr], axis=0).astype(BF16)
        o_ref[0, j] = jnp.dot(m2_ref[...], z, preferred_element_type=F32).astype(BF16)


def _fourier_lat(x, g, sh, sc, mats, *, batch, tm):
    t, d = x.shape
    s = t // batch
    tpb = s // tm
    zr, zi = pl.pallas_call(
        _fourier_in_kernel,
        grid=(t // tm,),
        in_specs=[
            pl.BlockSpec((tm, d), lambda i: (i, 0)),
            _resident((1, d)), _mod_spec(tpb, d), _mod_spec(tpb, d),
            _resident(mats["wc"].shape),
        ],
        out_specs=[pl.BlockSpec((tm, d), lambda i: (i, 0))] * 2,
        out_shape=[jax.ShapeDtypeStruct((t, d), BF16)] * 2,
        compiler_params=_cparams(1),
        name="fourier_in",
    )(x, g, sh, sc, mats["wc"])

    r = FFT_RADIX
    cols = (s // r) * d
    tn = 8192
    blk = pl.BlockSpec((1, r, tn), lambda b, j: (b, 0, j))
    ar, ai = pl.pallas_call(
        _fft_stage1_kernel,
        grid=(batch, cols // tn),
        in_specs=[blk, blk, pl.BlockSpec(mats["m1"].shape, lambda b, j: (0, 0))],
        out_specs=[blk, blk],
        out_shape=[jax.ShapeDtypeStruct((batch, r, cols), BF16)] * 2,
        compiler_params=_cparams(2),
        name="fft_stage1",
    )(zr.reshape(batch, r, cols), zi.reshape(batch, r, cols), mats["m1"])

    kg = 8
    blk4 = pl.BlockSpec((1, kg, r, d), lambda b, j: (b, j, 0, 0))
    tw = pl.BlockSpec((kg, r, LANES), lambda b, j: (j, 0, 0))
    frev = pl.pallas_call(
        _fft_stage2_kernel,
        grid=(batch, r // kg),
        in_specs=[blk4, blk4, tw, tw, pl.BlockSpec(mats["m2"].shape, lambda b, j: (0, 0))],
        out_specs=blk4,
        out_shape=jax.ShapeDtypeStruct((batch, r, r, d), BF16),
        compiler_params=_cparams(2),
        name="fft_stage2",
    )(ar.reshape(batch, r, r, d), ai.reshape(batch, r, r, d), mats["tr"], mats["ti"], mats["m2"])
    return jnp.swapaxes(frev, 1, 2).reshape(t, d)


def _fourier_ctx_kernel(x_ref, g_ref, sh_ref, sc_ref, gt_ref, wc_ref, seq_ref, w_ref, b_ref, o_ref):
    x = x_ref[...]
    h = _norm_mod(x, g_ref[...], sh_ref[0], sc_ref[0]).astype(BF16)
    zr, zi = _chan_dft(h, wc_ref)
    z = jnp.concatenate([zr, zi], axis=0).astype(BF16)
    f = jnp.dot(seq_ref[...], z, preferred_element_type=F32).astype(BF16)
    y = jnp.dot(f, w_ref[...], preferred_element_type=F32) + b_ref[...]
    o_ref[...] = x + gt_ref[0] * y


def _fourier_ctx(x, g, sh, sc, gate, mats, w, bias, *, ctx_len):
    t, d = x.shape
    n_tiles = t // ctx_len
    return pl.pallas_call(
        _fourier_ctx_kernel,
        grid=(n_tiles,),
        in_specs=[
            pl.BlockSpec((ctx_len, d), lambda i: (i, 0)),
            _resident((1, d)), _mod_spec(n_tiles, d), _mod_spec(n_tiles, d), _mod_spec(n_tiles, d),
            _resident(mats["wc"].shape), _resident(mats["ctx_seq"].shape),
            _resident(w.shape), _resident((1, d)),
        ],
        out_specs=pl.BlockSpec((ctx_len, d), lambda i: (i, 0)),
        out_shape=jax.ShapeDtypeStruct((t, d), F32),
        compiler_params=_cparams(1),
        name="fourier_ctx",
    )(x, g, sh, sc, gate, mats["wc"], mats["ctx_seq"], w, bias)


def _rope_tables(rows, dim, lane_of_dim):
    n_freq = dim // 4
    inv_freq = ROPE_BASE ** (-jnp.arange(n_freq, dtype=F32) / n_freq)
    t = jnp.arange(rows * GRID_W)
    row = (t // GRID_W).astype(F32)
    col = (t % GRID_W).astype(F32)
    ang = jnp.stack([row[:, None] * inv_freq, col[:, None] * inv_freq], axis=1)
    cos, sin = jnp.cos(ang), jnp.sin(ang)
    dims = np.array([lane_of_dim(l) for l in range(LANES)])
    live = dims >= 0
    dd = np.where(live, dims, 0)
    axis, pair, freq = dd // (2 * n_freq), (dd % (2 * n_freq)) // n_freq, dd % n_freq
    sign = np.where(pair == 0, -1.0, 1.0).astype(np.float32)
    cos_t = jnp.where(live[None, :], cos[:, axis, freq], 1.0)
    sin_t = jnp.where(live[None, :], sin[:, axis, freq] * sign[None, :], 0.0)
    return cos_t.astype(F32), sin_t.astype(F32)


def _scores(q, k_refs):
    nt = (((1,), (1,)), ((), ()))
    return [lax.dot_general(q, k[0], nt, preferred_element_type=F32) for k in k_refs]


def _attend(q, k_refs, vaug_refs):
    parts = _scores(q, k_refs)
    m = functools.reduce(jnp.maximum, [jnp.max(p, axis=-1, keepdims=True) for p in parts])
    acc = None
    for p, v in zip(parts, vaug_refs):
        e = jnp.exp2(p - m).astype(BF16)
        part = jnp.dot(e, v[0], preferred_element_type=F32)
        acc = part if acc is None else acc + part
    return acc[:, :LANES] / acc[:, LANES:]


def _pipe_attn_kernel(*refs, mode, n_seg, lam_init):
    n_par = 5 if mode == "diff" else 0
    par = refs[:n_par]
    q_ref = refs[n_par]
    k_refs = refs[n_par + 1:n_par + 1 + n_seg]
    v_refs = refs[n_par + 1 + n_seg:n_par + 1 + 2 * n_seg]
    o_ref, s_a, s_b, m_a, m_b = refs[n_par + 1 + 2 * n_seg:]
    t = pl.program_id(0)
    nt = (((1,), (1,)), ((), ()))

    def scores(s_w, m_w):
        q = q_ref[0]
        if mode == "diff":
            lane = lax.broadcasted_iota(jnp.int32, q.shape, 1)
            zero = jnp.zeros_like(q)
            qs = [jnp.where(lane < DIFF_HEAD_DIM, q, zero), jnp.where(lane >= DIFF_HEAD_DIM, q, zero)]
        else:
            qs = [q[:, :LANES], q[:, LANES:]]
        for c in range(2):
            m, off = None, 0
            for k in k_refs:
                kk = k[0] if mode == "diff" else k[0, :, c * LANES:(c + 1) * LANES]
                p = lax.dot_general(qs[c], kk, nt, preferred_element_type=F32)
                s_w[c, :, off:off + p.shape[1]] = p
                off += p.shape[1]
                pm = jnp.max(p, axis=-1, keepdims=True)
                m = pm if m is None else jnp.maximum(m, pm)
            m_w[c] = jnp.broadcast_to(m, m_w.shape[1:])

    def values(s_r, m_r):
        outs = []
        for c in range(2):
            m = m_r[c][:, :1]
            acc, off = None, 0
            for v in v_refs:
                w = v.shape[1]
                e = jnp.exp2(s_r[c, :, off:off + w] - m).astype(BF16)
                part = jnp.dot(e, v[0], preferred_element_type=F32)
                acc = part if acc is None else acc + part
                off += w
            outs.append(acc[:, :LANES] / acc[:, LANES:])
        if mode == "diff":
            lq1, lk1, lq2, lk2, sg = par
            lam = (jnp.exp(jnp.sum(lq1[...] * lk1[...], axis=-1, keepdims=True))
                   - jnp.exp(jnp.sum(lq2[...] * lk2[...], axis=-1, keepdims=True)) + lam_init)
            o = outs[0] - lam * outs[1]
            o_ref[0] = (_rms(o, sg[...]) * (1.0 - lam_init)).astype(BF16)
        else:
            lane = lax.broadcasted_iota(jnp.int32, outs[0].shape, 1)
            o_ref[0] = jnp.where(lane < MLA_V_DIM, outs[0], outs[1]).astype(BF16)

    @pl.when(t == 0)
    def _():
        s_b[...] = jnp.zeros(s_b.shape, F32)
        m_b[...] = jnp.zeros(m_b.shape, F32)

    @pl.when(t % 2 == 0)
    def _():
        scores(s_a, m_a)
        values(s_b, m_b)

    @pl.when(t % 2 == 1)
    def _():
        scores(s_b, m_b)
        values(s_a, m_a)


def _pipe_attn(params, q_arr, k_arrs, v_arrs, *, mode, units, q_width, k_block, tq, lam_init=0.0):
    b, sq, _ = q_arr.shape
    nq = sq // tq
    n_items = b * units * nq
    n_seg = len(k_arrs)
    sk = sum(a.shape[1] for a in k_arrs)

    def item(t, lag):
        tt = jnp.clip(t - lag, 0, n_items - 1)
        return tt // (units * nq), (tt // nq) % units, tt % nq

    def q_map(t):
        bb, u, i = item(t, 0)
        return bb, i, u

    def k_map(t):
        bb, u, _ = item(t, 0)
        return bb, 0, k_block + u

    def v_map(t):
        bb, u, _ = item(t, 1)
        return bb, 0, k_block + u

    def o_map(t):
        bb, u, i = item(t, 1)
        return bb, i, u

    in_specs = [pl.BlockSpec(p.shape, lambda t: (0, 0)) for p in params]
    in_specs += [pl.BlockSpec((1, tq, q_width), q_map)]
    in_specs += [pl.BlockSpec((1, a.shape[1], q_width), k_map) for a in k_arrs]
    in_specs += [pl.BlockSpec((1, a.shape[1], 2 * LANES), v_map) for a in v_arrs]
    return pl.pallas_call(
        functools.partial(_pipe_attn_kernel, mode=mode, n_seg=n_seg, lam_init=lam_init),
        grid=(n_items + 1,),
        in_specs=in_specs,
        out_specs=pl.BlockSpec((1, tq, LANES), o_map),
        out_shape=jax.ShapeDtypeStruct((b, sq, units * LANES), BF16),
        scratch_shapes=[pltpu.VMEM((2, tq, sk), F32), pltpu.VMEM((2, tq, sk), F32),
                        pltpu.VMEM((2, tq, LANES), F32), pltpu.VMEM((2, tq, LANES), F32)],
        compiler_params=_cparams(1),
        name=mode + "_pipe_attn",
    )(*params, q_arr, *k_arrs, *v_arrs)


def _diff_qkv_kernel(x_ref, g_ref, sh_ref, sc_ref, w_ref, cos_ref, sin_ref, o_ref, *, rope, qscale):
    h = _norm_mod(x_ref[...], g_ref[...], sh_ref[0], sc_ref[0]).astype(BF16)
    d = x_ref.shape[1]
    acc = jnp.dot(h, w_ref[...], preferred_element_type=F32)
    heads = d // LANES
    ones = jnp.ones((x_ref.shape[0], LANES), BF16)
    for tile in range(2 * heads):
        a = acc[:, tile * LANES:(tile + 1) * LANES]
        if rope:
            a = _rope_tile(a, cos_ref[...], sin_ref[...], DIFF_HEAD_DIM // 4)
        if tile < heads:
            a = a * qscale
        o_ref[:, tile * LANES:(tile + 1) * LANES] = a.astype(BF16)
    for hd in range(heads):
        src = (2 * heads + hd) * LANES
        dst = (2 * heads + 2 * hd) * LANES
        o_ref[:, dst:dst + LANES] = acc[:, src:src + LANES].astype(BF16)
        o_ref[:, dst + LANES:dst + 2 * LANES] = ones


def _diff_qkv(x, g, sh, sc, w, cos, sin, *, rope, tm):
    t, d = x.shape
    nb = sh.shape[0]
    tpb = t // nb // tm
    spb = t // nb // tm
    n = w.shape[1] + d
    qscale = DIFF_HEAD_DIM ** -0.5 * LOG2E
    tab = pl.BlockSpec((tm, LANES), (lambda i: (i % spb, 0)) if rope else (lambda i: (0, 0)))
    return pl.pallas_call(
        functools.partial(_diff_qkv_kernel, rope=rope, qscale=qscale),
        grid=(t // tm,),
        in_specs=[
            pl.BlockSpec((tm, d), lambda i: (i, 0)),
            _resident((1, d)), _mod_spec(tpb, d), _mod_spec(tpb, d),
            _resident(w.shape), tab, tab,
        ],
        out_specs=pl.BlockSpec((tm, n), lambda i: (i, 0)),
        out_shape=jax.ShapeDtypeStruct((t, n), BF16),
        compiler_params=_cparams(1),
        name="diff_qkv",
    )(x, g, sh, sc, w, cos, sin)


def _diff_attn_kernel(lq1_ref, lk1_ref, lq2_ref, lk2_ref, sg_ref, q_ref, *refs, n_seg, lam_init):
    k_refs, v_refs, o_ref = refs[:n_seg], refs[n_seg:2 * n_seg], refs[2 * n_seg]
    lam = (jnp.exp(jnp.sum(lq1_ref[...] * lk1_ref[...], axis=-1, keepdims=True))
           - jnp.exp(jnp.sum(lq2_ref[...] * lk2_ref[...], axis=-1, keepdims=True)) + lam_init)
    q = q_ref[0]
    lane = lax.broadcasted_iota(jnp.int32, q.shape, 1)
    zero = jnp.zeros_like(q)
    o1 = _attend(jnp.where(lane < DIFF_HEAD_DIM, q, zero), k_refs, v_refs)
    o2 = _attend(jnp.where(lane >= DIFF_HEAD_DIM, q, zero), k_refs, v_refs)
    o = o1 - lam * o2
    o_ref[0] = (_rms(o, sg_ref[...]) * (1.0 - lam_init)).astype(BF16)


def _diff_attn(lams, subln_g, q_arr, kv_arrs, *, heads, tq, lam_init):
    b, sq, _ = q_arr.shape
    n_seg = len(kv_arrs)
    lam_specs = [pl.BlockSpec(l.shape, lambda bb, h, i: (0, 0)) for l in lams]
    in_specs = lam_specs + [
        pl.BlockSpec((1, DIFF_V_DIM), lambda bb, h, i: (0, 0)),
        pl.BlockSpec((1, tq, LANES), lambda bb, h, i: (bb, i, h)),
    ]
    in_specs += [pl.BlockSpec((1, a.shape[1], LANES), lambda bb, h, i: (bb, 0, heads + h))
                 for a in kv_arrs]
    in_specs += [pl.BlockSpec((1, a.shape[1], 2 * LANES), lambda bb, h, i: (bb, 0, heads + h))
                 for a in kv_arrs]
    return pl.pallas_call(
        functools.partial(_diff_attn_kernel, n_seg=n_seg, lam_init=lam_init),
        grid=(b, heads, sq // tq),
        in_specs=in_specs,
        out_specs=pl.BlockSpec((1, tq, LANES), lambda bb, h, i: (bb, i, h)),
        out_shape=jax.ShapeDtypeStruct((b, sq, heads * DIFF_V_DIM), BF16),
        compiler_params=_cparams(3),
        name="diff_attn",
    )(*lams, subln_g, q_arr, *kv_arrs, *kv_arrs)


def _mla_down_kernel(x_ref, g_ref, sh_ref, sc_ref, w_ref, qg_ref, kvg_ref, cos_ref, sin_ref,
                     cq_ref, ckv_ref, kr_ref, *, rope):
    h = _norm_mod(x_ref[...], g_ref[...], sh_ref[0], sc_ref[0]).astype(BF16)
    acc = jnp.dot(h, w_ref[...], preferred_element_type=F32)
    cq_ref[...] = _rms(acc[:, :MLA_Q_RANK], qg_ref[...]).astype(BF16)
    ckv_ref[...] = _rms(acc[:, MLA_Q_RANK:MLA_Q_RANK + MLA_KV_RANK], kvg_ref[...]).astype(BF16)
    kr = acc[:, MLA_Q_RANK + MLA_KV_RANK:]
    if rope:
        kr = _rope_tile(kr, cos_ref[...], sin_ref[...], MLA_ROPE_DIM // 4)
    kr_ref[...] = kr.astype(BF16)


def _mla_up_kernel(cq_ref, ckv_ref, kr_ref, wq_ref, wk_ref, wv_ref, cos_ref, sin_ref,
                   q_ref, k_ref, v_ref, *, rope, qscale):
    q = jnp.dot(cq_ref[...], wq_ref[...], preferred_element_type=F32)
    for hd in range(MLA_HEADS):
        a = q[:, hd * LANES:(hd + 1) * LANES]
        if rope:
            a = _rope_tile(a, cos_ref[...], sin_ref[...], MLA_ROPE_DIM // 4)
        q_ref[:, hd * LANES:(hd + 1) * LANES] = (a * qscale).astype(BF16)
    k = jnp.dot(ckv_ref[...], wk_ref[...], preferred_element_type=F32)
    kr = kr_ref[...].astype(F32)
    for hd in range(MLA_HEADS):
        k_ref[:, hd * LANES:(hd + 1) * LANES] = (k[:, hd * LANES:(hd + 1) * LANES] + kr).astype(BF16)
    v = jnp.dot(ckv_ref[...], wv_ref[...], preferred_element_type=F32)
    ones = jnp.ones((v.shape[0], LANES), BF16)
    for pair in range(MLA_HEADS // 2):
        v_ref[:, 2 * pair * LANES:(2 * pair + 1) * LANES] = v[:, pair * LANES:(pair + 1) * LANES].astype(BF16)
        v_ref[:, (2 * pair + 1) * LANES:(2 * pair + 2) * LANES] = ones


def _mla_project(x, g, sh, sc, w_down, qg, kvg, wq, wk, wv, cos, sin, *, rope, tm):
    t, d = x.shape
    nb = sh.shape[0]
    tpb = t // nb // tm
    tab = pl.BlockSpec((tm, LANES), (lambda i: (i % tpb, 0)) if rope else (lambda i: (0, 0)))
    row = lambda n: pl.BlockSpec((tm, n), lambda i: (i, 0))
    cq, ckv, kr = pl.pallas_call(
        functools.partial(_mla_down_kernel, rope=rope),
        grid=(t // tm,),
        in_specs=[row(d), _resident((1, d)), _mod_spec(tpb, d), _mod_spec(tpb, d),
                  _resident(w_down.shape), _resident(qg.shape), _resident(kvg.shape), tab, tab],
        out_specs=[row(MLA_Q_RANK), row(MLA_KV_RANK), row(LANES)],
        out_shape=[jax.ShapeDtypeStruct((t, MLA_Q_RANK), BF16),
                   jax.ShapeDtypeStruct((t, MLA_KV_RANK), BF16),
                   jax.ShapeDtypeStruct((t, LANES), BF16)],
        compiler_params=_cparams(1),
        name="mla_down",
    )(x, g, sh, sc, w_down, qg, kvg, cos, sin)
    qscale = (MLA_NOPE_DIM + MLA_ROPE_DIM) ** -0.5 * LOG2E
    nq, nv = wq.shape[1], 2 * wv.shape[1]
    return pl.pallas_call(
        functools.partial(_mla_up_kernel, rope=rope, qscale=qscale),
        grid=(t // tm,),
        in_specs=[row(MLA_Q_RANK), row(MLA_KV_RANK), row(LANES),
                  _resident(wq.shape), _resident(wk.shape), _resident(wv.shape), tab, tab],
        out_specs=[row(nq), row(nq), row(nv)],
        out_shape=[jax.ShapeDtypeStruct((t, nq), BF16), jax.ShapeDtypeStruct((t, nq), BF16),
                   jax.ShapeDtypeStruct((t, nv), BF16)],
        compiler_params=_cparams(1),
        name="mla_up",
    )(cq, ckv, kr, wq, wk, wv, cos, sin)


def _mla_attn_kernel(q_ref, *refs, n_seg):
    k_refs, v_refs, o_ref = refs[:n_seg], refs[n_seg:2 * n_seg], refs[2 * n_seg]
    outs = []
    for hd in range(2):
        q = q_ref[0, :, hd * LANES:(hd + 1) * LANES]
        ks = [k.at[:, :, hd * LANES:(hd + 1) * LANES] for k in k_refs]
        outs.append(_attend(q, ks, v_refs))
    lane = lax.broadcasted_iota(jnp.int32, outs[0].shape, 1)
    o_ref[0] = jnp.where(lane < MLA_V_DIM, outs[0], outs[1]).astype(BF16)


def _mla_attn(q, k_arrs, v_arrs, *, tq):
    b, sq, nq = q.shape
    pairs = nq // (2 * LANES)
    n_seg = len(k_arrs)
    in_specs = [pl.BlockSpec((1, tq, 2 * LANES), lambda bb, h, i: (bb, i, h))]
    in_specs += [pl.BlockSpec((1, a.shape[1], 2 * LANES), lambda bb, h, i: (bb, 0, h)) for a in k_arrs]
    in_specs += [pl.BlockSpec((1, a.shape[1], 2 * LANES), lambda bb, h, i: (bb, 0, h)) for a in v_arrs]
    return pl.pallas_call(
        functools.partial(_mla_attn_kernel, n_seg=n_seg),
        grid=(b, pairs, sq // tq),
        in_specs=in_specs,
        out_specs=pl.BlockSpec((1, tq, LANES), lambda bb, h, i: (bb, i, h)),
        out_shape=jax.ShapeDtypeStruct((b, sq, pairs * LANES), BF16),
        compiler_params=_cparams(3),
        name="mla_attn",
    )(q, *k_arrs, *v_arrs)


def _mla_weights(w_down, w_uq, w_ukv):
    d = w_down.shape[0]
    hq = MLA_NOPE_DIM + MLA_ROPE_DIM
    pad_r = LANES - hq
    lo = MLA_Q_RANK + MLA_KV_RANK
    zeros = lambda n: jnp.zeros((d, n), w_down.dtype)
    wd = jnp.concatenate([w_down[:, :lo], zeros(MLA_NOPE_DIM), w_down[:, lo:], zeros(pad_r)], axis=1)
    wq = w_uq.reshape(MLA_Q_RANK, MLA_HEADS, hq)
    wq = jnp.pad(wq, ((0, 0), (0, 0), (0, pad_r))).reshape(MLA_Q_RANK, MLA_HEADS * LANES)
    wkv = w_ukv.reshape(MLA_KV_RANK, MLA_HEADS, MLA_NOPE_DIM + MLA_V_DIM)
    wk = jnp.pad(wkv[:, :, :MLA_NOPE_DIM], ((0, 0), (0, 0), (0, LANES - MLA_NOPE_DIM)))
    wk = wk.reshape(MLA_KV_RANK, MLA_HEADS * LANES)
    wv = wkv[:, :, MLA_NOPE_DIM:].reshape(MLA_KV_RANK, MLA_HEADS * MLA_V_DIM)
    return wd.astype(BF16), wq.astype(BF16), wk.astype(BF16), wv.astype(BF16)


def kernel(x, c, ctx, c_ctx, mod_w, mod_b, norm_mix_g, norm_mlp_g, final_g, mlp_w1, mlp_w2, fourier_w, fourier_b, diff_w_qkv, diff_lambda_q1, diff_lambda_k1, diff_lambda_q2, diff_lambda_k2, diff_subln_g, diff_w_o, mla_w_down, mla_q_norm_g, mla_kv_norm_g, mla_w_uq, mla_w_ukv, mla_w_o):
    batch, seq, d = x.shape
    ctx_len = ctx.shape[1]
    depth = mod_w.shape[0]
    rows = seq // GRID_W
    assert seq == FFT_RADIX * FFT_RADIX and d == FOURIER_GROUPS * 256 and ctx_len == 256
    tm = 512

    cc_rows = -(-(batch + 1) // 8) * 8
    cc = jnp.zeros((cc_rows, d), F32).at[:batch].set(c).at[batch].set(c_ctx)
    mods = _modulation(cc, mod_w, mod_b).reshape(depth, cc_rows, 6, d)
    mats = _dft_mats()
    zero_bias = jnp.zeros((1, d), F32)
    final_g2 = final_g.reshape(1, d)

    lane_diff = lambda l: l % DIFF_HEAD_DIM
    cos_d, sin_d = _rope_tables(rows, DIFF_HEAD_DIM, lane_diff)
    lane_mla = lambda l: l - MLA_NOPE_DIM if MLA_NOPE_DIM <= l < MLA_NOPE_DIM + MLA_ROPE_DIM else -1
    cos_m, sin_m = _rope_tables(rows, MLA_ROPE_DIM, lane_mla)

    x_lat = x.reshape(batch * seq, d)
    x_ctx = ctx.reshape(batch * ctx_len, d)
    for i in range(depth):
        kind, j = i % N_MIXERS, i // N_MIXERS
        update_ctx = i < depth - 1
        ctx_feeds_mixer = update_ctx or kind != 0
        lat = [mods[i, :batch, k].reshape(batch, 1, d) for k in range(6)]
        cm = [mods[i, batch:batch + 1, k].reshape(1, 1, d) for k in range(6)]
        g_mix = norm_mix_g[i].reshape(1, d)
        g_mlp = norm_mlp_g[i].reshape(1, d)
        if kind == 0:
            w = fourier_w[j].astype(BF16)
            bias = fourier_b[j].reshape(1, d)
            f = _fourier_lat(x_lat, g_mix, lat[0], lat[1], mats, batch=batch, tm=tm)
            x_lat = _resid_proj(f, w, bias, x_lat, lat[2], tm=tm)
            if update_ctx:
                x_ctx = _fourier_ctx(x_ctx, g_mix, cm[0], cm[1], cm[2], mats, w, bias, ctx_len=ctx_len)
        elif kind == 1:
            heads = d // DIFF_V_DIM
            w_qkv = diff_w_qkv[j].astype(BF16)
            w_o = diff_w_o[j].astype(BF16)
            lam_init = 0.8 - 0.6 * math.exp(-0.3 * i)
            lams = [v[j].reshape(1, DIFF_HEAD_DIM) for v in
                    (diff_lambda_q1, diff_lambda_k1, diff_lambda_q2, diff_lambda_k2)]
            sg = diff_subln_g[j].reshape(1, DIFF_V_DIM)
            qkv_l = _diff_qkv(x_lat, g_mix, lat[0], lat[1], w_qkv, cos_d, sin_d, rope=True, tm=tm)
            qkv_l = qkv_l.reshape(batch, seq, 4 * d)
            qkv_c = _diff_qkv(x_ctx, g_mix, cm[0], cm[1], w_qkv, cos_d, sin_d, rope=False, tm=ctx_len)
            qkv_c = qkv_c.reshape(batch, ctx_len, 4 * d)
            o_l = _pipe_attn(lams + [sg], qkv_l, [qkv_c, qkv_l], [qkv_c, qkv_l], mode="diff",
                             units=heads, q_width=LANES, k_block=heads, tq=256, lam_init=lam_init)
            x_lat = _resid_proj(o_l.reshape(batch * seq, d), w_o, zero_bias, x_lat, lat[2], tm=tm)
            if update_ctx:
                o_c = _diff_attn(lams, sg, qkv_c, [qkv_c], heads=heads, tq=ctx_len, lam_init=lam_init)
                x_ctx = _resid_proj(o_c.reshape(batch * ctx_len, d), w_o, zero_bias, x_ctx, cm[2],
                                    tm=ctx_len)
        else:
            wd, wq, wk, wv = _mla_weights(mla_w_down[j], mla_w_uq[j], mla_w_ukv[j])
            w_o = mla_w_o[j].astype(BF16)
            qg = mla_q_norm_g[j].reshape(1, MLA_Q_RANK)
            kvg = mla_kv_norm_g[j].reshape(1, MLA_KV_RANK)
            q_l, k_l, v_l = _mla_project(x_lat, g_mix, lat[0], lat[1], wd, qg, kvg, wq, wk, wv,
                                         cos_m, sin_m, rope=True, tm=tm)
            q_c, k_c, v_c = _mla_project(x_ctx, g_mix, cm[0], cm[1], wd, qg, kvg, wq, wk, wv,
                                         cos_m, sin_m, rope=False, tm=ctx_len)
            r3 = lambda a, s: a.reshape(batch, s, a.shape[1])
            o_l = _pipe_attn([], r3(q_l, seq), [r3(k_c, ctx_len), r3(k_l, seq)],
                             [r3(v_c, ctx_len), r3(v_l, seq)], mode="mla", units=MLA_HEADS // 2,
                             q_width=2 * LANES, k_block=0, tq=256)
            x_lat = _resid_proj(o_l.reshape(batch * seq, d), w_o, zero_bias, x_lat, lat[2], tm=tm)
            if update_ctx:
                o_c = _mla_attn(r3(q_c, ctx_len), [r3(k_c, ctx_len)], [r3(v_c, ctx_len)], tq=ctx_len)
                x_ctx = _resid_proj(o_c.reshape(batch * ctx_len, d), w_o, zero_bias, x_ctx, cm[2],
                                    tm=ctx_len)
        w1 = mlp_w1[i].astype(BF16)
        w2 = mlp_w2[i].astype(BF16)
        x_lat = _mlp(x_lat, g_mlp, lat[3], lat[4], lat[5], w1, w2, final_g2,
                     final_norm=(i == depth - 1), tm=tm)
        if update_ctx:
            x_ctx = _mlp(x_ctx, g_mlp, cm[3], cm[4], cm[5], w1, w2, final_g2, final_norm=False,
                         tm=ctx_len)
    return x_lat.reshape(batch, seq, d)
```

```python
import functools
import math

import numpy as np
import jax
import jax.numpy as jnp
from jax import lax
from jax.experimental import pallas as pl
from jax.experimental.pallas import tpu as pltpu

F32 = jnp.float32
BF16 = jnp.bfloat16

GRID_W = 64
RMS_EPS = 1e-6
ROPE_BASE = 10000.0
FOURIER_GROUPS = 4
DIFF_HEAD_DIM = 64
DIFF_V_DIM = 128
MLA_HEADS = 16
MLA_NOPE_DIM = 64
MLA_ROPE_DIM = 32
MLA_V_DIM = 64
MLA_Q_RANK = 256
MLA_KV_RANK = 128
N_MIXERS = 3

LANES = 128
FFT_RADIX = 64
VMEM_LIMIT = 56 * 1024 * 1024
LOG2E = 1.4426950408889634


def _cparams(n_axes):
    return pltpu.CompilerParams(
        dimension_semantics=("arbitrary",) * n_axes, vmem_limit_bytes=VMEM_LIMIT)


def _resident(shape):
    nd = len(shape)
    return pl.BlockSpec(shape, lambda *_: (0,) * nd, pipeline_mode=pl.Buffered(1))


def _norm_mod(x, g, sh, sc):
    r = lax.rsqrt(jnp.mean(x * x, axis=-1, keepdims=True) + RMS_EPS)
    return ((x * r) * g) * (1.0 + sc) + sh


def _rms(x, g):
    r = lax.rsqrt(jnp.mean(x * x, axis=-1, keepdims=True) + RMS_EPS)
    return (x * r) * g


def _rope_tile(x, cos, sin, half):
    lane = lax.broadcasted_iota(jnp.int32, x.shape, 1)
    fwd = pltpu.roll(x, LANES - half, 1)
    bwd = pltpu.roll(x, half, 1)
    partner = jnp.where((lane & half) == 0, fwd, bwd)
    return x * cos + partner * sin


def _mod_kernel(c_ref, w_ref, b_ref, o_ref):
    c = c_ref[...]
    s = (c / (1.0 + jnp.exp(-c))).astype(BF16)
    o_ref[0] = jnp.dot(s, w_ref[0].astype(BF16), preferred_element_type=F32) + b_ref[0]


def _modulation(cc, mod_w, mod_b):
    depth, d, n = mod_w.shape
    rows = cc.shape[0]
    tn = n // 4
    return pl.pallas_call(
        _mod_kernel,
        grid=(depth, n // tn),
        in_specs=[
            pl.BlockSpec((rows, d), lambda i, j: (0, 0)),
            pl.BlockSpec((1, d, tn), lambda i, j: (i, 0, j)),
            pl.BlockSpec((1, 1, tn), lambda i, j: (i, 0, j)),
        ],
        out_specs=pl.BlockSpec((1, rows, tn), lambda i, j: (i, 0, j)),
        out_shape=jax.ShapeDtypeStruct((depth, rows, n), F32),
        compiler_params=_cparams(2),
        name="modulation",
    )(cc, mod_w, mod_b.reshape(depth, 1, n))


def _mod_spec(tiles_per_batch, d):
    return pl.BlockSpec((1, 1, d), lambda i: (i // tiles_per_batch, 0, 0))


def _block_tail_kernel(a_ref, wo_ref, bo_ref, ga_ref, x_ref, g_ref, sh_ref, sc_ref, gm_ref, w1_ref,
                       w2_ref, fg_ref, o_ref, *, ff_chunk, final_norm):
    y = jnp.dot(a_ref[...].astype(BF16), wo_ref[...], preferred_element_type=F32) + bo_ref[...]
    x = x_ref[...] + ga_ref[0] * y
    h = _norm_mod(x, g_ref[...], sh_ref[0], sc_ref[0]).astype(BF16)
    d_ff = w1_ref.shape[1]
    acc = jnp.zeros(x.shape, F32)
    for c in range(d_ff // ff_chunk):
        u = jnp.dot(h, w1_ref[:, c * ff_chunk:(c + 1) * ff_chunk], preferred_element_type=F32)
        u = jnp.square(jnp.maximum(u, 0.0)).astype(BF16)
        acc = acc + jnp.dot(u, w2_ref[c * ff_chunk:(c + 1) * ff_chunk, :],
                            preferred_element_type=F32)
    out = x + gm_ref[0] * acc
    if final_norm:
        out = _rms(out, fg_ref[...])
    o_ref[...] = out


def _block_tail(a, w_o, b_o, gate_a, x, g, sh, sc, gate_m, w1, w2, final_g, *, final_norm, tm):
    t, d = x.shape
    nb = sh.shape[0]
    tpb = t // nb // tm
    row = lambda n: pl.BlockSpec((tm, n), lambda i: (i, 0))
    return pl.pallas_call(
        functools.partial(_block_tail_kernel, ff_chunk=1024, final_norm=final_norm),
        grid=(t // tm,),
        in_specs=[
            row(a.shape[1]), _resident(w_o.shape), _resident((1, d)), _mod_spec(tpb, d),
            row(d), _resident((1, d)),
            _mod_spec(tpb, d), _mod_spec(tpb, d), _mod_spec(tpb, d),
            _resident(w1.shape), _resident(w2.shape),
            _resident((1, d)),
        ],
        out_specs=row(d),
        out_shape=jax.ShapeDtypeStruct((t, d), F32),
        compiler_params=_cparams(1),
        name="block_tail",
    )(a, w_o, b_o, gate_a, x, g, sh, sc, gate_m, w1, w2, final_g)


def _dft_mats():
    cg = 256
    j = np.arange(cg)
    th = 2.0 * np.pi * np.outer(j, j) / cg
    wc = np.concatenate([np.cos(th), -np.sin(th)], axis=1) / 16.0
    ctx_seq = np.concatenate([np.cos(th), np.sin(th)], axis=1) / 16.0
    r = np.arange(FFT_RADIX)
    ph = 2.0 * np.pi * np.outer(r, r) / FFT_RADIX
    dr, di = np.cos(ph) / 8.0, -np.sin(ph) / 8.0
    m1 = np.block([[dr, -di], [di, dr]])
    m2 = np.concatenate([dr, -di], axis=1)
    eye = np.eye(FFT_GROUP)
    m1 = np.einsum("pq,jk->jpqk", m1, eye).reshape(m1.shape[0] * FFT_GROUP, m1.shape[1] * FFT_GROUP)
    m2 = np.einsum("pq,jk->pjqk", m2, eye).reshape(m2.shape[0] * FFT_GROUP, m2.shape[1] * FFT_GROUP)
    tw = 2.0 * np.pi * np.outer(r, r) / (FFT_RADIX * FFT_RADIX)
    tr = np.repeat(np.cos(tw)[:, :, None], LANES, axis=2)
    ti = np.repeat(-np.sin(tw)[:, :, None], LANES, axis=2)
    as_bf = lambda a: jnp.asarray(a, F32).astype(BF16)
    return dict(wc=as_bf(wc), ctx_seq=as_bf(ctx_seq), m1=as_bf(m1), m2=as_bf(m2),
                tr=jnp.asarray(tr, F32), ti=jnp.asarray(ti, F32))


def _chan_dft(h, wc_ref):
    cg = wc_ref.shape[0]
    res, ims = [], []
    for grp in range(h.shape[1] // cg):
        z = jnp.dot(h[:, grp * cg:(grp + 1) * cg], wc_ref[...], preferred_element_type=F32)
        res.append(z[:, :cg])
        ims.append(z[:, cg:])
    return jnp.concatenate(res, axis=1), jnp.concatenate(ims, axis=1)


def _pack_complex(re, im):
    bits = lambda v: lax.bitcast_convert_type(v.astype(BF16).astype(F32), jnp.uint32)
    return (bits(re) >> 16) | bits(im)


def _unpack_complex(w):
    re = lax.bitcast_convert_type(w << 16, F32)
    im = lax.bitcast_convert_type(w & jnp.uint32(0xFFFF0000), F32)
    return jnp.concatenate([re, im], axis=0).astype(BF16)


def _fourier_in_kernel(x_ref, g_ref, sh_ref, sc_ref, wc_ref, z_ref):
    h = _norm_mod(x_ref[...], g_ref[...], sh_ref[0], sc_ref[0]).astype(BF16)
    zr, zi = _chan_dft(h, wc_ref)
    z_ref[...] = _pack_complex(zr, zi)


FFT_GROUP = 8


def _fft_stage1_kernel(z_ref, tr_ref, ti_ref, w1_ref, a_ref):
    r, kg = FFT_RADIX, FFT_GROUP
    d = z_ref.shape[-1]
    reps = d // LANES
    z = _unpack_complex(z_ref[...].reshape(r * kg, d))
    a = jnp.dot(w1_ref[...], z, preferred_element_type=F32)
    for j in range(kg):
        ar = a[2 * r * j:2 * r * j + r]
        ai = a[2 * r * j + r:2 * r * (j + 1)]
        tr = jnp.tile(tr_ref[j], (1, reps))
        ti = jnp.tile(ti_ref[j], (1, reps))
        a_ref[0, j] = _pack_complex(ar * tr - ai * ti, ar * ti + ai * tr)


def _fft_stage2_kernel(a_ref, w2_ref, f_ref):
    r, kg = FFT_RADIX, FFT_GROUP
    d = a_ref.shape[-1]
    z = _unpack_complex(a_ref[...].reshape(r * kg, d))
    f = jnp.dot(w2_ref[...], z, preferred_element_type=F32)
    f_ref[...] = f.reshape(r, kg, d)


def _fourier_lat(x, g, sh, sc, mats, *, batch, tm):
    t, d = x.shape
    s = t // batch
    tpb = s // tm
    z = pl.pallas_call(
        _fourier_in_kernel,
        grid=(t // tm,),
        in_specs=[
            pl.BlockSpec((tm, d), lambda i: (i, 0)),
            _resident((1, d)), _mod_spec(tpb, d), _mod_spec(tpb, d),
            _resident(mats["wc"].shape),
        ],
        out_specs=pl.BlockSpec((tm, d), lambda i: (i, 0)),
        out_shape=jax.ShapeDtypeStruct((t, d), jnp.uint32),
        compiler_params=_cparams(1),
        name="fourier_in",
    )(x, g, sh, sc, mats["wc"])

    r, kg = FFT_RADIX, FFT_GROUP
    gather = pl.BlockSpec((None, r, kg, d), lambda b, j: (b, 0, j, 0))
    slab = pl.BlockSpec((1, kg, r, d), lambda b, j: (b, j, 0, 0))
    tw = pl.BlockSpec((kg, r, LANES), lambda b, j: (j, 0, 0))
    const = lambda a: pl.BlockSpec(a.shape, lambda b, j: (0, 0))
    a = pl.pallas_call(
        _fft_stage1_kernel,
        grid=(batch, r // kg),
        in_specs=[gather, tw, tw, const(mats["m1"])],
        out_specs=slab,
        out_shape=jax.ShapeDtypeStruct((batch, r, r, d), jnp.uint32),
        compiler_params=_cparams(2),
        name="fft_stage1",
    )(z.reshape(batch, r, r, d), mats["tr"], mats["ti"], mats["m1"])
    f = pl.pallas_call(
        _fft_stage2_kernel,
        grid=(batch, r // kg),
        in_specs=[gather, const(mats["m2"])],
        out_specs=gather,
        out_shape=jax.ShapeDtypeStruct((batch, r, r, d), F32),
        compiler_params=_cparams(2),
        name="fft_stage2",
    )(a, mats["m2"])
    return f.reshape(t, d)


def _fourier_ctx_kernel(x_ref, g_ref, sh_ref, sc_ref, wc_ref, seq_ref, f_ref):
    h = _norm_mod(x_ref[...], g_ref[...], sh_ref[0], sc_ref[0]).astype(BF16)
    zr, zi = _chan_dft(h, wc_ref)
    z = jnp.concatenate([zr, zi], axis=0).astype(BF16)
    f_ref[...] = jnp.dot(seq_ref[...], z, preferred_element_type=F32)


def _fourier_ctx(x, g, sh, sc, mats, *, ctx_len):
    t, d = x.shape
    n_tiles = t // ctx_len
    return pl.pallas_call(
        _fourier_ctx_kernel,
        grid=(n_tiles,),
        in_specs=[
            pl.BlockSpec((ctx_len, d), lambda i: (i, 0)),
            _resident((1, d)), _mod_spec(n_tiles, d), _mod_spec(n_tiles, d),
            _resident(mats["wc"].shape), _resident(mats["ctx_seq"].shape),
        ],
        out_specs=pl.BlockSpec((ctx_len, d), lambda i: (i, 0)),
        out_shape=jax.ShapeDtypeStruct((t, d), F32),
        compiler_params=_cparams(1),
        name="fourier_ctx",
    )(x, g, sh, sc, mats["wc"], mats["ctx_seq"])


def _rope_tables(rows, dim, lane_of_dim):
    n_freq = dim // 4
    inv_freq = ROPE_BASE ** (-jnp.arange(n_freq, dtype=F32) / n_freq)
    t = jnp.arange(rows * GRID_W)
    row = (t // GRID_W).astype(F32)
    col = (t % GRID_W).astype(F32)
    ang = jnp.stack([row[:, None] * inv_freq, col[:, None] * inv_freq], axis=1)
    cos, sin = jnp.cos(ang), jnp.sin(ang)
    dims = np.array([lane_of_dim(l) for l in range(LANES)])
    live = dims >= 0
    dd = np.where(live, dims, 0)
    axis, pair, freq = dd // (2 * n_freq), (dd % (2 * n_freq)) // n_freq, dd % n_freq
    sign = np.where(pair == 0, -1.0, 1.0).astype(np.float32)
    cos_t = jnp.where(live[None, :], cos[:, axis, freq], 1.0)
    sin_t = jnp.where(live[None, :], sin[:, axis, freq] * sign[None, :], 0.0)
    return cos_t.astype(F32), sin_t.astype(F32)


def _scores(q, k_refs):
    nt = (((1,), (1,)), ((), ()))
    return [lax.dot_general(q, k[0], nt, preferred_element_type=F32) for k in k_refs]


def _attend(q, k_refs, vaug_refs):
    parts = _scores(q, k_refs)
    m = functools.reduce(jnp.maximum, [jnp.max(p, axis=-1, keepdims=True) for p in parts])
    acc = None
    for p, v in zip(parts, vaug_refs):
        e = jnp.exp2(p - m).astype(BF16)
        part = jnp.dot(e, v[0], preferred_element_type=F32)
        acc = part if acc is None else acc + part
    return acc[:, :LANES] / acc[:, LANES:]


def _pipe_attn_kernel(*refs, mode, n_seg, lam_init):
    n_par = 5 if mode == "diff" else 0
    par = refs[:n_par]
    q_ref = refs[n_par]
    k_refs = refs[n_par + 1:n_par + 1 + n_seg]
    v_refs = refs[n_par + 1 + n_seg:n_par + 1 + 2 * n_seg]
    o_ref, s_a, s_b, m_a, m_b = refs[n_par + 1 + 2 * n_seg:]
    t = pl.program_id(0)
    nt = (((1,), (1,)), ((), ()))

    def scores(s_w, m_w):
        q = q_ref[0]
        if mode == "diff":
            lane = lax.broadcasted_iota(jnp.int32, q.shape, 1)
            zero = jnp.zeros_like(q)
            qs = [jnp.where(lane < DIFF_HEAD_DIM, q, zero), jnp.where(lane >= DIFF_HEAD_DIM, q, zero)]
        else:
            qs = [q[:, :LANES], q[:, LANES:]]
        for c in range(2):
            m, off = None, 0
            for k in k_refs:
                kk = k[0] if mode == "diff" else k[0, :, c * LANES:(c + 1) * LANES]
                p = lax.dot_general(qs[c], kk, nt, preferred_element_type=F32)
                s_w[c, :, off:off + p.shape[1]] = p
                off += p.shape[1]
                pm = jnp.max(p, axis=-1, keepdims=True)
                m = pm if m is None else jnp.maximum(m, pm)
            m_w[c] = jnp.broadcast_to(m, m_w.shape[1:])

    def values(s_r, m_r):
        outs = []
        for c in range(2):
            m = m_r[c][:, :1]
            acc, off = None, 0
            for v in v_refs:
                w = v.shape[1]
                e = jnp.exp2(s_r[c, :, off:off + w] - m).astype(BF16)
                part = jnp.dot(e, v[0], preferred_element_type=F32)
                acc = part if acc is None else acc + part
                off += w
            outs.append(acc[:, :LANES] / acc[:, LANES:])
        if mode == "diff":
            lq1, lk1, lq2, lk2, sg = par
            lam = (jnp.exp(jnp.sum(lq1[...] * lk1[...], axis=-1, keepdims=True))
                   - jnp.exp(jnp.sum(lq2[...] * lk2[...], axis=-1, keepdims=True)) + lam_init)
            o = outs[0] - lam * outs[1]
            o_ref[0] = (_rms(o, sg[...]) * (1.0 - lam_init)).astype(BF16)
        else:
            lane = lax.broadcasted_iota(jnp.int32, outs[0].shape, 1)
            o_ref[0] = jnp.where(lane < MLA_V_DIM, outs[0], outs[1]).astype(BF16)

    @pl.when(t == 0)
    def _():
        s_b[...] = jnp.zeros(s_b.shape, F32)
        m_b[...] = jnp.zeros(m_b.shape, F32)

    @pl.when(t % 2 == 0)
    def _():
        scores(s_a, m_a)
        values(s_b, m_b)

    @pl.when(t % 2 == 1)
    def _():
        scores(s_b, m_b)
        values(s_a, m_a)


def _pipe_attn(params, q_arr, k_arrs, v_arrs, *, mode, units, q_width, k_block, tq, lam_init=0.0):
    b, sq, _ = q_arr.shape
    nq = sq // tq
    n_items = b * units * nq
    n_seg = len(k_arrs)
    sk = sum(a.shape[1] for a in k_arrs)

    def item(t, lag):
        tt = jnp.clip(t - lag, 0, n_items - 1)
        return tt // (units * nq), (tt // nq) % units, tt % nq

    def q_map(t):
        bb, u, i = item(t, 0)
        return bb, i, u

    def k_map(t):
        bb, u, _ = item(t, 0)
        return bb, 0, k_block + u

    def v_map(t):
        bb, u, _ = item(t, 1)
        return bb, 0, k_block + u

    def o_map(t):
        bb, u, i = item(t, 1)
        return bb, i, u

    in_specs = [pl.BlockSpec(p.shape, lambda t: (0, 0)) for p in params]
    in_specs += [pl.BlockSpec((1, tq, q_width), q_map)]
    in_specs += [pl.BlockSpec((1, a.shape[1], q_width), k_map) for a in k_arrs]
    in_specs += [pl.BlockSpec((1, a.shape[1], 2 * LANES), v_map) for a in v_arrs]
    return pl.pallas_call(
        functools.partial(_pipe_attn_kernel, mode=mode, n_seg=n_seg, lam_init=lam_init),
        grid=(n_items + 1,),
        in_specs=in_specs,
        out_specs=pl.BlockSpec((1, tq, LANES), o_map),
        out_shape=jax.ShapeDtypeStruct((b, sq, units * LANES), BF16),
        scratch_shapes=[pltpu.VMEM((2, tq, sk), F32), pltpu.VMEM((2, tq, sk), F32),
                        pltpu.VMEM((2, tq, LANES), F32), pltpu.VMEM((2, tq, LANES), F32)],
        compiler_params=_cparams(1),
        name=mode + "_pipe_attn",
    )(*params, q_arr, *k_arrs, *v_arrs)


def _diff_qkv_kernel(x_ref, g_ref, sh_ref, sc_ref, w_ref, cos_ref, sin_ref, o_ref, *, rope, qscale):
    h = _norm_mod(x_ref[...], g_ref[...], sh_ref[0], sc_ref[0]).astype(BF16)
    d = x_ref.shape[1]
    acc = jnp.dot(h, w_ref[...], preferred_element_type=F32)
    heads = d // LANES
    ones = jnp.ones((x_ref.shape[0], LANES), BF16)
    for tile in range(2 * heads):
        a = acc[:, tile * LANES:(tile + 1) * LANES]
        if rope:
            a = _rope_tile(a, cos_ref[...], sin_ref[...], DIFF_HEAD_DIM // 4)
        if tile < heads:
            a = a * qscale
        o_ref[:, tile * LANES:(tile + 1) * LANES] = a.astype(BF16)
    for hd in range(heads):
        src = (2 * heads + hd) * LANES
        dst = (2 * heads + 2 * hd) * LANES
        o_ref[:, dst:dst + LANES] = acc[:, src:src + LANES].astype(BF16)
        o_ref[:, dst + LANES:dst + 2 * LANES] = ones


def _diff_qkv(x, g, sh, sc, w, cos, sin, *, rope, tm):
    t, d = x.shape
    nb = sh.shape[0]
    tpb = t // nb // tm
    spb = t // nb // tm
    n = w.shape[1] + d
    qscale = DIFF_HEAD_DIM ** -0.5 * LOG2E
    tab = pl.BlockSpec((tm, LANES), (lambda i: (i % spb, 0)) if rope else (lambda i: (0, 0)))
    return pl.pallas_call(
        functools.partial(_diff_qkv_kernel, rope=rope, qscale=qscale),
        grid=(t // tm,),
        in_specs=[
            pl.BlockSpec((tm, d), lambda i: (i, 0)),
            _resident((1, d)), _mod_spec(tpb, d), _mod_spec(tpb, d),
            _resident(w.shape), tab, tab,
        ],
        out_specs=pl.BlockSpec((tm, n), lambda i: (i, 0)),
        out_shape=jax.ShapeDtypeStruct((t, n), BF16),
        compiler_params=_cparams(1),
        name="diff_qkv",
    )(x, g, sh, sc, w, cos, sin)


def _diff_attn_kernel(lq1_ref, lk1_ref, lq2_ref, lk2_ref, sg_ref, q_ref, *refs, n_seg, lam_init):
    k_refs, v_refs, o_ref = refs[:n_seg], refs[n_seg:2 * n_seg], refs[2 * n_seg]
    lam = (jnp.exp(jnp.sum(lq1_ref[...] * lk1_ref[...], axis=-1, keepdims=True))
           - jnp.exp(jnp.sum(lq2_ref[...] * lk2_ref[...], axis=-1, keepdims=True)) + lam_init)
    q = q_ref[0]
    lane = lax.broadcasted_iota(jnp.int32, q.shape, 1)
    zero = jnp.zeros_like(q)
    o1 = _attend(jnp.where(lane < DIFF_HEAD_DIM, q, zero), k_refs, v_refs)
    o2 = _attend(jnp.where(lane >= DIFF_HEAD_DIM, q, zero), k_refs, v_refs)
    o = o1 - lam * o2
    o_ref[0] = (_rms(o, sg_ref[...]) * (1.0 - lam_init)).astype(BF16)


def _diff_attn(lams, subln_g, q_arr, kv_arrs, *, heads, tq, lam_init):
    b, sq, _ = q_arr.shape
    n_seg = len(kv_arrs)
    lam_specs = [pl.BlockSpec(l.shape, lambda bb, h, i: (0, 0)) for l in lams]
    in_specs = lam_specs + [
        pl.BlockSpec((1, DIFF_V_DIM), lambda bb, h, i: (0, 0)),
        pl.BlockSpec((1, tq, LANES), lambda bb, h, i: (bb, i, h)),
    ]
    in_specs += [pl.BlockSpec((1, a.shape[1], LANES), lambda bb, h, i: (bb, 0, heads + h))
                 for a in kv_arrs]
    in_specs += [pl.BlockSpec((1, a.shape[1], 2 * LANES), lambda bb, h, i: (bb, 0, heads + h))
                 for a in kv_arrs]
    return pl.pallas_call(
        functools.partial(_diff_attn_kernel, n_seg=n_seg, lam_init=lam_init),
        grid=(b, heads, sq // tq),
        in_specs=in_specs,
        out_specs=pl.BlockSpec((1, tq, LANES), lambda bb, h, i: (bb, i, h)),
        out_shape=jax.ShapeDtypeStruct((b, sq, heads * DIFF_V_DIM), BF16),
        compiler_params=_cparams(3),
        name="diff_attn",
    )(*lams, subln_g, q_arr, *kv_arrs, *kv_arrs)


def _mla_down_kernel(x_ref, g_ref, sh_ref, sc_ref, w_ref, qg_ref, kvg_ref, cos_ref, sin_ref,
                     cq_ref, ckv_ref, kr_ref, *, rope):
    h = _norm_mod(x_ref[...], g_ref[...], sh_ref[0], sc_ref[0]).astype(BF16)
    acc = jnp.dot(h, w_ref[...], preferred_element_type=F32)
    cq_ref[...] = _rms(acc[:, :MLA_Q_RANK], qg_ref[...]).astype(BF16)
    ckv_ref[...] = _rms(acc[:, MLA_Q_RANK:MLA_Q_RANK + MLA_KV_RANK], kvg_ref[...]).astype(BF16)
    kr = acc[:, MLA_Q_RANK + MLA_KV_RANK:]
    if rope:
        kr = _rope_tile(kr, cos_ref[...], sin_ref[...], MLA_ROPE_DIM // 4)
    kr_ref[...] = kr.astype(BF16)


def _mla_up_kernel(cq_ref, ckv_ref, kr_ref, wq_ref, wk_ref, wv_ref, cos_ref, sin_ref,
                   q_ref, k_ref, v_ref, *, rope, qscale):
    q = jnp.dot(cq_ref[...], wq_ref[...], preferred_element_type=F32)
    for hd in range(MLA_HEADS):
        a = q[:, hd * LANES:(hd + 1) * LANES]
        if rope:
            a = _rope_tile(a, cos_ref[...], sin_ref[...], MLA_ROPE_DIM // 4)
        q_ref[:, hd * LANES:(hd + 1) * LANES] = (a * qscale).astype(BF16)
    k = jnp.dot(ckv_ref[...], wk_ref[...], preferred_element_type=F32)
    kr = kr_ref[...].astype(F32)
    for hd in range(MLA_HEADS):
        k_ref[:, hd * LANES:(hd + 1) * LANES] = (k[:, hd * LANES:(hd + 1) * LANES] + kr).astype(BF16)
    v = jnp.dot(ckv_ref[...], wv_ref[...], preferred_element_type=F32)
    ones = jnp.ones((v.shape[0], LANES), BF16)
    for pair in range(MLA_HEADS // 2):
        v_ref[:, 2 * pair * LANES:(2 * pair + 1) * LANES] = v[:, pair * LANES:(pair + 1) * LANES].astype(BF16)
        v_ref[:, (2 * pair + 1) * LANES:(2 * pair + 2) * LANES] = ones


def _mla_project(x, g, sh, sc, w_down, qg, kvg, wq, wk, wv, cos, sin, *, rope, tm):
    t, d = x.shape
    nb = sh.shape[0]
    tpb = t // nb // tm
    tab = pl.BlockSpec((tm, LANES), (lambda i: (i % tpb, 0)) if rope else (lambda i: (0, 0)))
    row = lambda n: pl.BlockSpec((tm, n), lambda i: (i, 0))
    cq, ckv, kr = pl.pallas_call(
        functools.partial(_mla_down_kernel, rope=rope),
        grid=(t // tm,),
        in_specs=[row(d), _resident((1, d)), _mod_spec(tpb, d), _mod_spec(tpb, d),
                  _resident(w_down.shape), _resident(qg.shape), _resident(kvg.shape), tab, tab],
        out_specs=[row(MLA_Q_RANK), row(MLA_KV_RANK), row(LANES)],
        out_shape=[jax.ShapeDtypeStruct((t, MLA_Q_RANK), BF16),
                   jax.ShapeDtypeStruct((t, MLA_KV_RANK), BF16),
                   jax.ShapeDtypeStruct((t, LANES), BF16)],
        compiler_params=_cparams(1),
        name="mla_down",
    )(x, g, sh, sc, w_down, qg, kvg, cos, sin)
    qscale = (MLA_NOPE_DIM + MLA_ROPE_DIM) ** -0.5 * LOG2E
    nq, nv = wq.shape[1], 2 * wv.shape[1]
    return pl.pallas_call(
        functools.partial(_mla_up_kernel, rope=rope, qscale=qscale),
        grid=(t // tm,),
        in_specs=[row(MLA_Q_RANK), row(MLA_KV_RANK), row(LANES),
                  _resident(wq.shape), _resident(wk.shape), _resident(wv.shape), tab, tab],
        out_specs=[row(nq), row(nq), row(nv)],
        out_shape=[jax.ShapeDtypeStruct((t, nq), BF16), jax.ShapeDtypeStruct((t, nq), BF16),
                   jax.ShapeDtypeStruct((t, nv), BF16)],
        compiler_params=_cparams(1),
        name="mla_up",
    )(cq, ckv, kr, wq, wk, wv, cos, sin)


def _mla_attn_kernel(q_ref, *refs, n_seg):
    k_refs, v_refs, o_ref = refs[:n_seg], refs[n_seg:2 * n_seg], refs[2 * n_seg]
    outs = []
    for hd in range(2):
        q = q_ref[0, :, hd * LANES:(hd + 1) * LANES]
        ks = [k.at[:, :, hd * LANES:(hd + 1) * LANES] for k in k_refs]
        outs.append(_attend(q, ks, v_refs))
    lane = lax.broadcasted_iota(jnp.int32, outs[0].shape, 1)
    o_ref[0] = jnp.where(lane < MLA_V_DIM, outs[0], outs[1]).astype(BF16)


def _mla_attn(q, k_arrs, v_arrs, *, tq):
    b, sq, nq = q.shape
    pairs = nq // (2 * LANES)
    n_seg = len(k_arrs)
    in_specs = [pl.BlockSpec((1, tq, 2 * LANES), lambda bb, h, i: (bb, i, h))]
    in_specs += [pl.BlockSpec((1, a.shape[1], 2 * LANES), lambda bb, h, i: (bb, 0, h)) for a in k_arrs]
    in_specs += [pl.BlockSpec((1, a.shape[1], 2 * LANES), lambda bb, h, i: (bb, 0, h)) for a in v_arrs]
    return pl.pallas_call(
        functools.partial(_mla_attn_kernel, n_seg=n_seg),
        grid=(b, pairs, sq // tq),
        in_specs=in_specs,
        out_specs=pl.BlockSpec((1, tq, LANES), lambda bb, h, i: (bb, i, h)),
        out_shape=jax.ShapeDtypeStruct((b, sq, pairs * LANES), BF16),
        compiler_params=_cparams(3),
        name="mla_attn",
    )(q, *k_arrs, *v_arrs)


def _mla_weights(w_down, w_uq, w_ukv):
    d = w_down.shape[0]
    hq = MLA_NOPE_DIM + MLA_ROPE_DIM
    pad_r = LANES - hq
    lo = MLA_Q_RANK + MLA_KV_RANK
    zeros = lambda n: jnp.zeros((d, n), w_down.dtype)
    wd = jnp.concatenate([w_down[:, :lo], zeros(MLA_NOPE_DIM), w_down[:, lo:], zeros(pad_r)], axis=1)
    wq = w_uq.reshape(MLA_Q_RANK, MLA_HEADS, hq)
    wq = jnp.pad(wq, ((0, 0), (0, 0), (0, pad_r))).reshape(MLA_Q_RANK, MLA_HEADS * LANES)
    wkv = w_ukv.reshape(MLA_KV_RANK, MLA_HEADS, MLA_NOPE_DIM + MLA_V_DIM)
    wk = jnp.pad(wkv[:, :, :MLA_NOPE_DIM], ((0, 0), (0, 0), (0, LANES - MLA_NOPE_DIM)))
    wk = wk.reshape(MLA_KV_RANK, MLA_HEADS * LANES)
    wv = wkv[:, :, MLA_NOPE_DIM:].reshape(MLA_KV_RANK, MLA_HEADS * MLA_V_DIM)
    return wd.astype(BF16), wq.astype(BF16), wk.astype(BF16), wv.astype(BF16)


def kernel(x, c, ctx, c_ctx, mod_w, mod_b, norm_mix_g, norm_mlp_g, final_g, mlp_w1, mlp_w2, fourier_w, fourier_b, diff_w_qkv, diff_lambda_q1, diff_lambda_k1, diff_lambda_q2, diff_lambda_k2, diff_subln_g, diff_w_o, mla_w_down, mla_q_norm_g, mla_kv_norm_g, mla_w_uq, mla_w_ukv, mla_w_o):
    batch, seq, d = x.shape
    ctx_len = ctx.shape[1]
    depth = mod_w.shape[0]
    rows = seq // GRID_W
    assert seq == FFT_RADIX * FFT_RADIX and d == FOURIER_GROUPS * 256 and ctx_len == 256
    tm = 512

    cc_rows = -(-(batch + 1) // 8) * 8
    cc = jnp.zeros((cc_rows, d), F32).at[:batch].set(c).at[batch].set(c_ctx)
    mods = _modulation(cc, mod_w, mod_b).reshape(depth, cc_rows, 6, d)
    mats = _dft_mats()
    zero_bias = jnp.zeros((1, d), F32)
    final_g2 = final_g.reshape(1, d)

    lane_diff = lambda l: l % DIFF_HEAD_DIM
    cos_d, sin_d = _rope_tables(rows, DIFF_HEAD_DIM, lane_diff)
    lane_mla = lambda l: l - MLA_NOPE_DIM if MLA_NOPE_DIM <= l < MLA_NOPE_DIM + MLA_ROPE_DIM else -1
    cos_m, sin_m = _rope_tables(rows, MLA_ROPE_DIM, lane_mla)

    x_lat = x.reshape(batch * seq, d)
    x_ctx = ctx.reshape(batch * ctx_len, d)
    for i in range(depth):
        kind, j = i % N_MIXERS, i // N_MIXERS
        update_ctx = i < depth - 1
        ctx_feeds_mixer = update_ctx or kind != 0
        lat = [mods[i, :batch, k].reshape(batch, 1, d) for k in range(6)]
        cm = [mods[i, batch:batch + 1, k].reshape(1, 1, d) for k in range(6)]
        g_mix = norm_mix_g[i].reshape(1, d)
        g_mlp = norm_mlp_g[i].reshape(1, d)
        a_ctx = None
        if kind == 0:
            w_o = fourier_w[j].astype(BF16)
            b_o = fourier_b[j].reshape(1, d)
            a_lat = _fourier_lat(x_lat, g_mix, lat[0], lat[1], mats, batch=batch, tm=tm)
            if update_ctx:
                a_ctx = _fourier_ctx(x_ctx, g_mix, cm[0], cm[1], mats, ctx_len=ctx_len)
        elif kind == 1:
            heads = d // DIFF_V_DIM
            w_qkv = diff_w_qkv[j].astype(BF16)
            w_o = diff_w_o[j].astype(BF16)
            b_o = zero_bias
            lam_init = 0.8 - 0.6 * math.exp(-0.3 * i)
            lams = [v[j].reshape(1, DIFF_HEAD_DIM) for v in
                    (diff_lambda_q1, diff_lambda_k1, diff_lambda_q2, diff_lambda_k2)]
            sg = diff_subln_g[j].reshape(1, DIFF_V_DIM)
            qkv_l = _diff_qkv(x_lat, g_mix, lat[0], lat[1], w_qkv, cos_d, sin_d, rope=True, tm=tm)
            qkv_l = qkv_l.reshape(batch, seq, 4 * d)
            qkv_c = _diff_qkv(x_ctx, g_mix, cm[0], cm[1], w_qkv, cos_d, sin_d, rope=False, tm=ctx_len)
            qkv_c = qkv_c.reshape(batch, ctx_len, 4 * d)
            o_l = _pipe_attn(lams + [sg], qkv_l, [qkv_c, qkv_l], [qkv_c, qkv_l], mode="diff",
                             units=heads, q_width=LANES, k_block=heads, tq=256, lam_init=lam_init)
            a_lat = o_l.reshape(batch * seq, d)
            if update_ctx:
                o_c = _diff_attn(lams, sg, qkv_c, [qkv_c], heads=heads, tq=ctx_len, lam_init=lam_init)
                a_ctx = o_c.reshape(batch * ctx_len, d)
        else:
            wd, wq, wk, wv = _mla_weights(mla_w_down[j], mla_w_uq[j], mla_w_ukv[j])
            w_o = mla_w_o[j].astype(BF16)
            b_o = zero_bias
            qg = mla_q_norm_g[j].reshape(1, MLA_Q_RANK)
            kvg = mla_kv_norm_g[j].reshape(1, MLA_KV_RANK)
            q_l, k_l, v_l = _mla_project(x_lat, g_mix, lat[0], lat[1], wd, qg, kvg, wq, wk, wv,
                                         cos_m, sin_m, rope=True, tm=tm)
            q_c, k_c, v_c = _mla_project(x_ctx, g_mix, cm[0], cm[1], wd, qg, kvg, wq, wk, wv,
                                         cos_m, sin_m, rope=False, tm=ctx_len)
            r3 = lambda a, s: a.reshape(batch, s, a.shape[1])
            o_l = _pipe_attn([], r3(q_l, seq), [r3(k_c, ctx_len), r3(k_l, seq)],
                             [r3(v_c, ctx_len), r3(v_l, seq)], mode="mla", units=MLA_HEADS // 2,
                             q_width=2 * LANES, k_block=0, tq=256)
            a_lat = o_l.reshape(batch * seq, d)
            if update_ctx:
                o_c = _mla_attn(r3(q_c, ctx_len), [r3(k_c, ctx_len)], [r3(v_c, ctx_len)], tq=ctx_len)
                a_ctx = o_c.reshape(batch * ctx_len, d)
        w1 = mlp_w1[i].astype(BF16)
        w2 = mlp_w2[i].astype(BF16)
        x_lat = _block_tail(a_lat, w_o, b_o, lat[2], x_lat, g_mlp, lat[3], lat[4], lat[5], w1, w2,
                            final_g2, final_norm=(i == depth - 1), tm=tm)
        if update_ctx:
            x_ctx = _block_tail(a_ctx, w_o, b_o, cm[2], x_ctx, g_mlp, cm[3], cm[4], cm[5], w1, w2,
                                final_g2, final_norm=False, tm=ctx_len)
    return x_lat.reshape(batch, seq, d)
```

```python
import functools
import math

import numpy as np
import jax
import jax.numpy as jnp
from jax import lax
from jax.experimental import pallas as pl
from jax.experimental.pallas import tpu as pltpu

F32 = jnp.float32
BF16 = jnp.bfloat16

GRID_W = 64
RMS_EPS = 1e-6
ROPE_BASE = 10000.0
FOURIER_GROUPS = 4
DIFF_HEAD_DIM = 64
DIFF_V_DIM = 128
MLA_HEADS = 16
MLA_NOPE_DIM = 64
MLA_ROPE_DIM = 32
MLA_V_DIM = 64
MLA_Q_RANK = 256
MLA_KV_RANK = 128
N_MIXERS = 3

LANES = 128
FFT_RADIX = 64
VMEM_LIMIT = 56 * 1024 * 1024
LOG2E = 1.4426950408889634


def _cparams(n_axes):
    return pltpu.CompilerParams(
        dimension_semantics=("arbitrary",) * n_axes, vmem_limit_bytes=VMEM_LIMIT)


def _resident(shape):
    nd = len(shape)
    return pl.BlockSpec(shape, lambda *_: (0,) * nd, pipeline_mode=pl.Buffered(1))


def _norm_mod(x, g, sh, sc):
    r = lax.rsqrt(jnp.mean(x * x, axis=-1, keepdims=True) + RMS_EPS)
    return ((x * r) * g) * (1.0 + sc) + sh


def _rms(x, g):
    r = lax.rsqrt(jnp.mean(x * x, axis=-1, keepdims=True) + RMS_EPS)
    return (x * r) * g


def _rope_tile(x, cos, sin, half):
    lane = lax.broadcasted_iota(jnp.int32, x.shape, 1)
    fwd = pltpu.roll(x, LANES - half, 1)
    bwd = pltpu.roll(x, half, 1)
    partner = jnp.where((lane & half) == 0, fwd, bwd)
    return x * cos + partner * sin


def _mod_kernel(c_ref, w_ref, b_ref, o_ref):
    c = c_ref[...]
    s = (c / (1.0 + jnp.exp(-c))).astype(BF16)
    o_ref[0] = jnp.dot(s, w_ref[0].astype(BF16), preferred_element_type=F32) + b_ref[0]


def _modulation(cc, mod_w, mod_b):
    depth, d, n = mod_w.shape
    rows = cc.shape[0]
    tn = n // 4
    return pl.pallas_call(
        _mod_kernel,
        grid=(depth, n // tn),
        in_specs=[
            pl.BlockSpec((rows, d), lambda i, j: (0, 0)),
            pl.BlockSpec((1, d, tn), lambda i, j: (i, 0, j)),
            pl.BlockSpec((1, 1, tn), lambda i, j: (i, 0, j)),
        ],
        out_specs=pl.BlockSpec((1, rows, tn), lambda i, j: (i, 0, j)),
        out_shape=jax.ShapeDtypeStruct((depth, rows, n), F32),
        compiler_params=_cparams(2),
        name="modulation",
    )(cc, mod_w, mod_b.reshape(depth, 1, n))


def _mod_spec(tiles_per_batch, d):
    return pl.BlockSpec((1, 1, d), lambda i: (i // tiles_per_batch, 0, 0))


FF_CHUNK = 1024


def _tail_math(a, x, wo_ref, bo_ref, ga_ref, g_ref, sh_ref, sc_ref, gm_ref, w1_ref, w2_ref, fg_ref,
               final_norm):
    y = jnp.dot(a.astype(BF16), wo_ref[...], preferred_element_type=F32) + bo_ref[...]
    x = x + ga_ref[0] * y
    h = _norm_mod(x, g_ref[...], sh_ref[0], sc_ref[0]).astype(BF16)
    acc = jnp.zeros(x.shape, F32)
    for c in range(w1_ref.shape[1] // FF_CHUNK):
        u = jnp.dot(h, w1_ref[:, c * FF_CHUNK:(c + 1) * FF_CHUNK], preferred_element_type=F32)
        u = jnp.square(jnp.maximum(u, 0.0)).astype(BF16)
        acc = acc + jnp.dot(u, w2_ref[c * FF_CHUNK:(c + 1) * FF_CHUNK, :],
                            preferred_element_type=F32)
    out = x + gm_ref[0] * acc
    if final_norm:
        out = _rms(out, fg_ref[...])
    return out


def _block_tail_kernel(a_ref, x_ref, *refs, final_norm):
    o_ref = refs[-1]
    o_ref[...] = _tail_math(a_ref[...], x_ref[...], *refs[:-1], final_norm=final_norm)


def _block_tail(a, w_o, b_o, gate_a, x, g, sh, sc, gate_m, w1, w2, final_g, *, final_norm, tm):
    t, d = x.shape
    nb = sh.shape[0]
    tpb = t // nb // tm
    row = lambda n: pl.BlockSpec((tm, n), lambda i: (i, 0))
    return pl.pallas_call(
        functools.partial(_block_tail_kernel, final_norm=final_norm),
        grid=(t // tm,),
        in_specs=[
            row(a.shape[1]), row(d),
            _resident(w_o.shape), _resident((1, d)), _mod_spec(tpb, d),
            _resident((1, d)), _mod_spec(tpb, d), _mod_spec(tpb, d), _mod_spec(tpb, d),
            _resident(w1.shape), _resident(w2.shape), _resident((1, d)),
        ],
        out_specs=row(d),
        out_shape=jax.ShapeDtypeStruct((t, d), F32),
        compiler_params=_cparams(1),
        name="block_tail",
    )(a, x, w_o, b_o, gate_a, g, sh, sc, gate_m, w1, w2, final_g)


def _dft_mats():
    cg = 256
    j = np.arange(cg)
    th = 2.0 * np.pi * np.outer(j, j) / cg
    wc = np.concatenate([np.cos(th), -np.sin(th)], axis=1) / 16.0
    ctx_seq = np.concatenate([np.cos(th), np.sin(th)], axis=1) / 16.0
    r = np.arange(FFT_RADIX)
    ph = 2.0 * np.pi * np.outer(r, r) / FFT_RADIX
    dr, di = np.cos(ph) / 8.0, -np.sin(ph) / 8.0
    m1 = np.block([[dr, -di], [di, dr]])
    m2 = np.concatenate([dr, -di], axis=1)
    eye = np.eye(FFT_GROUP)
    m1 = np.einsum("pq,jk->jpqk", m1, eye).reshape(m1.shape[0] * FFT_GROUP, m1.shape[1] * FFT_GROUP)
    m2 = np.einsum("pq,jk->pjqk", m2, eye).reshape(m2.shape[0] * FFT_GROUP, m2.shape[1] * FFT_GROUP)
    tw = 2.0 * np.pi * np.outer(r, r) / (FFT_RADIX * FFT_RADIX)
    tr = np.repeat(np.cos(tw)[:, :, None], LANES, axis=2)
    ti = np.repeat(-np.sin(tw)[:, :, None], LANES, axis=2)
    as_bf = lambda a: jnp.asarray(a, F32).astype(BF16)
    return dict(wc=as_bf(wc), ctx_seq=as_bf(ctx_seq), m1=as_bf(m1), m2=as_bf(m2),
                tr=jnp.asarray(tr, F32), ti=jnp.asarray(ti, F32))


def _chan_dft(h, wc_ref):
    cg = wc_ref.shape[0]
    res, ims = [], []
    for grp in range(h.shape[1] // cg):
        z = jnp.dot(h[:, grp * cg:(grp + 1) * cg], wc_ref[...], preferred_element_type=F32)
        res.append(z[:, :cg])
        ims.append(z[:, cg:])
    return jnp.concatenate(res, axis=1), jnp.concatenate(ims, axis=1)


def _pack_complex(re, im):
    bits = lambda v: lax.bitcast_convert_type(v.astype(BF16).astype(F32), jnp.uint32)
    return (bits(re) >> 16) | bits(im)


def _unpack_complex(w):
    re = lax.bitcast_convert_type(w << 16, F32)
    im = lax.bitcast_convert_type(w & jnp.uint32(0xFFFF0000), F32)
    return jnp.concatenate([re, im], axis=0).astype(BF16)


FFT_GROUP = 8


def _digit_spec(d):
    return pl.BlockSpec((None, FFT_RADIX, FFT_GROUP, d), lambda b, j: (b, 0, j, 0))


def _fourier_front_kernel(x_ref, g_ref, sh_ref, sc_ref, wc_ref, tr_ref, ti_ref, w1_ref, a_ref):
    r, kg = FFT_RADIX, FFT_GROUP
    d = x_ref.shape[-1]
    reps = d // LANES
    x = x_ref[...].reshape(r * kg, d)
    h = _norm_mod(x, g_ref[...], sh_ref[0], sc_ref[0]).astype(BF16)
    zr, zi = _chan_dft(h, wc_ref)
    z = jnp.concatenate([zr, zi], axis=0).astype(BF16)
    a = jnp.dot(w1_ref[...], z, preferred_element_type=F32)
    for j in range(kg):
        ar = a[2 * r * j:2 * r * j + r]
        ai = a[2 * r * j + r:2 * r * (j + 1)]
        tr = jnp.tile(tr_ref[j], (1, reps))
        ti = jnp.tile(ti_ref[j], (1, reps))
        a_ref[0, j] = _pack_complex(ar * tr - ai * ti, ar * ti + ai * tr)


def _fourier_front(x, g, sh, sc, mats, *, batch):
    t, d = x.shape
    r, kg = FFT_RADIX, FFT_GROUP
    const = lambda a: pl.BlockSpec(a.shape, lambda b, j: (0,) * a.ndim)
    mod = pl.BlockSpec((1, 1, d), lambda b, j: (b, 0, 0))
    tw = pl.BlockSpec((kg, r, LANES), lambda b, j: (j, 0, 0))
    return pl.pallas_call(
        _fourier_front_kernel,
        grid=(batch, r // kg),
        in_specs=[_digit_spec(d), const(g), mod, mod, const(mats["wc"]), tw, tw, const(mats["m1"])],
        out_specs=pl.BlockSpec((1, kg, r, d), lambda b, j: (b, j, 0, 0)),
        out_shape=jax.ShapeDtypeStruct((batch, r, r, d), jnp.uint32),
        compiler_params=_cparams(2),
        name="fourier_front",
    )(x.reshape(batch, r, r, d), g, sh, sc, mats["wc"], mats["tr"], mats["ti"], mats["m1"])


def _fourier_tail_kernel(a_ref, m2_ref, x_ref, *refs, final_norm):
    r, kg = FFT_RADIX, FFT_GROUP
    d = a_ref.shape[-1]
    o_ref = refs[-1]
    z = _unpack_complex(a_ref[...].reshape(r * kg, d))
    f = jnp.dot(m2_ref[...], z, preferred_element_type=F32)
    out = _tail_math(f, x_ref[...].reshape(r * kg, d), *refs[:-1], final_norm=final_norm)
    o_ref[...] = out.reshape(r, kg, d)


def _fourier_tail(a, mats, w_o, b_o, gate_a, x, g, sh, sc, gate_m, w1, w2, final_g, *, batch,
                  final_norm):
    t, d = x.shape
    r, kg = FFT_RADIX, FFT_GROUP
    const = lambda v: pl.BlockSpec(v.shape, lambda b, j: (0,) * v.ndim, pipeline_mode=pl.Buffered(1))
    mod = pl.BlockSpec((1, 1, d), lambda b, j: (b, 0, 0))
    out = pl.pallas_call(
        functools.partial(_fourier_tail_kernel, final_norm=final_norm),
        grid=(batch, r // kg),
        in_specs=[_digit_spec(d), const(mats["m2"]), _digit_spec(d),
                  const(w_o), const(b_o), mod, const(g), mod, mod, mod,
                  const(w1), const(w2), const(final_g)],
        out_specs=_digit_spec(d),
        out_shape=jax.ShapeDtypeStruct((batch, r, r, d), F32),
        compiler_params=_cparams(2),
        name="fourier_tail",
    )(a, mats["m2"], x.reshape(batch, r, r, d), w_o, b_o, gate_a, g, sh, sc, gate_m, w1, w2, final_g)
    return out.reshape(t, d)


def _fourier_ctx_kernel(x_ref, g_ref, sh_ref, sc_ref, wc_ref, seq_ref, f_ref):
    h = _norm_mod(x_ref[...], g_ref[...], sh_ref[0], sc_ref[0]).astype(BF16)
    zr, zi = _chan_dft(h, wc_ref)
    z = jnp.concatenate([zr, zi], axis=0).astype(BF16)
    f_ref[...] = jnp.dot(seq_ref[...], z, preferred_element_type=F32)


def _fourier_ctx(x, g, sh, sc, mats, *, ctx_len):
    t, d = x.shape
    n_tiles = t // ctx_len
    return pl.pallas_call(
        _fourier_ctx_kernel,
        grid=(n_tiles,),
        in_specs=[
            pl.BlockSpec((ctx_len, d), lambda i: (i, 0)),
            _resident((1, d)), _mod_spec(n_tiles, d), _mod_spec(n_tiles, d),
            _resident(mats["wc"].shape), _resident(mats["ctx_seq"].shape),
        ],
        out_specs=pl.BlockSpec((ctx_len, d), lambda i: (i, 0)),
        out_shape=jax.ShapeDtypeStruct((t, d), F32),
        compiler_params=_cparams(1),
        name="fourier_ctx",
    )(x, g, sh, sc, mats["wc"], mats["ctx_seq"])


def _rope_tables(rows, dim, lane_of_dim):
    n_freq = dim // 4
    inv_freq = ROPE_BASE ** (-jnp.arange(n_freq, dtype=F32) / n_freq)
    t = jnp.arange(rows * GRID_W)
    row = (t // GRID_W).astype(F32)
    col = (t % GRID_W).astype(F32)
    ang = jnp.stack([row[:, None] * inv_freq, col[:, None] * inv_freq], axis=1)
    cos, sin = jnp.cos(ang), jnp.sin(ang)
    dims = np.array([lane_of_dim(l) for l in range(LANES)])
    live = dims >= 0
    dd = np.where(live, dims, 0)
    axis, pair, freq = dd // (2 * n_freq), (dd % (2 * n_freq)) // n_freq, dd % n_freq
    sign = np.where(pair == 0, -1.0, 1.0).astype(np.float32)
    cos_t = jnp.where(live[None, :], cos[:, axis, freq], 1.0)
    sin_t = jnp.where(live[None, :], sin[:, axis, freq] * sign[None, :], 0.0)
    return cos_t.astype(F32), sin_t.astype(F32)


def _scores(q, k_refs):
    nt = (((1,), (1,)), ((), ()))
    return [lax.dot_general(q, k[0], nt, preferred_element_type=F32) for k in k_refs]


def _attend(q, k_refs, vaug_refs):
    parts = _scores(q, k_refs)
    m = functools.reduce(jnp.maximum, [jnp.max(p, axis=-1, keepdims=True) for p in parts])
    acc = None
    for p, v in zip(parts, vaug_refs):
        e = jnp.exp2(p - m).astype(BF16)
        part = jnp.dot(e, v[0], preferred_element_type=F32)
        acc = part if acc is None else acc + part
    return acc[:, :LANES] / acc[:, LANES:]


def _pipe_attn_kernel(*refs, mode, n_seg, lam_init):
    n_par = 5 if mode == "diff" else 0
    par = refs[:n_par]
    q_ref = refs[n_par]
    k_refs = refs[n_par + 1:n_par + 1 + n_seg]
    v_refs = refs[n_par + 1 + n_seg:n_par + 1 + 2 * n_seg]
    o_ref, s_a, s_b, m_a, m_b = refs[n_par + 1 + 2 * n_seg:]
    t = pl.program_id(0)
    nt = (((1,), (1,)), ((), ()))

    def scores(s_w, m_w):
        q = q_ref[0]
        if mode == "diff":
            lane = lax.broadcasted_iota(jnp.int32, q.shape, 1)
            zero = jnp.zeros_like(q)
            qs = [jnp.where(lane < DIFF_HEAD_DIM, q, zero), jnp.where(lane >= DIFF_HEAD_DIM, q, zero)]
        else:
            qs = [q[:, :LANES], q[:, LANES:]]
        for c in range(2):
            m, off = None, 0
            for k in k_refs:
                kk = k[0] if mode == "diff" else k[0, :, c * LANES:(c + 1) * LANES]
                p = lax.dot_general(qs[c], kk, nt, preferred_element_type=F32)
                s_w[c, :, off:off + p.shape[1]] = p
                off += p.shape[1]
                pm = jnp.max(p, axis=-1, keepdims=True)
                m = pm if m is None else jnp.maximum(m, pm)
            m_w[c] = jnp.broadcast_to(m, m_w.shape[1:])

    def values(s_r, m_r):
        outs = []
        for c in range(2):
            m = m_r[c][:, :1]
            acc, off = None, 0
            for v in v_refs:
                w = v.shape[1]
                e = jnp.exp2(s_r[c, :, off:off + w] - m).astype(BF16)
                part = jnp.dot(e, v[0], preferred_element_type=F32)
                acc = part if acc is None else acc + part
                off += w
            outs.append(acc[:, :LANES] / acc[:, LANES:])
        if mode == "diff":
            lq1, lk1, lq2, lk2, sg = par
            lam = (jnp.exp(jnp.sum(lq1[...] * lk1[...], axis=-1, keepdims=True))
                   - jnp.exp(jnp.sum(lq2[...] * lk2[...], axis=-1, keepdims=True)) + lam_init)
            o = outs[0] - lam * outs[1]
            o_ref[0] = (_rms(o, sg[...]) * (1.0 - lam_init)).astype(BF16)
        else:
            lane = lax.broadcasted_iota(jnp.int32, outs[0].shape, 1)
            o_ref[0] = jnp.where(lane < MLA_V_DIM, outs[0], outs[1]).astype(BF16)

    @pl.when(t == 0)
    def _():
        s_b[...] = jnp.zeros(s_b.shape, F32)
        m_b[...] = jnp.zeros(m_b.shape, F32)

    @pl.when(t % 2 == 0)
    def _():
        scores(s_a, m_a)
        values(s_b, m_b)

    @pl.when(t % 2 == 1)
    def _():
        scores(s_b, m_b)
        values(s_a, m_a)


def _pipe_attn(params, q_arr, k_arrs, v_arrs, *, mode, units, q_width, k_block, tq, lam_init=0.0):
    b, sq, _ = q_arr.shape
    nq = sq // tq
    n_items = b * units * nq
    n_seg = len(k_arrs)
    sk = sum(a.shape[1] for a in k_arrs)

    def item(t, lag):
        tt = jnp.clip(t - lag, 0, n_items - 1)
        return tt // (units * nq), (tt // nq) % units, tt % nq

    def q_map(t):
        bb, u, i = item(t, 0)
        return bb, i, u

    def k_map(t):
        bb, u, _ = item(t, 0)
        return bb, 0, k_block + u

    def v_map(t):
        bb, u, _ = item(t, 1)
        return bb, 0, k_block + u

    def o_map(t):
        bb, u, i = item(t, 1)
        return bb, i, u

    in_specs = [pl.BlockSpec(p.shape, lambda t: (0, 0)) for p in params]
    in_specs += [pl.BlockSpec((1, tq, q_width), q_map)]
    in_specs += [pl.BlockSpec((1, a.shape[1], q_width), k_map) for a in k_arrs]
    in_specs += [pl.BlockSpec((1, a.shape[1], 2 * LANES), v_map) for a in v_arrs]
    return pl.pallas_call(
        functools.partial(_pipe_attn_kernel, mode=mode, n_seg=n_seg, lam_init=lam_init),
        grid=(n_items + 1,),
        in_specs=in_specs,
        out_specs=pl.BlockSpec((1, tq, LANES), o_map),
        out_shape=jax.ShapeDtypeStruct((b, sq, units * LANES), BF16),
        scratch_shapes=[pltpu.VMEM((2, tq, sk), F32), pltpu.VMEM((2, tq, sk), F32),
                        pltpu.VMEM((2, tq, LANES), F32), pltpu.VMEM((2, tq, LANES), F32)],
        compiler_params=_cparams(1),
        name=mode + "_pipe_attn",
    )(*params, q_arr, *k_arrs, *v_arrs)


def _diff_qkv_kernel(x_ref, g_ref, sh_ref, sc_ref, w_ref, cos_ref, sin_ref, o_ref, *, rope, qscale):
    h = _norm_mod(x_ref[...], g_ref[...], sh_ref[0], sc_ref[0]).astype(BF16)
    d = x_ref.shape[1]
    acc = jnp.dot(h, w_ref[...], preferred_element_type=F32)
    heads = d // LANES
    ones = jnp.ones((x_ref.shape[0], LANES), BF16)
    for tile in range(2 * heads):
        a = acc[:, tile * LANES:(tile + 1) * LANES]
        if rope:
            a = _rope_tile(a, cos_ref[...], sin_ref[...], DIFF_HEAD_DIM // 4)
        if tile < heads:
            a = a * qscale
        o_ref[:, tile * LANES:(tile + 1) * LANES] = a.astype(BF16)
    for hd in range(heads):
        src = (2 * heads + hd) * LANES
        dst = (2 * heads + 2 * hd) * LANES
        o_ref[:, dst:dst + LANES] = acc[:, src:src + LANES].astype(BF16)
        o_ref[:, dst + LANES:dst + 2 * LANES] = ones


def _diff_qkv(x, g, sh, sc, w, cos, sin, *, rope, tm):
    t, d = x.shape
    nb = sh.shape[0]
    tpb = t // nb // tm
    spb = t // nb // tm
    n = w.shape[1] + d
    qscale = DIFF_HEAD_DIM ** -0.5 * LOG2E
    tab = pl.BlockSpec((tm, LANES), (lambda i: (i % spb, 0)) if rope else (lambda i: (0, 0)))
    return pl.pallas_call(
        functools.partial(_diff_qkv_kernel, rope=rope, qscale=qscale),
        grid=(t // tm,),
        in_specs=[
            pl.BlockSpec((tm, d), lambda i: (i, 0)),
            _resident((1, d)), _mod_spec(tpb, d), _mod_spec(tpb, d),
            _resident(w.shape), tab, tab,
        ],
        out_specs=pl.BlockSpec((tm, n), lambda i: (i, 0)),
        out_shape=jax.ShapeDtypeStruct((t, n), BF16),
        compiler_params=_cparams(1),
        name="diff_qkv",
    )(x, g, sh, sc, w, cos, sin)


def _diff_attn_kernel(lq1_ref, lk1_ref, lq2_ref, lk2_ref, sg_ref, q_ref, *refs, n_seg, lam_init):
    k_refs, v_refs, o_ref = refs[:n_seg], refs[n_seg:2 * n_seg], refs[2 * n_seg]
    lam = (jnp.exp(jnp.sum(lq1_ref[...] * lk1_ref[...], axis=-1, keepdims=True))
           - jnp.exp(jnp.sum(lq2_ref[...] * lk2_ref[...], axis=-1, keepdims=True)) + lam_init)
    q = q_ref[0]
    lane = lax.broadcasted_iota(jnp.int32, q.shape, 1)
    zero = jnp.zeros_like(q)
    o1 = _attend(jnp.where(lane < DIFF_HEAD_DIM, q, zero), k_refs, v_refs)
    o2 = _attend(jnp.where(lane >= DIFF_HEAD_DIM, q, zero), k_refs, v_refs)
    o = o1 - lam * o2
    o_ref[0] = (_rms(o, sg_ref[...]) * (1.0 - lam_init)).astype(BF16)


def _diff_attn(lams, subln_g, q_arr, kv_arrs, *, heads, tq, lam_init):
    b, sq, _ = q_arr.shape
    n_seg = len(kv_arrs)
    lam_specs = [pl.BlockSpec(l.shape, lambda bb, h, i: (0, 0)) for l in lams]
    in_specs = lam_specs + [
        pl.BlockSpec((1, DIFF_V_DIM), lambda bb, h, i: (0, 0)),
        pl.BlockSpec((1, tq, LANES), lambda bb, h, i: (bb, i, h)),
    ]
    in_specs += [pl.BlockSpec((1, a.shape[1], LANES), lambda bb, h, i: (bb, 0, heads + h))
                 for a in kv_arrs]
    in_specs += [pl.BlockSpec((1, a.shape[1], 2 * LANES), lambda bb, h, i: (bb, 0, heads + h))
                 for a in kv_arrs]
    return pl.pallas_call(
        functools.partial(_diff_attn_kernel, n_seg=n_seg, lam_init=lam_init),
        grid=(b, heads, sq // tq),
        in_specs=in_specs,
        out_specs=pl.BlockSpec((1, tq, LANES), lambda bb, h, i: (bb, i, h)),
        out_shape=jax.ShapeDtypeStruct((b, sq, heads * DIFF_V_DIM), BF16),
        compiler_params=_cparams(3),
        name="diff_attn",
    )(*lams, subln_g, q_arr, *kv_arrs, *kv_arrs)


def _mla_down_kernel(x_ref, g_ref, sh_ref, sc_ref, w_ref, qg_ref, kvg_ref, cos_ref, sin_ref,
                     cq_ref, ckv_ref, kr_ref, *, rope):
    h = _norm_mod(x_ref[...], g_ref[...], sh_ref[0], sc_ref[0]).astype(BF16)
    acc = jnp.dot(h, w_ref[...], preferred_element_type=F32)
    cq_ref[...] = _rms(acc[:, :MLA_Q_RANK], qg_ref[...]).astype(BF16)
    ckv_ref[...] = _rms(acc[:, MLA_Q_RANK:MLA_Q_RANK + MLA_KV_RANK], kvg_ref[...]).astype(BF16)
    kr = acc[:, MLA_Q_RANK + MLA_KV_RANK:]
    if rope:
        kr = _rope_tile(kr, cos_ref[...], sin_ref[...], MLA_ROPE_DIM // 4)
    kr_ref[...] = kr.astype(BF16)


def _mla_up_kernel(cq_ref, ckv_ref, kr_ref, wq_ref, wk_ref, wv_ref, cos_ref, sin_ref,
                   q_ref, k_ref, v_ref, *, rope, qscale):
    q = jnp.dot(cq_ref[...], wq_ref[...], preferred_element_type=F32)
    for hd in range(MLA_HEADS):
        a = q[:, hd * LANES:(hd + 1) * LANES]
        if rope:
            a = _rope_tile(a, cos_ref[...], sin_ref[...], MLA_ROPE_DIM // 4)
        q_ref[:, hd * LANES:(hd + 1) * LANES] = (a * qscale).astype(BF16)
    k = jnp.dot(ckv_ref[...], wk_ref[...], preferred_element_type=F32)
    kr = kr_ref[...].astype(F32)
    for hd in range(MLA_HEADS):
        k_ref[:, hd * LANES:(hd + 1) * LANES] = (k[:, hd * LANES:(hd + 1) * LANES] + kr).astype(BF16)
    v = jnp.dot(ckv_ref[...], wv_ref[...], preferred_element_type=F32)
    ones = jnp.ones((v.shape[0], LANES), BF16)
    for pair in range(MLA_HEADS // 2):
        v_ref[:, 2 * pair * LANES:(2 * pair + 1) * LANES] = v[:, pair * LANES:(pair + 1) * LANES].astype(BF16)
        v_ref[:, (2 * pair + 1) * LANES:(2 * pair + 2) * LANES] = ones


def _mla_project(x, g, sh, sc, w_down, qg, kvg, wq, wk, wv, cos, sin, *, rope, tm):
    t, d = x.shape
    nb = sh.shape[0]
    tpb = t // nb // tm
    tab = pl.BlockSpec((tm, LANES), (lambda i: (i % tpb, 0)) if rope else (lambda i: (0, 0)))
    row = lambda n: pl.BlockSpec((tm, n), lambda i: (i, 0))
    cq, ckv, kr = pl.pallas_call(
        functools.partial(_mla_down_kernel, rope=rope),
        grid=(t // tm,),
        in_specs=[row(d), _resident((1, d)), _mod_spec(tpb, d), _mod_spec(tpb, d),
                  _resident(w_down.shape), _resident(qg.shape), _resident(kvg.shape), tab, tab],
        out_specs=[row(MLA_Q_RANK), row(MLA_KV_RANK), row(LANES)],
        out_shape=[jax.ShapeDtypeStruct((t, MLA_Q_RANK), BF16),
                   jax.ShapeDtypeStruct((t, MLA_KV_RANK), BF16),
                   jax.ShapeDtypeStruct((t, LANES), BF16)],
        compiler_params=_cparams(1),
        name="mla_down",
    )(x, g, sh, sc, w_down, qg, kvg, cos, sin)
    qscale = (MLA_NOPE_DIM + MLA_ROPE_DIM) ** -0.5 * LOG2E
    nq, nv = wq.shape[1], 2 * wv.shape[1]
    return pl.pallas_call(
        functools.partial(_mla_up_kernel, rope=rope, qscale=qscale),
        grid=(t // tm,),
        in_specs=[row(MLA_Q_RANK), row(MLA_KV_RANK), row(LANES),
                  _resident(wq.shape), _resident(wk.shape), _resident(wv.shape), tab, tab],
        out_specs=[row(nq), row(nq), row(nv)],
        out_shape=[jax.ShapeDtypeStruct((t, nq), BF16), jax.ShapeDtypeStruct((t, nq), BF16),
                   jax.ShapeDtypeStruct((t, nv), BF16)],
        compiler_params=_cparams(1),
        name="mla_up",
    )(cq, ckv, kr, wq, wk, wv, cos, sin)


def _mla_attn_kernel(q_ref, *refs, n_seg):
    k_refs, v_refs, o_ref = refs[:n_seg], refs[n_seg:2 * n_seg], refs[2 * n_seg]
    outs = []
    for hd in range(2):
        q = q_ref[0, :, hd * LANES:(hd + 1) * LANES]
        ks = [k.at[:, :, hd * LANES:(hd + 1) * LANES] for k in k_refs]
        outs.append(_attend(q, ks, v_refs))
    lane = lax.broadcasted_iota(jnp.int32, outs[0].shape, 1)
    o_ref[0] = jnp.where(lane < MLA_V_DIM, outs[0], outs[1]).astype(BF16)


def _mla_attn(q, k_arrs, v_arrs, *, tq):
    b, sq, nq = q.shape
    pairs = nq // (2 * LANES)
    n_seg = len(k_arrs)
    in_specs = [pl.BlockSpec((1, tq, 2 * LANES), lambda bb, h, i: (bb, i, h))]
    in_specs += [pl.BlockSpec((1, a.shape[1], 2 * LANES), lambda bb, h, i: (bb, 0, h)) for a in k_arrs]
    in_specs += [pl.BlockSpec((1, a.shape[1], 2 * LANES), lambda bb, h, i: (bb, 0, h)) for a in v_arrs]
    return pl.pallas_call(
        functools.partial(_mla_attn_kernel, n_seg=n_seg),
        grid=(b, pairs, sq // tq),
        in_specs=in_specs,
        out_specs=pl.BlockSpec((1, tq, LANES), lambda bb, h, i: (bb, i, h)),
        out_shape=jax.ShapeDtypeStruct((b, sq, pairs * LANES), BF16),
        compiler_params=_cparams(3),
        name="mla_attn",
    )(q, *k_arrs, *v_arrs)


def _mla_weights(w_down, w_uq, w_ukv):
    d = w_down.shape[0]
    hq = MLA_NOPE_DIM + MLA_ROPE_DIM
    pad_r = LANES - hq
    lo = MLA_Q_RANK + MLA_KV_RANK
    zeros = lambda n: jnp.zeros((d, n), w_down.dtype)
    wd = jnp.concatenate([w_down[:, :lo], zeros(MLA_NOPE_DIM), w_down[:, lo:], zeros(pad_r)], axis=1)
    wq = w_uq.reshape(MLA_Q_RANK, MLA_HEADS, hq)
    wq = jnp.pad(wq, ((0, 0), (0, 0), (0, pad_r))).reshape(MLA_Q_RANK, MLA_HEADS * LANES)
    wkv = w_ukv.reshape(MLA_KV_RANK, MLA_HEADS, MLA_NOPE_DIM + MLA_V_DIM)
    wk = jnp.pad(wkv[:, :, :MLA_NOPE_DIM], ((0, 0), (0, 0), (0, LANES - MLA_NOPE_DIM)))
    wk = wk.reshape(MLA_KV_RANK, MLA_HEADS * LANES)
    wv = wkv[:, :, MLA_NOPE_DIM:].reshape(MLA_KV_RANK, MLA_HEADS * MLA_V_DIM)
    return wd.astype(BF16), wq.astype(BF16), wk.astype(BF16), wv.astype(BF16)


def kernel(x, c, ctx, c_ctx, mod_w, mod_b, norm_mix_g, norm_mlp_g, final_g, mlp_w1, mlp_w2, fourier_w, fourier_b, diff_w_qkv, diff_lambda_q1, diff_lambda_k1, diff_lambda_q2, diff_lambda_k2, diff_subln_g, diff_w_o, mla_w_down, mla_q_norm_g, mla_kv_norm_g, mla_w_uq, mla_w_ukv, mla_w_o):
    batch, seq, d = x.shape
    ctx_len = ctx.shape[1]
    depth = mod_w.shape[0]
    rows = seq // GRID_W
    assert seq == FFT_RADIX * FFT_RADIX and d == FOURIER_GROUPS * 256 and ctx_len == 256
    tm = 512

    cc_rows = -(-(batch + 1) // 8) * 8
    cc = jnp.zeros((cc_rows, d), F32).at[:batch].set(c).at[batch].set(c_ctx)
    mods = _modulation(cc, mod_w, mod_b).reshape(depth, cc_rows, 6, d)
    mats = _dft_mats()
    zero_bias = jnp.zeros((1, d), F32)
    final_g2 = final_g.reshape(1, d)

    lane_diff = lambda l: l % DIFF_HEAD_DIM
    cos_d, sin_d = _rope_tables(rows, DIFF_HEAD_DIM, lane_diff)
    lane_mla = lambda l: l - MLA_NOPE_DIM if MLA_NOPE_DIM <= l < MLA_NOPE_DIM + MLA_ROPE_DIM else -1
    cos_m, sin_m = _rope_tables(rows, MLA_ROPE_DIM, lane_mla)

    x_lat = x.reshape(batch * seq, d)
    x_ctx = ctx.reshape(batch * ctx_len, d)
    for i in range(depth):
        kind, j = i % N_MIXERS, i // N_MIXERS
        update_ctx = i < depth - 1
        ctx_feeds_mixer = update_ctx or kind != 0
        lat = [mods[i, :batch, k].reshape(batch, 1, d) for k in range(6)]
        cm = [mods[i, batch:batch + 1, k].reshape(1, 1, d) for k in range(6)]
        g_mix = norm_mix_g[i].reshape(1, d)
        g_mlp = norm_mlp_g[i].reshape(1, d)
        a_ctx = None
        if kind == 0:
            w_o = fourier_w[j].astype(BF16)
            b_o = fourier_b[j].reshape(1, d)
            a_lat = _fourier_front(x_lat, g_mix, lat[0], lat[1], mats, batch=batch)
            if update_ctx:
                a_ctx = _fourier_ctx(x_ctx, g_mix, cm[0], cm[1], mats, ctx_len=ctx_len)
        elif kind == 1:
            heads = d // DIFF_V_DIM
            w_qkv = diff_w_qkv[j].astype(BF16)
            w_o = diff_w_o[j].astype(BF16)
            b_o = zero_bias
            lam_init = 0.8 - 0.6 * math.exp(-0.3 * i)
            lams = [v[j].reshape(1, DIFF_HEAD_DIM) for v in
                    (diff_lambda_q1, diff_lambda_k1, diff_lambda_q2, diff_lambda_k2)]
            sg = diff_subln_g[j].reshape(1, DIFF_V_DIM)
            qkv_l = _diff_qkv(x_lat, g_mix, lat[0], lat[1], w_qkv, cos_d, sin_d, rope=True, tm=tm)
            qkv_l = qkv_l.reshape(batch, seq, 4 * d)
            qkv_c = _diff_qkv(x_ctx, g_mix, cm[0], cm[1], w_qkv, cos_d, sin_d, rope=False, tm=ctx_len)
            qkv_c = qkv_c.reshape(batch, ctx_len, 4 * d)
            o_l = _pipe_attn(lams + [sg], qkv_l, [qkv_c, qkv_l], [qkv_c, qkv_l], mode="diff",
                             units=heads, q_width=LANES, k_block=heads, tq=256, lam_init=lam_init)
            a_lat = o_l.reshape(batch * seq, d)
            if update_ctx:
                o_c = _diff_attn(lams, sg, qkv_c, [qkv_c], heads=heads, tq=ctx_len, lam_init=lam_init)
                a_ctx = o_c.reshape(batch * ctx_len, d)
        else:
            wd, wq, wk, wv = _mla_weights(mla_w_down[j], mla_w_uq[j], mla_w_ukv[j])
            w_o = mla_w_o[j].astype(BF16)
            b_o = zero_bias
            qg = mla_q_norm_g[j].reshape(1, MLA_Q_RANK)
            kvg = mla_kv_norm_g[j].reshape(1, MLA_KV_RANK)
            q_l, k_l, v_l = _mla_project(x_lat, g_mix, lat[0], lat[1], wd, qg, kvg, wq, wk, wv,
                                         cos_m, sin_m, rope=True, tm=tm)
            q_c, k_c, v_c = _mla_project(x_ctx, g_mix, cm[0], cm[1], wd, qg, kvg, wq, wk, wv,
                                         cos_m, sin_m, rope=False, tm=ctx_len)
            r3 = lambda a, s: a.reshape(batch, s, a.shape[1])
            o_l = _pipe_attn([], r3(q_l, seq), [r3(k_c, ctx_len), r3(k_l, seq)],
                             [r3(v_c, ctx_len), r3(v_l, seq)], mode="mla", units=MLA_HEADS // 2,
                             q_width=2 * LANES, k_block=0, tq=256)
            a_lat = o_l.reshape(batch * seq, d)
            if update_ctx:
                o_c = _mla_attn(r3(q_c, ctx_len), [r3(k_c, ctx_len)], [r3(v_c, ctx_len)], tq=ctx_len)
                a_ctx = o_c.reshape(batch * ctx_len, d)
        w1 = mlp_w1[i].astype(BF16)
        w2 = mlp_w2[i].astype(BF16)
        last = i == depth - 1
        if kind == 0:
            x_lat = _fourier_tail(a_lat, mats, w_o, b_o, lat[2], x_lat, g_mlp, lat[3], lat[4], lat[5],
                                  w1, w2, final_g2, batch=batch, final_norm=last)
        else:
            x_lat = _block_tail(a_lat, w_o, b_o, lat[2], x_lat, g_mlp, lat[3], lat[4], lat[5], w1, w2,
                                final_g2, final_norm=last, tm=tm)
        if update_ctx:
            x_ctx = _block_tail(a_ctx, w_o, b_o, cm[2], x_ctx, g_mlp, cm[3], cm[4], cm[5], w1, w2,
                                final_g2, final_norm=False, tm=ctx_len)
    return x_lat.reshape(batch, seq, d)
```

```python
import functools
import math

import numpy as np
import jax
import jax.numpy as jnp
from jax import lax
from jax.experimental import pallas as pl
from jax.experimental.pallas import tpu as pltpu

F32 = jnp.float32
BF16 = jnp.bfloat16

GRID_W = 64
RMS_EPS = 1e-6
ROPE_BASE = 10000.0
FOURIER_GROUPS = 4
DIFF_HEAD_DIM = 64
DIFF_V_DIM = 128
MLA_HEADS = 16
MLA_NOPE_DIM = 64
MLA_ROPE_DIM = 32
MLA_V_DIM = 64
MLA_Q_RANK = 256
MLA_KV_RANK = 128
N_MIXERS = 3

LANES = 128
FFT_RADIX = 64
VMEM_LIMIT = 56 * 1024 * 1024
LOG2E = 1.4426950408889634


def _cparams(n_axes):
    return pltpu.CompilerParams(
        dimension_semantics=("arbitrary",) * n_axes, vmem_limit_bytes=VMEM_LIMIT)


def _resident(shape):
    nd = len(shape)
    return pl.BlockSpec(shape, lambda *_: (0,) * nd, pipeline_mode=pl.Buffered(1))


def _norm_mod(x, g, sh, sc):
    r = lax.rsqrt(jnp.mean(x * x, axis=-1, keepdims=True) + RMS_EPS)
    return ((x * r) * g) * (1.0 + sc) + sh


def _rms(x, g):
    r = lax.rsqrt(jnp.mean(x * x, axis=-1, keepdims=True) + RMS_EPS)
    return (x * r) * g


def _rope_tile(x, cos, sin, half):
    lane = lax.broadcasted_iota(jnp.int32, x.shape, 1)
    fwd = pltpu.roll(x, LANES - half, 1)
    bwd = pltpu.roll(x, half, 1)
    partner = jnp.where((lane & half) == 0, fwd, bwd)
    return x * cos + partner * sin


def _mod_kernel(c_ref, w_ref, b_ref, o_ref):
    c = c_ref[...]
    s = (c / (1.0 + jnp.exp(-c))).astype(BF16)
    o_ref[0] = jnp.dot(s, w_ref[0].astype(BF16), preferred_element_type=F32) + b_ref[0]


def _modulation(cc, mod_w, mod_b):
    depth, d, n = mod_w.shape
    rows = cc.shape[0]
    tn = n // 4
    return pl.pallas_call(
        _mod_kernel,
        grid=(depth, n // tn),
        in_specs=[
            pl.BlockSpec((rows, d), lambda i, j: (0, 0)),
            pl.BlockSpec((1, d, tn), lambda i, j: (i, 0, j)),
            pl.BlockSpec((1, 1, tn), lambda i, j: (i, 0, j)),
        ],
        out_specs=pl.BlockSpec((1, rows, tn), lambda i, j: (i, 0, j)),
        out_shape=jax.ShapeDtypeStruct((depth, rows, n), F32),
        compiler_params=_cparams(2),
        name="modulation",
    )(cc, mod_w, mod_b.reshape(depth, 1, n))


def _mod_spec(tiles_per_batch, d):
    return pl.BlockSpec((1, 1, d), lambda i: (i // tiles_per_batch, 0, 0))


FF_CHUNK = 1024


def _tail_math(a, x, wo_ref, bo_ref, ga_ref, g_ref, sh_ref, sc_ref, gm_ref, w1_ref, w2_ref, fg_ref,
               final_norm):
    y = jnp.dot(a.astype(BF16), wo_ref[...], preferred_element_type=F32) + bo_ref[...]
    x = x + ga_ref[0] * y
    h = _norm_mod(x, g_ref[...], sh_ref[0], sc_ref[0]).astype(BF16)
    acc = jnp.zeros(x.shape, F32)
    for c in range(w1_ref.shape[1] // FF_CHUNK):
        u = jnp.dot(h, w1_ref[:, c * FF_CHUNK:(c + 1) * FF_CHUNK], preferred_element_type=F32)
        u = jnp.square(jnp.maximum(u, 0.0)).astype(BF16)
        acc = acc + jnp.dot(u, w2_ref[c * FF_CHUNK:(c + 1) * FF_CHUNK, :],
                            preferred_element_type=F32)
    out = x + gm_ref[0] * acc
    if final_norm:
        out = _rms(out, fg_ref[...])
    return out


def _block_tail_kernel(a_ref, x_ref, *refs, final_norm):
    o_ref = refs[-1]
    o_ref[...] = _tail_math(a_ref[...], x_ref[...], *refs[:-1], final_norm=final_norm)


def _block_tail(a, w_o, b_o, gate_a, x, g, sh, sc, gate_m, w1, w2, final_g, *, final_norm, tm):
    t, d = x.shape
    nb = sh.shape[0]
    tpb = t // nb // tm
    row = lambda n: pl.BlockSpec((tm, n), lambda i: (i, 0))
    return pl.pallas_call(
        functools.partial(_block_tail_kernel, final_norm=final_norm),
        grid=(t // tm,),
        in_specs=[
            row(a.shape[1]), row(d),
            _resident(w_o.shape), _resident((1, d)), _mod_spec(tpb, d),
            _resident((1, d)), _mod_spec(tpb, d), _mod_spec(tpb, d), _mod_spec(tpb, d),
            _resident(w1.shape), _resident(w2.shape), _resident((1, d)),
        ],
        out_specs=row(d),
        out_shape=jax.ShapeDtypeStruct((t, d), F32),
        compiler_params=_cparams(1),
        name="block_tail",
    )(a, x, w_o, b_o, gate_a, g, sh, sc, gate_m, w1, w2, final_g)


def _dft_mats():
    cg = 256
    j = np.arange(cg)
    th = 2.0 * np.pi * np.outer(j, j) / cg
    wc = np.concatenate([np.cos(th), -np.sin(th)], axis=1) / 16.0
    ctx_seq = np.concatenate([np.cos(th), np.sin(th)], axis=1) / 16.0
    r = np.arange(FFT_RADIX)
    ph = 2.0 * np.pi * np.outer(r, r) / FFT_RADIX
    dr, di = np.cos(ph) / 8.0, -np.sin(ph) / 8.0
    m1 = np.block([[dr, -di], [di, dr]])
    m2 = np.concatenate([dr, -di], axis=1)
    eye = np.eye(FFT_GROUP)
    m1 = np.einsum("pq,jk->jpqk", m1, eye).reshape(m1.shape[0] * FFT_GROUP, m1.shape[1] * FFT_GROUP)
    m2 = np.einsum("pq,jk->pjqk", m2, eye).reshape(m2.shape[0] * FFT_GROUP, m2.shape[1] * FFT_GROUP)
    tw = 2.0 * np.pi * np.outer(r, r) / (FFT_RADIX * FFT_RADIX)
    tr = np.repeat(np.cos(tw)[:, :, None], LANES, axis=2)
    ti = np.repeat(-np.sin(tw)[:, :, None], LANES, axis=2)
    as_bf = lambda a: jnp.asarray(a, F32).astype(BF16)
    return dict(wc=as_bf(wc), ctx_seq=as_bf(ctx_seq), m1=as_bf(m1), m2=as_bf(m2),
                tr=jnp.asarray(tr, F32), ti=jnp.asarray(ti, F32))


def _chan_dft(h, wc_ref):
    cg = wc_ref.shape[0]
    res, ims = [], []
    for grp in range(h.shape[1] // cg):
        z = jnp.dot(h[:, grp * cg:(grp + 1) * cg], wc_ref[...], preferred_element_type=F32)
        res.append(z[:, :cg])
        ims.append(z[:, cg:])
    return jnp.concatenate(res, axis=1), jnp.concatenate(ims, axis=1)


def _pack_complex(re, im):
    bits = lambda v: lax.bitcast_convert_type(v.astype(BF16).astype(F32), jnp.uint32)
    return (bits(re) >> 16) | bits(im)


def _unpack_complex(w):
    re = lax.bitcast_convert_type(w << 16, F32)
    im = lax.bitcast_convert_type(w & jnp.uint32(0xFFFF0000), F32)
    return jnp.concatenate([re, im], axis=0).astype(BF16)


FFT_GROUP = 8


def _digit_spec(d):
    return pl.BlockSpec((None, FFT_RADIX, FFT_GROUP, d), lambda b, j: (b, 0, j, 0))


def _fourier_front_kernel(x_ref, g_ref, sh_ref, sc_ref, wc_ref, tr_ref, ti_ref, w1_ref, a_ref):
    r, kg = FFT_RADIX, FFT_GROUP
    d = x_ref.shape[-1]
    reps = d // LANES
    x = x_ref[...].reshape(r * kg, d)
    h = _norm_mod(x, g_ref[...], sh_ref[0], sc_ref[0]).astype(BF16)
    zr, zi = _chan_dft(h, wc_ref)
    z = jnp.concatenate([zr, zi], axis=0).astype(BF16)
    a = jnp.dot(w1_ref[...], z, preferred_element_type=F32)
    for j in range(kg):
        ar = a[2 * r * j:2 * r * j + r]
        ai = a[2 * r * j + r:2 * r * (j + 1)]
        tr = jnp.tile(tr_ref[j], (1, reps))
        ti = jnp.tile(ti_ref[j], (1, reps))
        a_ref[0, j] = _pack_complex(ar * tr - ai * ti, ar * ti + ai * tr)


def _fourier_front(x, g, sh, sc, mats, *, batch):
    t, d = x.shape
    r, kg = FFT_RADIX, FFT_GROUP
    const = lambda a: pl.BlockSpec(a.shape, lambda b, j: (0,) * a.ndim)
    mod = pl.BlockSpec((1, 1, d), lambda b, j: (b, 0, 0))
    tw = pl.BlockSpec((kg, r, LANES), lambda b, j: (j, 0, 0))
    return pl.pallas_call(
        _fourier_front_kernel,
        grid=(batch, r // kg),
        in_specs=[_digit_spec(d), const(g), mod, mod, const(mats["wc"]), tw, tw, const(mats["m1"])],
        out_specs=pl.BlockSpec((1, kg, r, d), lambda b, j: (b, j, 0, 0)),
        out_shape=jax.ShapeDtypeStruct((batch, r, r, d), jnp.uint32),
        compiler_params=_cparams(2),
        name="fourier_front",
    )(x.reshape(batch, r, r, d), g, sh, sc, mats["wc"], mats["tr"], mats["ti"], mats["m1"])


def _fourier_tail_kernel(a_ref, m2_ref, x_ref, *refs, final_norm):
    r, kg = FFT_RADIX, FFT_GROUP
    d = a_ref.shape[-1]
    o_ref = refs[-1]
    z = _unpack_complex(a_ref[...].reshape(r * kg, d))
    f = jnp.dot(m2_ref[...], z, preferred_element_type=F32)
    out = _tail_math(f, x_ref[...].reshape(r * kg, d), *refs[:-1], final_norm=final_norm)
    o_ref[...] = out.reshape(r, kg, d)


def _fourier_tail(a, mats, w_o, b_o, gate_a, x, g, sh, sc, gate_m, w1, w2, final_g, *, batch,
                  final_norm):
    t, d = x.shape
    r, kg = FFT_RADIX, FFT_GROUP
    const = lambda v: pl.BlockSpec(v.shape, lambda b, j: (0,) * v.ndim, pipeline_mode=pl.Buffered(1))
    mod = pl.BlockSpec((1, 1, d), lambda b, j: (b, 0, 0))
    out = pl.pallas_call(
        functools.partial(_fourier_tail_kernel, final_norm=final_norm),
        grid=(batch, r // kg),
        in_specs=[_digit_spec(d), const(mats["m2"]), _digit_spec(d),
                  const(w_o), const(b_o), mod, const(g), mod, mod, mod,
                  const(w1), const(w2), const(final_g)],
        out_specs=_digit_spec(d),
        out_shape=jax.ShapeDtypeStruct((batch, r, r, d), F32),
        compiler_params=_cparams(2),
        name="fourier_tail",
    )(a, mats["m2"], x.reshape(batch, r, r, d), w_o, b_o, gate_a, g, sh, sc, gate_m, w1, w2, final_g)
    return out.reshape(t, d)


def _fourier_ctx_kernel(x_ref, g_ref, sh_ref, sc_ref, wc_ref, seq_ref, f_ref):
    h = _norm_mod(x_ref[...], g_ref[...], sh_ref[0], sc_ref[0]).astype(BF16)
    zr, zi = _chan_dft(h, wc_ref)
    z = jnp.concatenate([zr, zi], axis=0).astype(BF16)
    f_ref[...] = jnp.dot(seq_ref[...], z, preferred_element_type=F32)


def _fourier_ctx(x, g, sh, sc, mats, *, ctx_len):
    t, d = x.shape
    n_tiles = t // ctx_len
    return pl.pallas_call(
        _fourier_ctx_kernel,
        grid=(n_tiles,),
        in_specs=[
            pl.BlockSpec((ctx_len, d), lambda i: (i, 0)),
            _resident((1, d)), _mod_spec(n_tiles, d), _mod_spec(n_tiles, d),
            _resident(mats["wc"].shape), _resident(mats["ctx_seq"].shape),
        ],
        out_specs=pl.BlockSpec((ctx_len, d), lambda i: (i, 0)),
        out_shape=jax.ShapeDtypeStruct((t, d), F32),
        compiler_params=_cparams(1),
        name="fourier_ctx",
    )(x, g, sh, sc, mats["wc"], mats["ctx_seq"])


def _rope_tables(rows, dim, lane_of_dim):
    n_freq = dim // 4
    inv_freq = ROPE_BASE ** (-jnp.arange(n_freq, dtype=F32) / n_freq)
    t = jnp.arange(rows * GRID_W)
    row = (t // GRID_W).astype(F32)
    col = (t % GRID_W).astype(F32)
    ang = jnp.stack([row[:, None] * inv_freq, col[:, None] * inv_freq], axis=1)
    cos, sin = jnp.cos(ang), jnp.sin(ang)
    dims = np.array([lane_of_dim(l) for l in range(LANES)])
    live = dims >= 0
    dd = np.where(live, dims, 0)
    axis, pair, freq = dd // (2 * n_freq), (dd % (2 * n_freq)) // n_freq, dd % n_freq
    sign = np.where(pair == 0, -1.0, 1.0).astype(np.float32)
    cos_t = jnp.where(live[None, :], cos[:, axis, freq], 1.0)
    sin_t = jnp.where(live[None, :], sin[:, axis, freq] * sign[None, :], 0.0)
    return cos_t.astype(F32), sin_t.astype(F32)


def _scores(q, k_refs):
    nt = (((1,), (1,)), ((), ()))
    return [lax.dot_general(q, k[0], nt, preferred_element_type=F32) for k in k_refs]


def _attend(q, k_refs, vaug_refs):
    parts = _scores(q, k_refs)
    m = functools.reduce(jnp.maximum, [jnp.max(p, axis=-1, keepdims=True) for p in parts])
    acc = None
    for p, v in zip(parts, vaug_refs):
        e = jnp.exp2(p - m).astype(BF16)
        part = jnp.dot(e, v[0], preferred_element_type=F32)
        acc = part if acc is None else acc + part
    return acc[:, :LANES] / acc[:, LANES:]


def _pipe_attn_kernel(*refs, mode, n_seg, lam_init):
    n_par = 5 if mode == "diff" else 0
    par = refs[:n_par]
    q_ref = refs[n_par]
    k_refs = refs[n_par + 1:n_par + 1 + n_seg]
    v_refs = refs[n_par + 1 + n_seg:n_par + 1 + 2 * n_seg]
    o_ref, s_a, s_b, m_a, m_b = refs[n_par + 1 + 2 * n_seg:]
    t = pl.program_id(0)
    nt = (((1,), (1,)), ((), ()))

    def scores(s_w, m_w):
        q = q_ref[0]
        if mode == "diff":
            lane = lax.broadcasted_iota(jnp.int32, q.shape, 1)
            zero = jnp.zeros_like(q)
            qs = [jnp.where(lane < DIFF_HEAD_DIM, q, zero), jnp.where(lane >= DIFF_HEAD_DIM, q, zero)]
        else:
            qs = [q[:, :LANES], q[:, LANES:]]
        for c in range(2):
            m, off = None, 0
            for k in k_refs:
                kk = k[0] if mode == "diff" else k[0, :, c * LANES:(c + 1) * LANES]
                p = lax.dot_general(qs[c], kk, nt, preferred_element_type=F32)
                s_w[c, :, off:off + p.shape[1]] = p
                off += p.shape[1]
                pm = jnp.max(p, axis=-1, keepdims=True)
                m = pm if m is None else jnp.maximum(m, pm)
            m_w[c] = jnp.broadcast_to(m, m_w.shape[1:])

    def values(s_r, m_r):
        outs = []
        for c in range(2):
            m = m_r[c][:, :1]
            acc, off = None, 0
            for v in v_refs:
                w = v.shape[1]
                e = jnp.exp2(s_r[c, :, off:off + w] - m).astype(BF16)
                part = jnp.dot(e, v[0], preferred_element_type=F32)
                acc = part if acc is None else acc + part
                off += w
            outs.append(acc[:, :LANES] / acc[:, LANES:])
        if mode == "diff":
            lq1, lk1, lq2, lk2, sg = par
            lam = (jnp.exp(jnp.sum(lq1[...] * lk1[...], axis=-1, keepdims=True))
                   - jnp.exp(jnp.sum(lq2[...] * lk2[...], axis=-1, keepdims=True)) + lam_init)
            o = outs[0] - lam * outs[1]
            o_ref[0] = (_rms(o, sg[...]) * (1.0 - lam_init)).astype(BF16)
        else:
            lane = lax.broadcasted_iota(jnp.int32, outs[0].shape, 1)
            o_ref[0] = jnp.where(lane < MLA_V_DIM, outs[0], outs[1]).astype(BF16)

    @pl.when(t == 0)
    def _():
        s_b[...] = jnp.zeros(s_b.shape, F32)
        m_b[...] = jnp.zeros(m_b.shape, F32)

    @pl.when(t % 2 == 0)
    def _():
        scores(s_a, m_a)
        values(s_b, m_b)

    @pl.when(t % 2 == 1)
    def _():
        scores(s_b, m_b)
        values(s_a, m_a)


def _pipe_attn(params, q_arr, k_arrs, v_arrs, *, mode, units, q_width, k_block, tq, lam_init=0.0):
    b, sq, _ = q_arr.shape
    nq = sq // tq
    n_items = b * units * nq
    n_seg = len(k_arrs)
    sk = sum(a.shape[1] for a in k_arrs)

    def item(t, lag):
        tt = jnp.clip(t - lag, 0, n_items - 1)
        return tt // (units * nq), (tt // nq) % units, tt % nq

    def q_map(t):
        bb, u, i = item(t, 0)
        return bb, i, u

    def k_map(t):
        bb, u, _ = item(t, 0)
        return bb, 0, k_block + u

    def v_map(t):
        bb, u, _ = item(t, 1)
        return bb, 0, k_block + u

    def o_map(t):
        bb, u, i = item(t, 1)
        return bb, i, u

    in_specs = [pl.BlockSpec(p.shape, lambda t: (0, 0)) for p in params]
    in_specs += [pl.BlockSpec((1, tq, q_width), q_map)]
    in_specs += [pl.BlockSpec((1, a.shape[1], q_width), k_map) for a in k_arrs]
    in_specs += [pl.BlockSpec((1, a.shape[1], 2 * LANES), v_map) for a in v_arrs]
    return pl.pallas_call(
        functools.partial(_pipe_attn_kernel, mode=mode, n_seg=n_seg, lam_init=lam_init),
        grid=(n_items + 1,),
        in_specs=in_specs,
        out_specs=pl.BlockSpec((1, tq, LANES), o_map),
        out_shape=jax.ShapeDtypeStruct((b, sq, units * LANES), BF16),
        scratch_shapes=[pltpu.VMEM((2, tq, sk), F32), pltpu.VMEM((2, tq, sk), F32),
                        pltpu.VMEM((2, tq, LANES), F32), pltpu.VMEM((2, tq, LANES), F32)],
        compiler_params=_cparams(1),
        name=mode + "_pipe_attn",
    )(*params, q_arr, *k_arrs, *v_arrs)


def _diff_qkv_kernel(x_ref, g_ref, sh_ref, sc_ref, w_ref, cos_ref, sin_ref, o_ref, *, rope, qscale):
    h = _norm_mod(x_ref[...], g_ref[...], sh_ref[0], sc_ref[0]).astype(BF16)
    d = x_ref.shape[1]
    acc = jnp.dot(h, w_ref[...], preferred_element_type=F32)
    heads = d // LANES
    ones = jnp.ones((x_ref.shape[0], LANES), BF16)
    for tile in range(2 * heads):
        a = acc[:, tile * LANES:(tile + 1) * LANES]
        if rope:
            a = _rope_tile(a, cos_ref[...], sin_ref[...], DIFF_HEAD_DIM // 4)
        if tile < heads:
            a = a * qscale
        o_ref[:, tile * LANES:(tile + 1) * LANES] = a.astype(BF16)
    for hd in range(heads):
        src = (2 * heads + hd) * LANES
        dst = (2 * heads + 2 * hd) * LANES
        o_ref[:, dst:dst + LANES] = acc[:, src:src + LANES].astype(BF16)
        o_ref[:, dst + LANES:dst + 2 * LANES] = ones


def _diff_qkv(x, g, sh, sc, w, cos, sin, *, rope, tm):
    t, d = x.shape
    nb = sh.shape[0]
    tpb = t // nb // tm
    spb = t // nb // tm
    n = w.shape[1] + d
    qscale = DIFF_HEAD_DIM ** -0.5 * LOG2E
    tab = pl.BlockSpec((tm, LANES), (lambda i: (i % spb, 0)) if rope else (lambda i: (0, 0)))
    return pl.pallas_call(
        functools.partial(_diff_qkv_kernel, rope=rope, qscale=qscale),
        grid=(t // tm,),
        in_specs=[
            pl.BlockSpec((tm, d), lambda i: (i, 0)),
            _resident((1, d)), _mod_spec(tpb, d), _mod_spec(tpb, d),
            _resident(w.shape), tab, tab,
        ],
        out_specs=pl.BlockSpec((tm, n), lambda i: (i, 0)),
        out_shape=jax.ShapeDtypeStruct((t, n), BF16),
        compiler_params=_cparams(1),
        name="diff_qkv",
    )(x, g, sh, sc, w, cos, sin)


def _diff_attn_kernel(lq1_ref, lk1_ref, lq2_ref, lk2_ref, sg_ref, q_ref, *refs, n_seg, lam_init):
    k_refs, v_refs, o_ref = refs[:n_seg], refs[n_seg:2 * n_seg], refs[2 * n_seg]
    lam = (jnp.exp(jnp.sum(lq1_ref[...] * lk1_ref[...], axis=-1, keepdims=True))
           - jnp.exp(jnp.sum(lq2_ref[...] * lk2_ref[...], axis=-1, keepdims=True)) + lam_init)
    q = q_ref[0]
    lane = lax.broadcasted_iota(jnp.int32, q.shape, 1)
    zero = jnp.zeros_like(q)
    o1 = _attend(jnp.where(lane < DIFF_HEAD_DIM, q, zero), k_refs, v_refs)
    o2 = _attend(jnp.where(lane >= DIFF_HEAD_DIM, q, zero), k_refs, v_refs)
    o = o1 - lam * o2
    o_ref[0] = (_rms(o, sg_ref[...]) * (1.0 - lam_init)).astype(BF16)


def _diff_attn(lams, subln_g, q_arr, kv_arrs, *, heads, tq, lam_init):
    b, sq, _ = q_arr.shape
    n_seg = len(kv_arrs)
    lam_specs = [pl.BlockSpec(l.shape, lambda bb, h, i: (0, 0)) for l in lams]
    in_specs = lam_specs + [
        pl.BlockSpec((1, DIFF_V_DIM), lambda bb, h, i: (0, 0)),
        pl.BlockSpec((1, tq, LANES), lambda bb, h, i: (bb, i, h)),
    ]
    in_specs += [pl.BlockSpec((1, a.shape[1], LANES), lambda bb, h, i: (bb, 0, heads + h))
                 for a in kv_arrs]
    in_specs += [pl.BlockSpec((1, a.shape[1], 2 * LANES), lambda bb, h, i: (bb, 0, heads + h))
                 for a in kv_arrs]
    return pl.pallas_call(
        functools.partial(_diff_attn_kernel, n_seg=n_seg, lam_init=lam_init),
        grid=(b, heads, sq // tq),
        in_specs=in_specs,
        out_specs=pl.BlockSpec((1, tq, LANES), lambda bb, h, i: (bb, i, h)),
        out_shape=jax.ShapeDtypeStruct((b, sq, heads * DIFF_V_DIM), BF16),
        compiler_params=_cparams(3),
        name="diff_attn",
    )(*lams, subln_g, q_arr, *kv_arrs, *kv_arrs)


def _mla_down_kernel(x_ref, g_ref, sh_ref, sc_ref, w_ref, qg_ref, kvg_ref, cos_ref, sin_ref,
                     cq_ref, ckv_ref, kr_ref, *, rope):
    h = _norm_mod(x_ref[...], g_ref[...], sh_ref[0], sc_ref[0]).astype(BF16)
    acc = jnp.dot(h, w_ref[...], preferred_element_type=F32)
    cq_ref[...] = _rms(acc[:, :MLA_Q_RANK], qg_ref[...]).astype(BF16)
    ckv_ref[...] = _rms(acc[:, MLA_Q_RANK:MLA_Q_RANK + MLA_KV_RANK], kvg_ref[...]).astype(BF16)
    kr = acc[:, MLA_Q_RANK + MLA_KV_RANK:]
    if rope:
        kr = _rope_tile(kr, cos_ref[...], sin_ref[...], MLA_ROPE_DIM // 4)
    kr_ref[...] = kr.astype(BF16)


def _mla_up_kernel(cq_ref, ckv_ref, kr_ref, wq_ref, wk_ref, wv_ref, cos_ref, sin_ref,
                   q_ref, k_ref, v_ref, *, rope, qscale):
    q = jnp.dot(cq_ref[...], wq_ref[...], preferred_element_type=F32)
    for hd in range(MLA_HEADS):
        a = q[:, hd * LANES:(hd + 1) * LANES]
        if rope:
            a = _rope_tile(a, cos_ref[...], sin_ref[...], MLA_ROPE_DIM // 4)
        q_ref[:, hd * LANES:(hd + 1) * LANES] = (a * qscale).astype(BF16)
    k = jnp.dot(ckv_ref[...], wk_ref[...], preferred_element_type=F32)
    kr = kr_ref[...].astype(F32)
    for hd in range(MLA_HEADS):
        k_ref[:, hd * LANES:(hd + 1) * LANES] = (k[:, hd * LANES:(hd + 1) * LANES] + kr).astype(BF16)
    v = jnp.dot(ckv_ref[...], wv_ref[...], preferred_element_type=F32)
    ones = jnp.ones((v.shape[0], LANES), BF16)
    for pair in range(MLA_HEADS // 2):
        v_ref[:, 2 * pair * LANES:(2 * pair + 1) * LANES] = v[:, pair * LANES:(pair + 1) * LANES].astype(BF16)
        v_ref[:, (2 * pair + 1) * LANES:(2 * pair + 2) * LANES] = ones


def _mla_project(x, g, sh, sc, w_down, qg, kvg, wq, wk, wv, cos, sin, *, rope, tm):
    t, d = x.shape
    nb = sh.shape[0]
    tpb = t // nb // tm
    tab = pl.BlockSpec((tm, LANES), (lambda i: (i % tpb, 0)) if rope else (lambda i: (0, 0)))
    row = lambda n: pl.BlockSpec((tm, n), lambda i: (i, 0))
    cq, ckv, kr = pl.pallas_call(
        functools.partial(_mla_down_kernel, rope=rope),
        grid=(t // tm,),
        in_specs=[row(d), _resident((1, d)), _mod_spec(tpb, d), _mod_spec(tpb, d),
                  _resident(w_down.shape), _resident(qg.shape), _resident(kvg.shape), tab, tab],
        out_specs=[row(MLA_Q_RANK), row(MLA_KV_RANK), row(LANES)],
        out_shape=[jax.ShapeDtypeStruct((t, MLA_Q_RANK), BF16),
                   jax.ShapeDtypeStruct((t, MLA_KV_RANK), BF16),
                   jax.ShapeDtypeStruct((t, LANES), BF16)],
        compiler_params=_cparams(1),
        name="mla_down",
    )(x, g, sh, sc, w_down, qg, kvg, cos, sin)
    qscale = (MLA_NOPE_DIM + MLA_ROPE_DIM) ** -0.5 * LOG2E
    nq, nv = wq.shape[1], 2 * wv.shape[1]
    return pl.pallas_call(
        functools.partial(_mla_up_kernel, rope=rope, qscale=qscale),
        grid=(t // tm,),
        in_specs=[row(MLA_Q_RANK), row(MLA_KV_RANK), row(LANES),
                  _resident(wq.shape), _resident(wk.shape), _resident(wv.shape), tab, tab],
        out_specs=[row(nq), row(nq), row(nv)],
        out_shape=[jax.ShapeDtypeStruct((t, nq), BF16), jax.ShapeDtypeStruct((t, nq), BF16),
                   jax.ShapeDtypeStruct((t, nv), BF16)],
        compiler_params=_cparams(1),
        name="mla_up",
    )(cq, ckv, kr, wq, wk, wv, cos, sin)


def _mla_attn_kernel(q_ref, *refs, n_seg):
    k_refs, v_refs, o_ref = refs[:n_seg], refs[n_seg:2 * n_seg], refs[2 * n_seg]
    outs = []
    for hd in range(2):
        q = q_ref[0, :, hd * LANES:(hd + 1) * LANES]
        ks = [k.at[:, :, hd * LANES:(hd + 1) * LANES] for k in k_refs]
        outs.append(_attend(q, ks, v_refs))
    lane = lax.broadcasted_iota(jnp.int32, outs[0].shape, 1)
    o_ref[0] = jnp.where(lane < MLA_V_DIM, outs[0], outs[1]).astype(BF16)


def _mla_attn(q, k_arrs, v_arrs, *, tq):
    b, sq, nq = q.shape
    pairs = nq // (2 * LANES)
    n_seg = len(k_arrs)
    in_specs = [pl.BlockSpec((1, tq, 2 * LANES), lambda bb, h, i: (bb, i, h))]
    in_specs += [pl.BlockSpec((1, a.shape[1], 2 * LANES), lambda bb, h, i: (bb, 0, h)) for a in k_arrs]
    in_specs += [pl.BlockSpec((1, a.shape[1], 2 * LANES), lambda bb, h, i: (bb, 0, h)) for a in v_arrs]
    return pl.pallas_call(
        functools.partial(_mla_attn_kernel, n_seg=n_seg),
        grid=(b, pairs, sq // tq),
        in_specs=in_specs,
        out_specs=pl.BlockSpec((1, tq, LANES), lambda bb, h, i: (bb, i, h)),
        out_shape=jax.ShapeDtypeStruct((b, sq, pairs * LANES), BF16),
        compiler_params=_cparams(3),
        name="mla_attn",
    )(q, *k_arrs, *v_arrs)


def _mla_weights(w_down, w_uq, w_ukv):
    d = w_down.shape[0]
    hq = MLA_NOPE_DIM + MLA_ROPE_DIM
    pad_r = LANES - hq
    lo = MLA_Q_RANK + MLA_KV_RANK
    zeros = lambda n: jnp.zeros((d, n), w_down.dtype)
    wd = jnp.concatenate([w_down[:, :lo], zeros(MLA_NOPE_DIM), w_down[:, lo:], zeros(pad_r)], axis=1)
    wq = w_uq.reshape(MLA_Q_RANK, MLA_HEADS, hq)
    wq = jnp.pad(wq, ((0, 0), (0, 0), (0, pad_r))).reshape(MLA_Q_RANK, MLA_HEADS * LANES)
    wkv = w_ukv.reshape(MLA_KV_RANK, MLA_HEADS, MLA_NOPE_DIM + MLA_V_DIM)
    wk = jnp.pad(wkv[:, :, :MLA_NOPE_DIM], ((0, 0), (0, 0), (0, LANES - MLA_NOPE_DIM)))
    wk = wk.reshape(MLA_KV_RANK, MLA_HEADS * LANES)
    wv = wkv[:, :, MLA_NOPE_DIM:].reshape(MLA_KV_RANK, MLA_HEADS * MLA_V_DIM)
    return wd.astype(BF16), wq.astype(BF16), wk.astype(BF16), wv.astype(BF16)


def kernel(x, c, ctx, c_ctx, mod_w, mod_b, norm_mix_g, norm_mlp_g, final_g, mlp_w1, mlp_w2, fourier_w, fourier_b, diff_w_qkv, diff_lambda_q1, diff_lambda_k1, diff_lambda_q2, diff_lambda_k2, diff_subln_g, diff_w_o, mla_w_down, mla_q_norm_g, mla_kv_norm_g, mla_w_uq, mla_w_ukv, mla_w_o):
    batch, seq, d = x.shape
    ctx_len = ctx.shape[1]
    depth = mod_w.shape[0]
    rows = seq // GRID_W
    assert seq == FFT_RADIX * FFT_RADIX and d == FOURIER_GROUPS * 256 and ctx_len == 256
    tm = 512

    cc_rows = -(-(batch + 1) // 8) * 8
    cc = jnp.zeros((cc_rows, d), F32).at[:batch].set(c).at[batch].set(c_ctx)
    mods = _modulation(cc, mod_w, mod_b).reshape(depth, cc_rows, 6, d)
    mats = _dft_mats()
    zero_bias = jnp.zeros((1, d), F32)
    final_g2 = final_g.reshape(1, d)

    lane_diff = lambda l: l % DIFF_HEAD_DIM
    cos_d, sin_d = _rope_tables(rows, DIFF_HEAD_DIM, lane_diff)
    lane_mla = lambda l: l - MLA_NOPE_DIM if MLA_NOPE_DIM <= l < MLA_NOPE_DIM + MLA_ROPE_DIM else -1
    cos_m, sin_m = _rope_tables(rows, MLA_ROPE_DIM, lane_mla)

    x_lat = x.reshape(batch * seq, d)
    x_ctx = ctx.reshape(batch * ctx_len, d)
    for i in range(depth):
        kind, j = i % N_MIXERS, i // N_MIXERS
        update_ctx = i < depth - 1
        ctx_feeds_mixer = update_ctx or kind != 0
        lat = [mods[i, :batch, k].reshape(batch, 1, d) for k in range(6)]
        cm = [mods[i, batch:batch + 1, k].reshape(1, 1, d) for k in range(6)]
        g_mix = norm_mix_g[i].reshape(1, d)
        g_mlp = norm_mlp_g[i].reshape(1, d)
        a_ctx = None
        if kind == 0:
            w_o = fourier_w[j].astype(BF16)
            b_o = fourier_b[j].reshape(1, d)
            a_lat = _fourier_front(x_lat, g_mix, lat[0], lat[1], mats, batch=batch)
            if update_ctx:
                a_ctx = _fourier_ctx(x_ctx, g_mix, cm[0], cm[1], mats, ctx_len=ctx_len)
        elif kind == 1:
            heads = d // DIFF_V_DIM
            w_qkv = diff_w_qkv[j].astype(BF16)
            w_o = diff_w_o[j].astype(BF16)
            b_o = zero_bias
            lam_init = 0.8 - 0.6 * math.exp(-0.3 * i)
            lams = [v[j].reshape(1, DIFF_HEAD_DIM) for v in
                    (diff_lambda_q1, diff_lambda_k1, diff_lambda_q2, diff_lambda_k2)]
            sg = diff_subln_g[j].reshape(1, DIFF_V_DIM)
            qkv_l = _diff_qkv(x_lat, g_mix, lat[0], lat[1], w_qkv, cos_d, sin_d, rope=True, tm=tm)
            qkv_l = qkv_l.reshape(batch, seq, 4 * d)
            qkv_c = _diff_qkv(x_ctx, g_mix, cm[0], cm[1], w_qkv, cos_d, sin_d, rope=False, tm=ctx_len)
            qkv_c = qkv_c.reshape(batch, ctx_len, 4 * d)
            o_l = _pipe_attn(lams + [sg], qkv_l, [qkv_c, qkv_l], [qkv_c, qkv_l], mode="diff",
                             units=heads, q_width=LANES, k_block=heads, tq=512, lam_init=lam_init)
            a_lat = o_l.reshape(batch * seq, d)
            if update_ctx:
                o_c = _diff_attn(lams, sg, qkv_c, [qkv_c], heads=heads, tq=ctx_len, lam_init=lam_init)
                a_ctx = o_c.reshape(batch * ctx_len, d)
        else:
            wd, wq, wk, wv = _mla_weights(mla_w_down[j], mla_w_uq[j], mla_w_ukv[j])
            w_o = mla_w_o[j].astype(BF16)
            b_o = zero_bias
            qg = mla_q_norm_g[j].reshape(1, MLA_Q_RANK)
            kvg = mla_kv_norm_g[j].reshape(1, MLA_KV_RANK)
            q_l, k_l, v_l = _mla_project(x_lat, g_mix, lat[0], lat[1], wd, qg, kvg, wq, wk, wv,
                                         cos_m, sin_m, rope=True, tm=tm)
            q_c, k_c, v_c = _mla_project(x_ctx, g_mix, cm[0], cm[1], wd, qg, kvg, wq, wk, wv,
                                         cos_m, sin_m, rope=False, tm=ctx_len)
            r3 = lambda a, s: a.reshape(batch, s, a.shape[1])
            o_l = _pipe_attn([], r3(q_l, seq), [r3(k_c, ctx_len), r3(k_l, seq)],
                             [r3(v_c, ctx_len), r3(v_l, seq)], mode="mla", units=MLA_HEADS // 2,
                             q_width=2 * LANES, k_block=0, tq=512)
            a_lat = o_l.reshape(batch * seq, d)
            if update_ctx:
                o_c = _mla_attn(r3(q_c, ctx_len), [r3(k_c, ctx_len)], [r3(v_c, ctx_len)], tq=ctx_len)
                a_ctx = o_c.reshape(batch * ctx_len, d)
        w1 = mlp_w1[i].astype(BF16)
        w2 = mlp_w2[i].astype(BF16)
        last = i == depth - 1
        if kind == 0:
            x_lat = _fourier_tail(a_lat, mats, w_o, b_o, lat[2], x_lat, g_mlp, lat[3], lat[4], lat[5],
                                  w1, w2, final_g2, batch=batch, final_norm=last)
        else:
            x_lat = _block_tail(a_lat, w_o, b_o, lat[2], x_lat, g_mlp, lat[3], lat[4], lat[5], w1, w2,
                                final_g2, final_norm=last, tm=tm)
        if update_ctx:
            x_ctx = _block_tail(a_ctx, w_o, b_o, cm[2], x_ctx, g_mlp, cm[3], cm[4], cm[5], w1, w2,
                                final_g2, final_norm=False, tm=ctx_len)
    return x_lat.reshape(batch, seq, d)
```

```python
import functools
import math

import numpy as np
import jax
import jax.numpy as jnp
from jax import lax
from jax.experimental import pallas as pl
from jax.experimental.pallas import tpu as pltpu

F32 = jnp.float32
BF16 = jnp.bfloat16

GRID_W = 64
RMS_EPS = 1e-6
ROPE_BASE = 10000.0
FOURIER_GROUPS = 4
DIFF_HEAD_DIM = 64
DIFF_V_DIM = 128
MLA_HEADS = 16
MLA_NOPE_DIM = 64
MLA_ROPE_DIM = 32
MLA_V_DIM = 64
MLA_Q_RANK = 256
MLA_KV_RANK = 128
N_MIXERS = 3

LANES = 128
FFT_RADIX = 64
VMEM_LIMIT = 56 * 1024 * 1024
LOG2E = 1.4426950408889634


def _cparams(n_axes):
    return pltpu.CompilerParams(
        dimension_semantics=("arbitrary",) * n_axes, vmem_limit_bytes=VMEM_LIMIT)


def _resident(shape):
    nd = len(shape)
    return pl.BlockSpec(shape, lambda *_: (0,) * nd, pipeline_mode=pl.Buffered(1))


def _norm_mod(x, g, sh, sc):
    r = lax.rsqrt(jnp.mean(x * x, axis=-1, keepdims=True) + RMS_EPS)
    return ((x * r) * g) * (1.0 + sc) + sh


def _rms(x, g):
    r = lax.rsqrt(jnp.mean(x * x, axis=-1, keepdims=True) + RMS_EPS)
    return (x * r) * g


def _rope_tile(x, cos, sin):
    return x * cos + pltpu.roll(x, LANES // 2, 1) * sin


def _diff_lane_dim(lane):
    pair, sub, axis, freq = lane // 64, (lane % 64) // 32, (lane % 32) // 16, lane % 16
    return sub, axis * 32 + pair * 16 + freq


def _mla_lane_dim(lane):
    half, off = lane // 64, lane % 64
    if off < 16:
        return "rope", (off // 8) * 16 + half * 8 + off % 8
    if half == 0:
        return "nope", off - 16
    return ("nope", 48 + off - 16) if off < 32 else ("pad", 0)


def _mod_kernel(c_ref, w_ref, b_ref, o_ref):
    c = c_ref[...]
    s = (c / (1.0 + jnp.exp(-c))).astype(BF16)
    o_ref[0] = jnp.dot(s, w_ref[0].astype(BF16), preferred_element_type=F32) + b_ref[0]


def _modulation(cc, mod_w, mod_b):
    depth, d, n = mod_w.shape
    rows = cc.shape[0]
    tn = n // 4
    return pl.pallas_call(
        _mod_kernel,
        grid=(depth, n // tn),
        in_specs=[
            pl.BlockSpec((rows, d), lambda i, j: (0, 0)),
            pl.BlockSpec((1, d, tn), lambda i, j: (i, 0, j)),
            pl.BlockSpec((1, 1, tn), lambda i, j: (i, 0, j)),
        ],
        out_specs=pl.BlockSpec((1, rows, tn), lambda i, j: (i, 0, j)),
        out_shape=jax.ShapeDtypeStruct((depth, rows, n), F32),
        compiler_params=_cparams(2),
        name="modulation",
    )(cc, mod_w, mod_b.reshape(depth, 1, n))


def _mod_spec(tiles_per_batch, d):
    return pl.BlockSpec((1, 1, d), lambda i: (i // tiles_per_batch, 0, 0))


FF_CHUNK = 1024


def _tail_math(a, x, wo_ref, bo_ref, ga_ref, g_ref, sh_ref, sc_ref, gm_ref, w1_ref, w2_ref, fg_ref,
               final_norm):
    y = jnp.dot(a.astype(BF16), wo_ref[...], preferred_element_type=F32) + bo_ref[...]
    x = x + ga_ref[0] * y
    h = _norm_mod(x, g_ref[...], sh_ref[0], sc_ref[0]).astype(BF16)
    acc = jnp.zeros(x.shape, F32)
    for c in range(w1_ref.shape[1] // FF_CHUNK):
        u = jnp.dot(h, w1_ref[:, c * FF_CHUNK:(c + 1) * FF_CHUNK], preferred_element_type=F32)
        u = jnp.square(jnp.maximum(u, 0.0)).astype(BF16)
        acc = acc + jnp.dot(u, w2_ref[c * FF_CHUNK:(c + 1) * FF_CHUNK, :],
                            preferred_element_type=F32)
    out = x + gm_ref[0] * acc
    if final_norm:
        out = _rms(out, fg_ref[...])
    return out


def _block_tail_kernel(a_ref, x_ref, *refs, final_norm):
    o_ref = refs[-1]
    o_ref[...] = _tail_math(a_ref[...], x_ref[...], *refs[:-1], final_norm=final_norm)


def _block_tail(a, w_o, b_o, gate_a, x, g, sh, sc, gate_m, w1, w2, final_g, *, final_norm, tm):
    t, d = x.shape
    nb = sh.shape[0]
    tpb = t // nb // tm
    row = lambda n: pl.BlockSpec((tm, n), lambda i: (i, 0))
    return pl.pallas_call(
        functools.partial(_block_tail_kernel, final_norm=final_norm),
        grid=(t // tm,),
        in_specs=[
            row(a.shape[1]), row(d),
            _resident(w_o.shape), _resident((1, d)), _mod_spec(tpb, d),
            _resident((1, d)), _mod_spec(tpb, d), _mod_spec(tpb, d), _mod_spec(tpb, d),
            _resident(w1.shape), _resident(w2.shape), _resident((1, d)),
        ],
        out_specs=row(d),
        out_shape=jax.ShapeDtypeStruct((t, d), F32),
        compiler_params=_cparams(1),
        name="block_tail",
    )(a, x, w_o, b_o, gate_a, g, sh, sc, gate_m, w1, w2, final_g)


def _dft_mats():
    cg = 256
    j = np.arange(cg)
    th = 2.0 * np.pi * np.outer(j, j) / cg
    wc = np.concatenate([np.cos(th), -np.sin(th)], axis=1) / 16.0
    ctx_seq = np.concatenate([np.cos(th), np.sin(th)], axis=1) / 16.0
    r = np.arange(FFT_RADIX)
    ph = 2.0 * np.pi * np.outer(r, r) / FFT_RADIX
    dr, di = np.cos(ph) / 8.0, -np.sin(ph) / 8.0
    m1 = np.block([[dr, -di], [di, dr]])
    m2 = np.concatenate([dr, -di], axis=1)
    eye = np.eye(FFT_GROUP)
    m1 = np.einsum("pq,jk->jpqk", m1, eye).reshape(m1.shape[0] * FFT_GROUP, m1.shape[1] * FFT_GROUP)
    m2 = np.einsum("pq,jk->pjqk", m2, eye).reshape(m2.shape[0] * FFT_GROUP, m2.shape[1] * FFT_GROUP)
    tw = 2.0 * np.pi * np.outer(r, r) / (FFT_RADIX * FFT_RADIX)
    tr = np.repeat(np.cos(tw)[:, :, None], LANES, axis=2)
    ti = np.repeat(-np.sin(tw)[:, :, None], LANES, axis=2)
    as_bf = lambda a: jnp.asarray(a, F32).astype(BF16)
    return dict(wc=as_bf(wc), ctx_seq=as_bf(ctx_seq), m1=as_bf(m1), m2=as_bf(m2),
                tr=jnp.asarray(tr, F32), ti=jnp.asarray(ti, F32))


def _chan_dft(h, wc_ref):
    cg = wc_ref.shape[0]
    res, ims = [], []
    for grp in range(h.shape[1] // cg):
        z = jnp.dot(h[:, grp * cg:(grp + 1) * cg], wc_ref[...], preferred_element_type=F32)
        res.append(z[:, :cg])
        ims.append(z[:, cg:])
    return jnp.concatenate(res, axis=1), jnp.concatenate(ims, axis=1)


def _pack_complex(re, im):
    bits = lambda v: lax.bitcast_convert_type(v.astype(BF16).astype(F32), jnp.uint32)
    return (bits(re) >> 16) | bits(im)


def _unpack_complex(w):
    re = lax.bitcast_convert_type(w << 16, F32)
    im = lax.bitcast_convert_type(w & jnp.uint32(0xFFFF0000), F32)
    return jnp.concatenate([re, im], axis=0).astype(BF16)


FFT_GROUP = 8


def _digit_spec(d):
    return pl.BlockSpec((None, FFT_RADIX, FFT_GROUP, d), lambda b, j: (b, 0, j, 0))


def _fourier_front_kernel(x_ref, g_ref, sh_ref, sc_ref, wc_ref, tr_ref, ti_ref, w1_ref, a_ref):
    r, kg = FFT_RADIX, FFT_GROUP
    d = x_ref.shape[-1]
    reps = d // LANES
    x = x_ref[...].reshape(r * kg, d)
    h = _norm_mod(x, g_ref[...], sh_ref[0], sc_ref[0]).astype(BF16)
    zr, zi = _chan_dft(h, wc_ref)
    z = jnp.concatenate([zr, zi], axis=0).astype(BF16)
    a = jnp.dot(w1_ref[...], z, preferred_element_type=F32)
    for j in range(kg):
        ar = a[2 * r * j:2 * r * j + r]
        ai = a[2 * r * j + r:2 * r * (j + 1)]
        tr = jnp.tile(tr_ref[j], (1, reps))
        ti = jnp.tile(ti_ref[j], (1, reps))
        a_ref[0, j] = _pack_complex(ar * tr - ai * ti, ar * ti + ai * tr)


def _fourier_front(x, g, sh, sc, mats, *, batch):
    t, d = x.shape
    r, kg = FFT_RADIX, FFT_GROUP
    const = lambda a: pl.BlockSpec(a.shape, lambda b, j: (0,) * a.ndim)
    mod = pl.BlockSpec((1, 1, d), lambda b, j: (b, 0, 0))
    tw = pl.BlockSpec((kg, r, LANES), lambda b, j: (j, 0, 0))
    return pl.pallas_call(
        _fourier_front_kernel,
        grid=(batch, r // kg),
        in_specs=[_digit_spec(d), const(g), mod, mod, const(mats["wc"]), tw, tw, const(mats["m1"])],
        out_specs=pl.BlockSpec((1, kg, r, d), lambda b, j: (b, j, 0, 0)),
        out_shape=jax.ShapeDtypeStruct((batch, r, r, d), jnp.uint32),
        compiler_params=_cparams(2),
        name="fourier_front",
    )(x.reshape(batch, r, r, d), g, sh, sc, mats["wc"], mats["tr"], mats["ti"], mats["m1"])


def _fourier_tail_kernel(a_ref, m2_ref, x_ref, *refs, final_norm):
    r, kg = FFT_RADIX, FFT_GROUP
    d = a_ref.shape[-1]
    o_ref = refs[-1]
    z = _unpack_complex(a_ref[...].reshape(r * kg, d))
    f = jnp.dot(m2_ref[...], z, preferred_element_type=F32)
    out = _tail_math(f, x_ref[...].reshape(r * kg, d), *refs[:-1], final_norm=final_norm)
    o_ref[...] = out.reshape(r, kg, d)


def _fourier_tail(a, mats, w_o, b_o, gate_a, x, g, sh, sc, gate_m, w1, w2, final_g, *, batch,
                  final_norm):
    t, d = x.shape
    r, kg = FFT_RADIX, FFT_GROUP
    const = lambda v: pl.BlockSpec(v.shape, lambda b, j: (0,) * v.ndim, pipeline_mode=pl.Buffered(1))
    mod = pl.BlockSpec((1, 1, d), lambda b, j: (b, 0, 0))
    out = pl.pallas_call(
        functools.partial(_fourier_tail_kernel, final_norm=final_norm),
        grid=(batch, r // kg),
        in_specs=[_digit_spec(d), const(mats["m2"]), _digit_spec(d),
                  const(w_o), const(b_o), mod, const(g), mod, mod, mod,
                  const(w1), const(w2), const(final_g)],
        out_specs=_digit_spec(d),
        out_shape=jax.ShapeDtypeStruct((batch, r, r, d), F32),
        compiler_params=_cparams(2),
        name="fourier_tail",
    )(a, mats["m2"], x.reshape(batch, r, r, d), w_o, b_o, gate_a, g, sh, sc, gate_m, w1, w2, final_g)
    return out.reshape(t, d)


def _fourier_ctx_kernel(x_ref, g_ref, sh_ref, sc_ref, wc_ref, seq_ref, f_ref):
    h = _norm_mod(x_ref[...], g_ref[...], sh_ref[0], sc_ref[0]).astype(BF16)
    zr, zi = _chan_dft(h, wc_ref)
    z = jnp.concatenate([zr, zi], axis=0).astype(BF16)
    f_ref[...] = jnp.dot(seq_ref[...], z, preferred_element_type=F32)


def _fourier_ctx(x, g, sh, sc, mats, *, ctx_len):
    t, d = x.shape
    n_tiles = t // ctx_len
    return pl.pallas_call(
        _fourier_ctx_kernel,
        grid=(n_tiles,),
        in_specs=[
            pl.BlockSpec((ctx_len, d), lambda i: (i, 0)),
            _resident((1, d)), _mod_spec(n_tiles, d), _mod_spec(n_tiles, d),
            _resident(mats["wc"].shape), _resident(mats["ctx_seq"].shape),
        ],
        out_specs=pl.BlockSpec((ctx_len, d), lambda i: (i, 0)),
        out_shape=jax.ShapeDtypeStruct((t, d), F32),
        compiler_params=_cparams(1),
        name="fourier_ctx",
    )(x, g, sh, sc, mats["wc"], mats["ctx_seq"])


def _rope_tables(rows, dim, lane_of_dim):
    n_freq = dim // 4
    inv_freq = ROPE_BASE ** (-jnp.arange(n_freq, dtype=F32) / n_freq)
    t = jnp.arange(rows * GRID_W)
    row = (t // GRID_W).astype(F32)
    col = (t % GRID_W).astype(F32)
    ang = jnp.stack([row[:, None] * inv_freq, col[:, None] * inv_freq], axis=1)
    cos, sin = jnp.cos(ang), jnp.sin(ang)
    dims = np.array([lane_of_dim(l) for l in range(LANES)])
    live = dims >= 0
    dd = np.where(live, dims, 0)
    axis, pair, freq = dd // (2 * n_freq), (dd % (2 * n_freq)) // n_freq, dd % n_freq
    sign = np.where(pair == 0, -1.0, 1.0).astype(np.float32)
    cos_t = jnp.where(live[None, :], cos[:, axis, freq], 1.0)
    sin_t = jnp.where(live[None, :], sin[:, axis, freq] * sign[None, :], 0.0)
    return cos_t.astype(F32), sin_t.astype(F32)


def _split_sub_heads(q):
    lane = lax.broadcasted_iota(jnp.int32, q.shape, 1)
    first = (lane & (DIFF_HEAD_DIM // 2)) == 0
    zero = jnp.zeros_like(q)
    return [jnp.where(first, q, zero), jnp.where(first, zero, q)]


def _scores(q, k_refs):
    nt = (((1,), (1,)), ((), ()))
    return [lax.dot_general(q, k[0], nt, preferred_element_type=F32) for k in k_refs]


def _attend(q, k_refs, vaug_refs):
    parts = _scores(q, k_refs)
    m = functools.reduce(jnp.maximum, [jnp.max(p, axis=-1, keepdims=True) for p in parts])
    acc = None
    for p, v in zip(parts, vaug_refs):
        e = jnp.exp2(p - m).astype(BF16)
        part = jnp.dot(e, v[0], preferred_element_type=F32)
        acc = part if acc is None else acc + part
    return acc[:, :LANES] / acc[:, LANES:]


def _pipe_attn_kernel(*refs, mode, n_seg, lam_init):
    n_par = 5 if mode == "diff" else 0
    par = refs[:n_par]
    q_ref = refs[n_par]
    k_refs = refs[n_par + 1:n_par + 1 + n_seg]
    v_refs = refs[n_par + 1 + n_seg:n_par + 1 + 2 * n_seg]
    o_ref, s_a, s_b, m_a, m_b = refs[n_par + 1 + 2 * n_seg:]
    t = pl.program_id(0)
    nt = (((1,), (1,)), ((), ()))

    def scores(s_w, m_w):
        q = q_ref[0]
        if mode == "diff":
            qs = _split_sub_heads(q)
        else:
            qs = [q[:, :LANES], q[:, LANES:]]
        for c in range(2):
            m, off = None, 0
            for k in k_refs:
                kk = k[0] if mode == "diff" else k[0, :, c * LANES:(c + 1) * LANES]
                p = lax.dot_general(qs[c], kk, nt, preferred_element_type=F32)
                s_w[c, :, off:off + p.shape[1]] = p
                off += p.shape[1]
                pm = jnp.max(p, axis=-1, keepdims=True)
                m = pm if m is None else jnp.maximum(m, pm)
            m_w[c] = jnp.broadcast_to(m, m_w.shape[1:])

    def values(s_r, m_r):
        outs = []
        for c in range(2):
            m = m_r[c][:, :1]
            acc, off = None, 0
            for v in v_refs:
                w = v.shape[1]
                e = jnp.exp2(s_r[c, :, off:off + w] - m).astype(BF16)
                part = jnp.dot(e, v[0], preferred_element_type=F32)
                acc = part if acc is None else acc + part
                off += w
            outs.append(acc[:, :LANES] / acc[:, LANES:])
        if mode == "diff":
            lq1, lk1, lq2, lk2, sg = par
            lam = (jnp.exp(jnp.sum(lq1[...] * lk1[...], axis=-1, keepdims=True))
                   - jnp.exp(jnp.sum(lq2[...] * lk2[...], axis=-1, keepdims=True)) + lam_init)
            o = outs[0] - lam * outs[1]
            o_ref[0] = (_rms(o, sg[...]) * (1.0 - lam_init)).astype(BF16)
        else:
            lane = lax.broadcasted_iota(jnp.int32, outs[0].shape, 1)
            o_ref[0] = jnp.where(lane < MLA_V_DIM, outs[0], outs[1]).astype(BF16)

    @pl.when(t == 0)
    def _():
        s_b[...] = jnp.zeros(s_b.shape, F32)
        m_b[...] = jnp.zeros(m_b.shape, F32)

    @pl.when(t % 2 == 0)
    def _():
        scores(s_a, m_a)
        values(s_b, m_b)

    @pl.when(t % 2 == 1)
    def _():
        scores(s_b, m_b)
        values(s_a, m_a)


def _pipe_attn(params, q_arr, k_arrs, v_arrs, *, mode, units, q_width, k_block, tq, lam_init=0.0):
    b, sq, _ = q_arr.shape
    nq = sq // tq
    n_items = b * units * nq
    n_seg = len(k_arrs)
    sk = sum(a.shape[1] for a in k_arrs)

    def item(t, lag):
        tt = jnp.clip(t - lag, 0, n_items - 1)
        return tt // (units * nq), (tt // nq) % units, tt % nq

    def q_map(t):
        bb, u, i = item(t, 0)
        return bb, i, u

    def k_map(t):
        bb, u, _ = item(t, 0)
        return bb, 0, k_block + u

    def v_map(t):
        bb, u, _ = item(t, 1)
        return bb, 0, k_block + u

    def o_map(t):
        bb, u, i = item(t, 1)
        return bb, i, u

    in_specs = [pl.BlockSpec(p.shape, lambda t: (0, 0)) for p in params]
    in_specs += [pl.BlockSpec((1, tq, q_width), q_map)]
    in_specs += [pl.BlockSpec((1, a.shape[1], q_width), k_map) for a in k_arrs]
    in_specs += [pl.BlockSpec((1, a.shape[1], 2 * LANES), v_map) for a in v_arrs]
    return pl.pallas_call(
        functools.partial(_pipe_attn_kernel, mode=mode, n_seg=n_seg, lam_init=lam_init),
        grid=(n_items + 1,),
        in_specs=in_specs,
        out_specs=pl.BlockSpec((1, tq, LANES), o_map),
        out_shape=jax.ShapeDtypeStruct((b, sq, units * LANES), BF16),
        scratch_shapes=[pltpu.VMEM((2, tq, sk), F32), pltpu.VMEM((2, tq, sk), F32),
                        pltpu.VMEM((2, tq, LANES), F32), pltpu.VMEM((2, tq, LANES), F32)],
        compiler_params=_cparams(1),
        name=mode + "_pipe_attn",
    )(*params, q_arr, *k_arrs, *v_arrs)


def _diff_qkv_kernel(x_ref, g_ref, sh_ref, sc_ref, w_ref, cos_ref, sin_ref, o_ref, *, rope, qscale):
    h = _norm_mod(x_ref[...], g_ref[...], sh_ref[0], sc_ref[0]).astype(BF16)
    d = x_ref.shape[1]
    acc = jnp.dot(h, w_ref[...], preferred_element_type=F32)
    heads = d // LANES
    ones = jnp.ones((x_ref.shape[0], LANES), BF16)
    for tile in range(2 * heads):
        a = acc[:, tile * LANES:(tile + 1) * LANES]
        if rope:
            a = _rope_tile(a, cos_ref[...], sin_ref[...])
        if tile < heads:
            a = a * qscale
        o_ref[:, tile * LANES:(tile + 1) * LANES] = a.astype(BF16)
    for hd in range(heads):
        src = (2 * heads + hd) * LANES
        dst = (2 * heads + 2 * hd) * LANES
        o_ref[:, dst:dst + LANES] = acc[:, src:src + LANES].astype(BF16)
        o_ref[:, dst + LANES:dst + 2 * LANES] = ones


def _diff_qkv(x, g, sh, sc, w, cos, sin, *, rope, tm):
    t, d = x.shape
    nb = sh.shape[0]
    tpb = t // nb // tm
    spb = t // nb // tm
    n = w.shape[1] + d
    qscale = DIFF_HEAD_DIM ** -0.5 * LOG2E
    tab = pl.BlockSpec((tm, LANES), (lambda i: (i % spb, 0)) if rope else (lambda i: (0, 0)))
    return pl.pallas_call(
        functools.partial(_diff_qkv_kernel, rope=rope, qscale=qscale),
        grid=(t // tm,),
        in_specs=[
            pl.BlockSpec((tm, d), lambda i: (i, 0)),
            _resident((1, d)), _mod_spec(tpb, d), _mod_spec(tpb, d),
            _resident(w.shape), tab, tab,
        ],
        out_specs=pl.BlockSpec((tm, n), lambda i: (i, 0)),
        out_shape=jax.ShapeDtypeStruct((t, n), BF16),
        compiler_params=_cparams(1),
        name="diff_qkv",
    )(x, g, sh, sc, w, cos, sin)


def _diff_attn_kernel(lq1_ref, lk1_ref, lq2_ref, lk2_ref, sg_ref, q_ref, *refs, n_seg, lam_init):
    k_refs, v_refs, o_ref = refs[:n_seg], refs[n_seg:2 * n_seg], refs[2 * n_seg]
    lam = (jnp.exp(jnp.sum(lq1_ref[...] * lk1_ref[...], axis=-1, keepdims=True))
           - jnp.exp(jnp.sum(lq2_ref[...] * lk2_ref[...], axis=-1, keepdims=True)) + lam_init)
    q1, q2 = _split_sub_heads(q_ref[0])
    o = _attend(q1, k_refs, v_refs) - lam * _attend(q2, k_refs, v_refs)
    o_ref[0] = (_rms(o, sg_ref[...]) * (1.0 - lam_init)).astype(BF16)


def _diff_attn(lams, subln_g, q_arr, kv_arrs, *, heads, tq, lam_init):
    b, sq, _ = q_arr.shape
    n_seg = len(kv_arrs)
    lam_specs = [pl.BlockSpec(l.shape, lambda bb, h, i: (0, 0)) for l in lams]
    in_specs = lam_specs + [
        pl.BlockSpec((1, DIFF_V_DIM), lambda bb, h, i: (0, 0)),
        pl.BlockSpec((1, tq, LANES), lambda bb, h, i: (bb, i, h)),
    ]
    in_specs += [pl.BlockSpec((1, a.shape[1], LANES), lambda bb, h, i: (bb, 0, heads + h))
                 for a in kv_arrs]
    in_specs += [pl.BlockSpec((1, a.shape[1], 2 * LANES), lambda bb, h, i: (bb, 0, heads + h))
                 for a in kv_arrs]
    return pl.pallas_call(
        functools.partial(_diff_attn_kernel, n_seg=n_seg, lam_init=lam_init),
        grid=(b, heads, sq // tq),
        in_specs=in_specs,
        out_specs=pl.BlockSpec((1, tq, LANES), lambda bb, h, i: (bb, i, h)),
        out_shape=jax.ShapeDtypeStruct((b, sq, heads * DIFF_V_DIM), BF16),
        compiler_params=_cparams(3),
        name="diff_attn",
    )(*lams, subln_g, q_arr, *kv_arrs, *kv_arrs)


def _mla_down_kernel(x_ref, g_ref, sh_ref, sc_ref, w_ref, qg_ref, kvg_ref, cos_ref, sin_ref,
                     cq_ref, ckv_ref, kr_ref, *, rope):
    h = _norm_mod(x_ref[...], g_ref[...], sh_ref[0], sc_ref[0]).astype(BF16)
    acc = jnp.dot(h, w_ref[...], preferred_element_type=F32)
    cq_ref[...] = _rms(acc[:, :MLA_Q_RANK], qg_ref[...]).astype(BF16)
    ckv_ref[...] = _rms(acc[:, MLA_Q_RANK:MLA_Q_RANK + MLA_KV_RANK], kvg_ref[...]).astype(BF16)
    kr = acc[:, MLA_Q_RANK + MLA_KV_RANK:]
    if rope:
        kr = _rope_tile(kr, cos_ref[...], sin_ref[...])
    kr_ref[...] = kr.astype(BF16)


def _mla_up_kernel(cq_ref, ckv_ref, kr_ref, wq_ref, wk_ref, wv_ref, cos_ref, sin_ref,
                   q_ref, k_ref, v_ref, *, rope, qscale):
    q = jnp.dot(cq_ref[...], wq_ref[...], preferred_element_type=F32)
    for hd in range(MLA_HEADS):
        a = q[:, hd * LANES:(hd + 1) * LANES]
        if rope:
            a = _rope_tile(a, cos_ref[...], sin_ref[...])
        q_ref[:, hd * LANES:(hd + 1) * LANES] = (a * qscale).astype(BF16)
    k = jnp.dot(ckv_ref[...], wk_ref[...], preferred_element_type=F32)
    kr = kr_ref[...].astype(F32)
    for hd in range(MLA_HEADS):
        k_ref[:, hd * LANES:(hd + 1) * LANES] = (k[:, hd * LANES:(hd + 1) * LANES] + kr).astype(BF16)
    v = jnp.dot(ckv_ref[...], wv_ref[...], preferred_element_type=F32)
    ones = jnp.ones((v.shape[0], LANES), BF16)
    for pair in range(MLA_HEADS // 2):
        v_ref[:, 2 * pair * LANES:(2 * pair + 1) * LANES] = v[:, pair * LANES:(pair + 1) * LANES].astype(BF16)
        v_ref[:, (2 * pair + 1) * LANES:(2 * pair + 2) * LANES] = ones


def _mla_project(x, g, sh, sc, w_down, qg, kvg, wq, wk, wv, cos, sin, *, rope, tm):
    t, d = x.shape
    nb = sh.shape[0]
    tpb = t // nb // tm
    tab = pl.BlockSpec((tm, LANES), (lambda i: (i % tpb, 0)) if rope else (lambda i: (0, 0)))
    row = lambda n: pl.BlockSpec((tm, n), lambda i: (i, 0))
    cq, ckv, kr = pl.pallas_call(
        functools.partial(_mla_down_kernel, rope=rope),
        grid=(t // tm,),
        in_specs=[row(d), _resident((1, d)), _mod_spec(tpb, d), _mod_spec(tpb, d),
                  _resident(w_down.shape), _resident(qg.shape), _resident(kvg.shape), tab, tab],
        out_specs=[row(MLA_Q_RANK), row(MLA_KV_RANK), row(LANES)],
        out_shape=[jax.ShapeDtypeStruct((t, MLA_Q_RANK), BF16),
                   jax.ShapeDtypeStruct((t, MLA_KV_RANK), BF16),
                   jax.ShapeDtypeStruct((t, LANES), BF16)],
        compiler_params=_cparams(1),
        name="mla_down",
    )(x, g, sh, sc, w_down, qg, kvg, cos, sin)
    qscale = (MLA_NOPE_DIM + MLA_ROPE_DIM) ** -0.5 * LOG2E
    nq, nv = wq.shape[1], 2 * wv.shape[1]
    return pl.pallas_call(
        functools.partial(_mla_up_kernel, rope=rope, qscale=qscale),
        grid=(t // tm,),
        in_specs=[row(MLA_Q_RANK), row(MLA_KV_RANK), row(LANES),
                  _resident(wq.shape), _resident(wk.shape), _resident(wv.shape), tab, tab],
        out_specs=[row(nq), row(nq), row(nv)],
        out_shape=[jax.ShapeDtypeStruct((t, nq), BF16), jax.ShapeDtypeStruct((t, nq), BF16),
                   jax.ShapeDtypeStruct((t, nv), BF16)],
        compiler_params=_cparams(1),
        name="mla_up",
    )(cq, ckv, kr, wq, wk, wv, cos, sin)


def _mla_attn_kernel(q_ref, *refs, n_seg):
    k_refs, v_refs, o_ref = refs[:n_seg], refs[n_seg:2 * n_seg], refs[2 * n_seg]
    outs = []
    for hd in range(2):
        q = q_ref[0, :, hd * LANES:(hd + 1) * LANES]
        ks = [k.at[:, :, hd * LANES:(hd + 1) * LANES] for k in k_refs]
        outs.append(_attend(q, ks, v_refs))
    lane = lax.broadcasted_iota(jnp.int32, outs[0].shape, 1)
    o_ref[0] = jnp.where(lane < MLA_V_DIM, outs[0], outs[1]).astype(BF16)


def _mla_attn(q, k_arrs, v_arrs, *, tq):
    b, sq, nq = q.shape
    pairs = nq // (2 * LANES)
    n_seg = len(k_arrs)
    in_specs = [pl.BlockSpec((1, tq, 2 * LANES), lambda bb, h, i: (bb, i, h))]
    in_specs += [pl.BlockSpec((1, a.shape[1], 2 * LANES), lambda bb, h, i: (bb, 0, h)) for a in k_arrs]
    in_specs += [pl.BlockSpec((1, a.shape[1], 2 * LANES), lambda bb, h, i: (bb, 0, h)) for a in v_arrs]
    return pl.pallas_call(
        functools.partial(_mla_attn_kernel, n_seg=n_seg),
        grid=(b, pairs, sq // tq),
        in_specs=in_specs,
        out_specs=pl.BlockSpec((1, tq, LANES), lambda bb, h, i: (bb, i, h)),
        out_shape=jax.ShapeDtypeStruct((b, sq, pairs * LANES), BF16),
        compiler_params=_cparams(3),
        name="mla_attn",
    )(q, *k_arrs, *v_arrs)


def _take_cols(w, src):
    src = np.asarray(src)
    return jnp.where(src >= 0, jnp.take(w, np.maximum(src, 0), axis=-1), 0).astype(w.dtype)


def _mla_weights(w_down, w_uq, w_ukv):
    hq = MLA_NOPE_DIM + MLA_ROPE_DIM
    lo = MLA_Q_RANK + MLA_KV_RANK
    kinds = [_mla_lane_dim(l) for l in range(LANES)]
    q_src = np.array([{"nope": dim, "rope": MLA_NOPE_DIM + dim, "pad": -1}[k] for k, dim in kinds])
    k_src = np.array([dim if k == "nope" else -1 for k, dim in kinds])
    r_src = np.array([lo + dim if k == "rope" else -1 for k, dim in kinds])
    wd = jnp.concatenate([w_down[:, :lo], _take_cols(w_down, r_src)], axis=1)
    wq = _take_cols(w_uq.reshape(MLA_Q_RANK, MLA_HEADS, hq), q_src)
    wkv = w_ukv.reshape(MLA_KV_RANK, MLA_HEADS, MLA_NOPE_DIM + MLA_V_DIM)
    wk = _take_cols(wkv, k_src)
    wv = wkv[:, :, MLA_NOPE_DIM:].reshape(MLA_KV_RANK, MLA_HEADS * MLA_V_DIM)
    flat = lambda w: w.reshape(w.shape[0], MLA_HEADS * LANES)
    return wd.astype(BF16), flat(wq).astype(BF16), flat(wk).astype(BF16), wv.astype(BF16)


def _diff_weights(w_qkv):
    d = w_qkv.shape[0]
    src = np.array([sub * DIFF_HEAD_DIM + dim for sub, dim in map(_diff_lane_dim, range(LANES))])
    qk = w_qkv[:, :2 * d].reshape(d, 2 * d // LANES, LANES)
    qk = _take_cols(qk, src).reshape(d, 2 * d)
    return jnp.concatenate([qk, w_qkv[:, 2 * d:]], axis=1).astype(BF16)


def kernel(x, c, ctx, c_ctx, mod_w, mod_b, norm_mix_g, norm_mlp_g, final_g, mlp_w1, mlp_w2, fourier_w, fourier_b, diff_w_qkv, diff_lambda_q1, diff_lambda_k1, diff_lambda_q2, diff_lambda_k2, diff_subln_g, diff_w_o, mla_w_down, mla_q_norm_g, mla_kv_norm_g, mla_w_uq, mla_w_ukv, mla_w_o):
    batch, seq, d = x.shape
    ctx_len = ctx.shape[1]
    depth = mod_w.shape[0]
    rows = seq // GRID_W
    assert seq == FFT_RADIX * FFT_RADIX and d == FOURIER_GROUPS * 256 and ctx_len == 256
    tm = 512

    cc_rows = -(-(batch + 1) // 8) * 8
    cc = jnp.zeros((cc_rows, d), F32).at[:batch].set(c).at[batch].set(c_ctx)
    mods = _modulation(cc, mod_w, mod_b).reshape(depth, cc_rows, 6, d)
    mats = _dft_mats()
    zero_bias = jnp.zeros((1, d), F32)
    final_g2 = final_g.reshape(1, d)

    cos_d, sin_d = _rope_tables(rows, DIFF_HEAD_DIM, lambda l: _diff_lane_dim(l)[1])
    lane_mla = lambda l: _mla_lane_dim(l)[1] if _mla_lane_dim(l)[0] == "rope" else -1
    cos_m, sin_m = _rope_tables(rows, MLA_ROPE_DIM, lane_mla)

    x_lat = x.reshape(batch * seq, d)
    x_ctx = ctx.reshape(batch * ctx_len, d)
    for i in range(depth):
        kind, j = i % N_MIXERS, i // N_MIXERS
        update_ctx = i < depth - 1
        ctx_feeds_mixer = update_ctx or kind != 0
        lat = [mods[i, :batch, k].reshape(batch, 1, d) for k in range(6)]
        cm = [mods[i, batch:batch + 1, k].reshape(1, 1, d) for k in range(6)]
        g_mix = norm_mix_g[i].reshape(1, d)
        g_mlp = norm_mlp_g[i].reshape(1, d)
        a_ctx = None
        if kind == 0:
            w_o = fourier_w[j].astype(BF16)
            b_o = fourier_b[j].reshape(1, d)
            a_lat = _fourier_front(x_lat, g_mix, lat[0], lat[1], mats, batch=batch)
            if update_ctx:
                a_ctx = _fourier_ctx(x_ctx, g_mix, cm[0], cm[1], mats, ctx_len=ctx_len)
        elif kind == 1:
            heads = d // DIFF_V_DIM
            w_qkv = _diff_weights(diff_w_qkv[j])
            w_o = diff_w_o[j].astype(BF16)
            b_o = zero_bias
            lam_init = 0.8 - 0.6 * math.exp(-0.3 * i)
            lams = [v[j].reshape(1, DIFF_HEAD_DIM) for v in
                    (diff_lambda_q1, diff_lambda_k1, diff_lambda_q2, diff_lambda_k2)]
            sg = diff_subln_g[j].reshape(1, DIFF_V_DIM)
            qkv_l = _diff_qkv(x_lat, g_mix, lat[0], lat[1], w_qkv, cos_d, sin_d, rope=True, tm=tm)
            qkv_l = qkv_l.reshape(batch, seq, 4 * d)
            qkv_c = _diff_qkv(x_ctx, g_mix, cm[0], cm[1], w_qkv, cos_d, sin_d, rope=False, tm=ctx_len)
            qkv_c = qkv_c.reshape(batch, ctx_len, 4 * d)
            o_l = _pipe_attn(lams + [sg], qkv_l, [qkv_c, qkv_l], [qkv_c, qkv_l], mode="diff",
                             units=heads, q_width=LANES, k_block=heads, tq=512, lam_init=lam_init)
            a_lat = o_l.reshape(batch * seq, d)
            if update_ctx:
                o_c = _diff_attn(lams, sg, qkv_c, [qkv_c], heads=heads, tq=ctx_len, lam_init=lam_init)
                a_ctx = o_c.reshape(batch * ctx_len, d)
        else:
            wd, wq, wk, wv = _mla_weights(mla_w_down[j], mla_w_uq[j], mla_w_ukv[j])
            w_o = mla_w_o[j].astype(BF16)
            b_o = zero_bias
            qg = mla_q_norm_g[j].reshape(1, MLA_Q_RANK)
            kvg = mla_kv_norm_g[j].reshape(1, MLA_KV_RANK)
            q_l, k_l, v_l = _mla_project(x_lat, g_mix, lat[0], lat[1], wd, qg, kvg, wq, wk, wv,
                                         cos_m, sin_m, rope=True, tm=tm)
            q_c, k_c, v_c = _mla_project(x_ctx, g_mix, cm[0], cm[1], wd, qg, kvg, wq, wk, wv,
                                         cos_m, sin_m, rope=False, tm=ctx_len)
            r3 = lambda a, s: a.reshape(batch, s, a.shape[1])
            o_l = _pipe_attn([], r3(q_l, seq), [r3(k_c, ctx_len), r3(k_l, seq)],
                             [r3(v_c, ctx_len), r3(v_l, seq)], mode="mla", units=MLA_HEADS // 2,
                             q_width=2 * LANES, k_block=0, tq=512)
            a_lat = o_l.reshape(batch * seq, d)
            if update_ctx:
                o_c = _mla_attn(r3(q_c, ctx_len), [r3(k_c, ctx_len)], [r3(v_c, ctx_len)], tq=ctx_len)
                a_ctx = o_c.reshape(batch * ctx_len, d)
        w1 = mlp_w1[i].astype(BF16)
        w2 = mlp_w2[i].astype(BF16)
        last = i == depth - 1
        if kind == 0:
            x_lat = _fourier_tail(a_lat, mats, w_o, b_o, lat[2], x_lat, g_mlp, lat[3], lat[4], lat[5],
                                  w1, w2, final_g2, batch=batch, final_norm=last)
        else:
            x_lat = _block_tail(a_lat, w_o, b_o, lat[2], x_lat, g_mlp, lat[3], lat[4], lat[5], w1, w2,
                                final_g2, final_norm=last, tm=tm)
        if update_ctx:
            x_ctx = _block_tail(a_ctx, w_o, b_o, cm[2], x_ctx, g_mlp, cm[3], cm[4], cm[5], w1, w2,
                                final_g2, final_norm=False, tm=ctx_len)
    return x_lat.reshape(batch, seq, d)
```

```python
import functools
import math

import numpy as np
import jax
import jax.numpy as jnp
from jax import lax
from jax.experimental import pallas as pl
from jax.experimental.pallas import tpu as pltpu

F32 = jnp.float32
BF16 = jnp.bfloat16

GRID_W = 64
RMS_EPS = 1e-6
ROPE_BASE = 10000.0
FOURIER_GROUPS = 4
DIFF_HEAD_DIM = 64
DIFF_V_DIM = 128
MLA_HEADS = 16
MLA_NOPE_DIM = 64
MLA_ROPE_DIM = 32
MLA_V_DIM = 64
MLA_Q_RANK = 256
MLA_KV_RANK = 128
N_MIXERS = 3

LANES = 128
FFT_RADIX = 64
VMEM_LIMIT = 56 * 1024 * 1024
LOG2E = 1.4426950408889634


def _cparams(n_axes):
    return pltpu.CompilerParams(
        dimension_semantics=("arbitrary",) * n_axes, vmem_limit_bytes=VMEM_LIMIT)


def _resident(shape):
    nd = len(shape)
    return pl.BlockSpec(shape, lambda *_: (0,) * nd, pipeline_mode=pl.Buffered(1))


def _norm_mod(x, g, sh, sc):
    r = lax.rsqrt(jnp.mean(x * x, axis=-1, keepdims=True) + RMS_EPS)
    return ((x * r) * g) * (1.0 + sc) + sh


def _rms(x, g):
    r = lax.rsqrt(jnp.mean(x * x, axis=-1, keepdims=True) + RMS_EPS)
    return (x * r) * g


def _rope_tile(x, cos, sin):
    return x * cos + pltpu.roll(x, LANES // 2, 1) * sin


def _diff_lane_dim(lane):
    pair, sub, axis, freq = lane // 64, (lane % 64) // 32, (lane % 32) // 16, lane % 16
    return sub, axis * 32 + pair * 16 + freq


def _mla_lane_dim(lane):
    half, off = lane // 64, lane % 64
    if off < 16:
        return "rope", (off // 8) * 16 + half * 8 + off % 8
    if half == 0:
        return "nope", off - 16
    return ("nope", 48 + off - 16) if off < 32 else ("pad", 0)


def _mod_kernel(c_ref, w_ref, b_ref, o_ref):
    c = c_ref[...]
    s = (c / (1.0 + jnp.exp(-c))).astype(BF16)
    o_ref[0] = jnp.dot(s, w_ref[0].astype(BF16), preferred_element_type=F32) + b_ref[0]


def _modulation(cc, mod_w, mod_b):
    depth, d, n = mod_w.shape
    rows = cc.shape[0]
    tn = n // 4
    return pl.pallas_call(
        _mod_kernel,
        grid=(depth, n // tn),
        in_specs=[
            pl.BlockSpec((rows, d), lambda i, j: (0, 0)),
            pl.BlockSpec((1, d, tn), lambda i, j: (i, 0, j)),
            pl.BlockSpec((1, 1, tn), lambda i, j: (i, 0, j)),
        ],
        out_specs=pl.BlockSpec((1, rows, tn), lambda i, j: (i, 0, j)),
        out_shape=jax.ShapeDtypeStruct((depth, rows, n), F32),
        compiler_params=_cparams(2),
        name="modulation",
    )(cc, mod_w, mod_b.reshape(depth, 1, n))


def _mod_spec(tiles_per_batch, d):
    return pl.BlockSpec((1, 1, d), lambda i: (i // tiles_per_batch, 0, 0))


FF_CHUNK = 1024


def _tail_math(a, x, wo_ref, bo_ref, ga_ref, g_ref, sh_ref, sc_ref, gm_ref, w1_ref, w2_ref, fg_ref,
               final_norm):
    y = jnp.dot(a.astype(BF16), wo_ref[...], preferred_element_type=F32) + bo_ref[...]
    x = x + ga_ref[0] * y
    h = _norm_mod(x, g_ref[...], sh_ref[0], sc_ref[0]).astype(BF16)
    acc = jnp.zeros(x.shape, F32)
    for c in range(w1_ref.shape[1] // FF_CHUNK):
        u = jnp.dot(h, w1_ref[:, c * FF_CHUNK:(c + 1) * FF_CHUNK], preferred_element_type=F32)
        u = jnp.square(jnp.maximum(u, 0.0)).astype(BF16)
        acc = acc + jnp.dot(u, w2_ref[c * FF_CHUNK:(c + 1) * FF_CHUNK, :],
                            preferred_element_type=F32)
    out = x + gm_ref[0] * acc
    if final_norm:
        out = _rms(out, fg_ref[...])
    return out


def _block_tail_kernel(a_ref, x_ref, *refs, final_norm):
    o_ref = refs[-1]
    o_ref[...] = _tail_math(a_ref[...], x_ref[...], *refs[:-1], final_norm=final_norm)


def _block_tail(a, w_o, b_o, gate_a, x, g, sh, sc, gate_m, w1, w2, final_g, *, final_norm, tm):
    t, d = x.shape
    nb = sh.shape[0]
    tpb = t // nb // tm
    row = lambda n: pl.BlockSpec((tm, n), lambda i: (i, 0))
    return pl.pallas_call(
        functools.partial(_block_tail_kernel, final_norm=final_norm),
        grid=(t // tm,),
        in_specs=[
            row(a.shape[1]), row(d),
            _resident(w_o.shape), _resident((1, d)), _mod_spec(tpb, d),
            _resident((1, d)), _mod_spec(tpb, d), _mod_spec(tpb, d), _mod_spec(tpb, d),
            _resident(w1.shape), _resident(w2.shape), _resident((1, d)),
        ],
        out_specs=row(d),
        out_shape=jax.ShapeDtypeStruct((t, d), F32),
        compiler_params=_cparams(1),
        name="block_tail",
    )(a, x, w_o, b_o, gate_a, g, sh, sc, gate_m, w1, w2, final_g)


def _dft_mats():
    cg = 256
    j = np.arange(cg)
    th = 2.0 * np.pi * np.outer(j, j) / cg
    wc = np.concatenate([np.cos(th), -np.sin(th)], axis=1) / 16.0
    ctx_seq = np.concatenate([np.cos(th), np.sin(th)], axis=1) / 16.0
    r = np.arange(FFT_RADIX)
    ph = 2.0 * np.pi * np.outer(r, r) / FFT_RADIX
    dr, di = np.cos(ph) / 8.0, -np.sin(ph) / 8.0
    m1 = np.block([[dr, -di], [di, dr]])
    m2 = np.concatenate([dr, -di], axis=1)
    tw = 2.0 * np.pi * np.outer(r, r) / (FFT_RADIX * FFT_RADIX)
    tr = np.repeat(np.cos(tw)[:, :, None], LANES, axis=2)
    ti = np.repeat(-np.sin(tw)[:, :, None], LANES, axis=2)
    as_bf = lambda a: jnp.asarray(a, F32).astype(BF16)
    return dict(wc=as_bf(wc), ctx_seq=as_bf(ctx_seq), m1=as_bf(m1), m2=as_bf(m2),
                tr=jnp.asarray(tr, F32), ti=jnp.asarray(ti, F32))


def _chan_dft(h, wc_ref):
    cg = wc_ref.shape[0]
    res, ims = [], []
    for grp in range(h.shape[1] // cg):
        z = jnp.dot(h[:, grp * cg:(grp + 1) * cg], wc_ref[...], preferred_element_type=F32)
        res.append(z[:, :cg])
        ims.append(z[:, cg:])
    return jnp.concatenate(res, axis=1), jnp.concatenate(ims, axis=1)


def _pack_complex(re, im):
    bits = lambda v: lax.bitcast_convert_type(v.astype(BF16).astype(F32), jnp.uint32)
    return (bits(re) >> 16) | bits(im)


def _unpack_complex(w):
    re = lax.bitcast_convert_type(w << 16, F32)
    im = lax.bitcast_convert_type(w & jnp.uint32(0xFFFF0000), F32)
    return jnp.concatenate([re, im], axis=0).astype(BF16)


FFT_GROUP = 8


def _digit_spec(d):
    return pl.BlockSpec((None, FFT_RADIX, FFT_GROUP, d), lambda b, j: (b, 0, j, 0))


def _fourier_front_kernel(x_ref, g_ref, sh_ref, sc_ref, wc_ref, tr_ref, ti_ref, w1_ref, a_ref,
                          zr_ref, zi_ref):
    r, kg = FFT_RADIX, FFT_GROUP
    d = x_ref.shape[-1]
    reps = d // LANES
    x = x_ref[...].reshape(r * kg, d)
    h = _norm_mod(x, g_ref[...], sh_ref[0], sc_ref[0]).astype(BF16)
    zr, zi = _chan_dft(h, wc_ref)
    for ct in range(reps):
        zr_ref[ct] = zr[:, ct * LANES:(ct + 1) * LANES]
        zi_ref[ct] = zi[:, ct * LANES:(ct + 1) * LANES]
    for j in range(kg):
        rows = pl.ds(j, r, stride=kg)
        pick = lambda ref: jnp.concatenate([ref[ct, rows, :] for ct in range(reps)], axis=1)
        z = jnp.concatenate([pick(zr_ref), pick(zi_ref)], axis=0).astype(BF16)
        a = jnp.dot(w1_ref[...], z, preferred_element_type=F32)
        ar, ai = a[:r], a[r:]
        tr = jnp.tile(tr_ref[j], (1, reps))
        ti = jnp.tile(ti_ref[j], (1, reps))
        a_ref[0, j] = _pack_complex(ar * tr - ai * ti, ar * ti + ai * tr)


def _fourier_front(x, g, sh, sc, mats, *, batch):
    t, d = x.shape
    r, kg = FFT_RADIX, FFT_GROUP
    const = lambda a: pl.BlockSpec(a.shape, lambda b, j: (0,) * a.ndim)
    mod = pl.BlockSpec((1, 1, d), lambda b, j: (b, 0, 0))
    tw = pl.BlockSpec((kg, r, LANES), lambda b, j: (j, 0, 0))
    return pl.pallas_call(
        _fourier_front_kernel,
        grid=(batch, r // kg),
        in_specs=[_digit_spec(d), const(g), mod, mod, const(mats["wc"]), tw, tw, const(mats["m1"])],
        out_specs=pl.BlockSpec((1, kg, r, d), lambda b, j: (b, j, 0, 0)),
        out_shape=jax.ShapeDtypeStruct((batch, r, r, d), jnp.uint32),
        scratch_shapes=[pltpu.VMEM((d // LANES, r * kg, LANES), F32)] * 2,
        compiler_params=_cparams(2),
        name="fourier_front",
    )(x.reshape(batch, r, r, d), g, sh, sc, mats["wc"], mats["tr"], mats["ti"], mats["m1"])


def _fourier_tail_kernel(a_ref, m2_ref, x_ref, *refs, final_norm):
    r, kg = FFT_RADIX, FFT_GROUP
    d = a_ref.shape[-1]
    reps = d // LANES
    o_ref, aw_ref, f_ref = refs[-3:]
    a = a_ref[...].reshape(r * kg, d)
    for ct in range(reps):
        aw_ref[ct] = a[:, ct * LANES:(ct + 1) * LANES]
    for j in range(kg):
        rows = pl.ds(j, r, stride=kg)
        w = jnp.concatenate([aw_ref[ct, rows, :] for ct in range(reps)], axis=1)
        f = jnp.dot(m2_ref[...], _unpack_complex(w), preferred_element_type=F32)
        for ct in range(reps):
            f_ref[ct, rows, :] = f[:, ct * LANES:(ct + 1) * LANES]
    f = jnp.concatenate([f_ref[ct] for ct in range(reps)], axis=1)
    out = _tail_math(f, x_ref[...].reshape(r * kg, d), *refs[:-3], final_norm=final_norm)
    o_ref[...] = out.reshape(r, kg, d)


def _fourier_tail(a, mats, w_o, b_o, gate_a, x, g, sh, sc, gate_m, w1, w2, final_g, *, batch,
                  final_norm):
    t, d = x.shape
    r, kg = FFT_RADIX, FFT_GROUP
    const = lambda v: pl.BlockSpec(v.shape, lambda b, j: (0,) * v.ndim, pipeline_mode=pl.Buffered(1))
    mod = pl.BlockSpec((1, 1, d), lambda b, j: (b, 0, 0))
    out = pl.pallas_call(
        functools.partial(_fourier_tail_kernel, final_norm=final_norm),
        grid=(batch, r // kg),
        in_specs=[_digit_spec(d), const(mats["m2"]), _digit_spec(d),
                  const(w_o), const(b_o), mod, const(g), mod, mod, mod,
                  const(w1), const(w2), const(final_g)],
        out_specs=_digit_spec(d),
        out_shape=jax.ShapeDtypeStruct((batch, r, r, d), F32),
        scratch_shapes=[pltpu.VMEM((d // LANES, r * kg, LANES), jnp.uint32),
                        pltpu.VMEM((d // LANES, r * kg, LANES), F32)],
        compiler_params=_cparams(2),
        name="fourier_tail",
    )(a, mats["m2"], x.reshape(batch, r, r, d), w_o, b_o, gate_a, g, sh, sc, gate_m, w1, w2, final_g)
    return out.reshape(t, d)


def _fourier_ctx_kernel(x_ref, g_ref, sh_ref, sc_ref, wc_ref, seq_ref, f_ref):
    h = _norm_mod(x_ref[...], g_ref[...], sh_ref[0], sc_ref[0]).astype(BF16)
    zr, zi = _chan_dft(h, wc_ref)
    z = jnp.concatenate([zr, zi], axis=0).astype(BF16)
    f_ref[...] = jnp.dot(seq_ref[...], z, preferred_element_type=F32)


def _fourier_ctx(x, g, sh, sc, mats, *, ctx_len):
    t, d = x.shape
    n_tiles = t // ctx_len
    return pl.pallas_call(
        _fourier_ctx_kernel,
        grid=(n_tiles,),
        in_specs=[
            pl.BlockSpec((ctx_len, d), lambda i: (i, 0)),
            _resident((1, d)), _mod_spec(n_tiles, d), _mod_spec(n_tiles, d),
            _resident(mats["wc"].shape), _resident(mats["ctx_seq"].shape),
        ],
        out_specs=pl.BlockSpec((ctx_len, d), lambda i: (i, 0)),
        out_shape=jax.ShapeDtypeStruct((t, d), F32),
        compiler_params=_cparams(1),
        name="fourier_ctx",
    )(x, g, sh, sc, mats["wc"], mats["ctx_seq"])


def _rope_tables(rows, dim, lane_of_dim):
    n_freq = dim // 4
    inv_freq = ROPE_BASE ** (-jnp.arange(n_freq, dtype=F32) / n_freq)
    t = jnp.arange(rows * GRID_W)
    row = (t // GRID_W).astype(F32)
    col = (t % GRID_W).astype(F32)
    ang = jnp.stack([row[:, None] * inv_freq, col[:, None] * inv_freq], axis=1)
    cos, sin = jnp.cos(ang), jnp.sin(ang)
    dims = np.array([lane_of_dim(l) for l in range(LANES)])
    live = dims >= 0
    dd = np.where(live, dims, 0)
    axis, pair, freq = dd // (2 * n_freq), (dd % (2 * n_freq)) // n_freq, dd % n_freq
    sign = np.where(pair == 0, -1.0, 1.0).astype(np.float32)
    cos_t = jnp.where(live[None, :], cos[:, axis, freq], 1.0)
    sin_t = jnp.where(live[None, :], sin[:, axis, freq] * sign[None, :], 0.0)
    return cos_t.astype(F32), sin_t.astype(F32)


def _split_sub_heads(q):
    lane = lax.broadcasted_iota(jnp.int32, q.shape, 1)
    first = (lane & (DIFF_HEAD_DIM // 2)) == 0
    zero = jnp.zeros_like(q)
    return [jnp.where(first, q, zero), jnp.where(first, zero, q)]


def _scores(q, k_refs):
    nt = (((1,), (1,)), ((), ()))
    return [lax.dot_general(q, k[0], nt, preferred_element_type=F32) for k in k_refs]


def _attend(q, k_refs, vaug_refs):
    parts = _scores(q, k_refs)
    m = functools.reduce(jnp.maximum, [jnp.max(p, axis=-1, keepdims=True) for p in parts])
    acc = None
    for p, v in zip(parts, vaug_refs):
        e = jnp.exp2(p - m).astype(BF16)
        part = jnp.dot(e, v[0], preferred_element_type=F32)
        acc = part if acc is None else acc + part
    return acc[:, :LANES] / acc[:, LANES:]


def _pipe_attn_kernel(*refs, mode, n_seg, lam_init):
    n_par = 5 if mode == "diff" else 0
    par = refs[:n_par]
    q_ref = refs[n_par]
    k_refs = refs[n_par + 1:n_par + 1 + n_seg]
    v_refs = refs[n_par + 1 + n_seg:n_par + 1 + 2 * n_seg]
    o_ref, s_a, s_b, m_a, m_b = refs[n_par + 1 + 2 * n_seg:]
    t = pl.program_id(0)
    nt = (((1,), (1,)), ((), ()))

    def scores(s_w, m_w):
        q = q_ref[0]
        if mode == "diff":
            qs = _split_sub_heads(q)
        else:
            qs = [q[:, :LANES], q[:, LANES:]]
        for c in range(2):
            m, off = None, 0
            for k in k_refs:
                kk = k[0] if mode == "diff" else k[0, :, c * LANES:(c + 1) * LANES]
                p = lax.dot_general(qs[c], kk, nt, preferred_element_type=F32)
                s_w[c, :, off:off + p.shape[1]] = p
                off += p.shape[1]
                pm = jnp.max(p, axis=-1, keepdims=True)
                m = pm if m is None else jnp.maximum(m, pm)
            m_w[c] = jnp.broadcast_to(m, m_w.shape[1:])

    def values(s_r, m_r):
        outs = []
        for c in range(2):
            m = m_r[c][:, :1]
            acc, off = None, 0
            for v in v_refs:
                w = v.shape[1]
                e = jnp.exp2(s_r[c, :, off:off + w] - m).astype(BF16)
                part = jnp.dot(e, v[0], preferred_element_type=F32)
                acc = part if acc is None else acc + part
                off += w
            outs.append(acc[:, :LANES] / acc[:, LANES:])
        if mode == "diff":
            lq1, lk1, lq2, lk2, sg = par
            lam = (jnp.exp(jnp.sum(lq1[...] * lk1[...], axis=-1, keepdims=True))
                   - jnp.exp(jnp.sum(lq2[...] * lk2[...], axis=-1, keepdims=True)) + lam_init)
            o = outs[0] - lam * outs[1]
            o_ref[0] = (_rms(o, sg[...]) * (1.0 - lam_init)).astype(BF16)
        else:
            lane = lax.broadcasted_iota(jnp.int32, outs[0].shape, 1)
            o_ref[0] = jnp.where(lane < MLA_V_DIM, outs[0], outs[1]).astype(BF16)

    @pl.when(t == 0)
    def _():
        s_b[...] = jnp.zeros(s_b.shape, F32)
        m_b[...] = jnp.zeros(m_b.shape, F32)

    @pl.when(t % 2 == 0)
    def _():
        scores(s_a, m_a)
        values(s_b, m_b)

    @pl.when(t % 2 == 1)
    def _():
        scores(s_b, m_b)
        values(s_a, m_a)


def _pipe_attn(params, q_arr, k_arrs, v_arrs, *, mode, units, q_width, k_block, tq, lam_init=0.0):
    b, sq, _ = q_arr.shape
    nq = sq // tq
    n_items = b * units * nq
    n_seg = len(k_arrs)
    sk = sum(a.shape[1] for a in k_arrs)

    def item(t, lag):
        tt = jnp.clip(t - lag, 0, n_items - 1)
        return tt // (units * nq), (tt // nq) % units, tt % nq

    def q_map(t):
        bb, u, i = item(t, 0)
        return bb, i, u

    def k_map(t):
        bb, u, _ = item(t, 0)
        return bb, 0, k_block + u

    def v_map(t):
        bb, u, _ = item(t, 1)
        return bb, 0, k_block + u

    def o_map(t):
        bb, u, i = item(t, 1)
        return bb, i, u

    in_specs = [pl.BlockSpec(p.shape, lambda t: (0, 0)) for p in params]
    in_specs += [pl.BlockSpec((1, tq, q_width), q_map)]
    in_specs += [pl.BlockSpec((1, a.shape[1], q_width), k_map) for a in k_arrs]
    in_specs += [pl.BlockSpec((1, a.shape[1], 2 * LANES), v_map) for a in v_arrs]
    return pl.pallas_call(
        functools.partial(_pipe_attn_kernel, mode=mode, n_seg=n_seg, lam_init=lam_init),
        grid=(n_items + 1,),
        in_specs=in_specs,
        out_specs=pl.BlockSpec((1, tq, LANES), o_map),
        out_shape=jax.ShapeDtypeStruct((b, sq, units * LANES), BF16),
        scratch_shapes=[pltpu.VMEM((2, tq, sk), F32), pltpu.VMEM((2, tq, sk), F32),
                        pltpu.VMEM((2, tq, LANES), F32), pltpu.VMEM((2, tq, LANES), F32)],
        compiler_params=_cparams(1),
        name=mode + "_pipe_attn",
    )(*params, q_arr, *k_arrs, *v_arrs)


def _diff_qkv_kernel(x_ref, g_ref, sh_ref, sc_ref, w_ref, cos_ref, sin_ref, o_ref, *, rope, qscale):
    h = _norm_mod(x_ref[...], g_ref[...], sh_ref[0], sc_ref[0]).astype(BF16)
    d = x_ref.shape[1]
    acc = jnp.dot(h, w_ref[...], preferred_element_type=F32)
    heads = d // LANES
    ones = jnp.ones((x_ref.shape[0], LANES), BF16)
    for tile in range(2 * heads):
        a = acc[:, tile * LANES:(tile + 1) * LANES]
        if rope:
            a = _rope_tile(a, cos_ref[...], sin_ref[...])
        if tile < heads:
            a = a * qscale
        o_ref[:, tile * LANES:(tile + 1) * LANES] = a.astype(BF16)
    for hd in range(heads):
        src = (2 * heads + hd) * LANES
        dst = (2 * heads + 2 * hd) * LANES
        o_ref[:, dst:dst + LANES] = acc[:, src:src + LANES].astype(BF16)
        o_ref[:, dst + LANES:dst + 2 * LANES] = ones


def _diff_qkv(x, g, sh, sc, w, cos, sin, *, rope, tm):
    t, d = x.shape
    nb = sh.shape[0]
    tpb = t // nb // tm
    spb = t // nb // tm
    n = w.shape[1] + d
    qscale = DIFF_HEAD_DIM ** -0.5 * LOG2E
    tab = pl.BlockSpec((tm, LANES), (lambda i: (i % spb, 0)) if rope else (lambda i: (0, 0)))
    return pl.pallas_call(
        functools.partial(_diff_qkv_kernel, rope=rope, qscale=qscale),
        grid=(t // tm,),
        in_specs=[
            pl.BlockSpec((tm, d), lambda i: (i, 0)),
            _resident((1, d)), _mod_spec(tpb, d), _mod_spec(tpb, d),
            _resident(w.shape), tab, tab,
        ],
        out_specs=pl.BlockSpec((tm, n), lambda i: (i, 0)),
        out_shape=jax.ShapeDtypeStruct((t, n), BF16),
        compiler_params=_cparams(1),
        name="diff_qkv",
    )(x, g, sh, sc, w, cos, sin)


def _diff_attn_kernel(lq1_ref, lk1_ref, lq2_ref, lk2_ref, sg_ref, q_ref, *refs, n_seg, lam_init):
    k_refs, v_refs, o_ref = refs[:n_seg], refs[n_seg:2 * n_seg], refs[2 * n_seg]
    lam = (jnp.exp(jnp.sum(lq1_ref[...] * lk1_ref[...], axis=-1, keepdims=True))
           - jnp.exp(jnp.sum(lq2_ref[...] * lk2_ref[...], axis=-1, keepdims=True)) + lam_init)
    q1, q2 = _split_sub_heads(q_ref[0])
    o = _attend(q1, k_refs, v_refs) - lam * _attend(q2, k_refs, v_refs)
    o_ref[0] = (_rms(o, sg_ref[...]) * (1.0 - lam_init)).astype(BF16)


def _diff_attn(lams, subln_g, q_arr, kv_arrs, *, heads, tq, lam_init):
    b, sq, _ = q_arr.shape
    n_seg = len(kv_arrs)
    lam_specs = [pl.BlockSpec(l.shape, lambda bb, h, i: (0, 0)) for l in lams]
    in_specs = lam_specs + [
        pl.BlockSpec((1, DIFF_V_DIM), lambda bb, h, i: (0, 0)),
        pl.BlockSpec((1, tq, LANES), lambda bb, h, i: (bb, i, h)),
    ]
    in_specs += [pl.BlockSpec((1, a.shape[1], LANES), lambda bb, h, i: (bb, 0, heads + h))
                 for a in kv_arrs]
    in_specs += [pl.BlockSpec((1, a.shape[1], 2 * LANES), lambda bb, h, i: (bb, 0, heads + h))
                 for a in kv_arrs]
    return pl.pallas_call(
        functools.partial(_diff_attn_kernel, n_seg=n_seg, lam_init=lam_init),
        grid=(b, heads, sq // tq),
        in_specs=in_specs,
        out_specs=pl.BlockSpec((1, tq, LANES), lambda bb, h, i: (bb, i, h)),
        out_shape=jax.ShapeDtypeStruct((b, sq, heads * DIFF_V_DIM), BF16),
        compiler_params=_cparams(3),
        name="diff_attn",
    )(*lams, subln_g, q_arr, *kv_arrs, *kv_arrs)


def _mla_down_kernel(x_ref, g_ref, sh_ref, sc_ref, w_ref, qg_ref, kvg_ref, cos_ref, sin_ref,
                     cq_ref, ckv_ref, kr_ref, *, rope):
    h = _norm_mod(x_ref[...], g_ref[...], sh_ref[0], sc_ref[0]).astype(BF16)
    acc = jnp.dot(h, w_ref[...], preferred_element_type=F32)
    cq_ref[...] = _rms(acc[:, :MLA_Q_RANK], qg_ref[...]).astype(BF16)
    ckv_ref[...] = _rms(acc[:, MLA_Q_RANK:MLA_Q_RANK + MLA_KV_RANK], kvg_ref[...]).astype(BF16)
    kr = acc[:, MLA_Q_RANK + MLA_KV_RANK:]
    if rope:
        kr = _rope_tile(kr, cos_ref[...], sin_ref[...])
    kr_ref[...] = kr.astype(BF16)


def _mla_up_kernel(cq_ref, ckv_ref, kr_ref, wq_ref, wk_ref, wv_ref, cos_ref, sin_ref,
                   q_ref, k_ref, v_ref, *, rope, qscale):
    q = jnp.dot(cq_ref[...], wq_ref[...], preferred_element_type=F32)
    for hd in range(MLA_HEADS):
        a = q[:, hd * LANES:(hd + 1) * LANES]
        if rope:
            a = _rope_tile(a, cos_ref[...], sin_ref[...])
        q_ref[:, hd * LANES:(hd + 1) * LANES] = (a * qscale).astype(BF16)
    k = jnp.dot(ckv_ref[...], wk_ref[...], preferred_element_type=F32)
    kr = kr_ref[...].astype(F32)
    for hd in range(MLA_HEADS):
        k_ref[:, hd * LANES:(hd + 1) * LANES] = (k[:, hd * LANES:(hd + 1) * LANES] + kr).astype(BF16)
    v = jnp.dot(ckv_ref[...], wv_ref[...], preferred_element_type=F32)
    ones = jnp.ones((v.shape[0], LANES), BF16)
    for pair in range(MLA_HEADS // 2):
        v_ref[:, 2 * pair * LANES:(2 * pair + 1) * LANES] = v[:, pair * LANES:(pair + 1) * LANES].astype(BF16)
        v_ref[:, (2 * pair + 1) * LANES:(2 * pair + 2) * LANES] = ones


def _mla_project(x, g, sh, sc, w_down, qg, kvg, wq, wk, wv, cos, sin, *, rope, tm):
    t, d = x.shape
    nb = sh.shape[0]
    tpb = t // nb // tm
    tab = pl.BlockSpec((tm, LANES), (lambda i: (i % tpb, 0)) if rope else (lambda i: (0, 0)))
    row = lambda n: pl.BlockSpec((tm, n), lambda i: (i, 0))
    cq, ckv, kr = pl.pallas_call(
        functools.partial(_mla_down_kernel, rope=rope),
        grid=(t // tm,),
        in_specs=[row(d), _resident((1, d)), _mod_spec(tpb, d), _mod_spec(tpb, d),
                  _resident(w_down.shape), _resident(qg.shape), _resident(kvg.shape), tab, tab],
        out_specs=[row(MLA_Q_RANK), row(MLA_KV_RANK), row(LANES)],
        out_shape=[jax.ShapeDtypeStruct((t, MLA_Q_RANK), BF16),
                   jax.ShapeDtypeStruct((t, MLA_KV_RANK), BF16),
                   jax.ShapeDtypeStruct((t, LANES), BF16)],
        compiler_params=_cparams(1),
        name="mla_down",
    )(x, g, sh, sc, w_down, qg, kvg, cos, sin)
    qscale = (MLA_NOPE_DIM + MLA_ROPE_DIM) ** -0.5 * LOG2E
    nq, nv = wq.shape[1], 2 * wv.shape[1]
    return pl.pallas_call(
        functools.partial(_mla_up_kernel, rope=rope, qscale=qscale),
        grid=(t // tm,),
        in_specs=[row(MLA_Q_RANK), row(MLA_KV_RANK), row(LANES),
                  _resident(wq.shape), _resident(wk.shape), _resident(wv.shape), tab, tab],
        out_specs=[row(nq), row(nq), row(nv)],
        out_shape=[jax.ShapeDtypeStruct((t, nq), BF16), jax.ShapeDtypeStruct((t, nq), BF16),
                   jax.ShapeDtypeStruct((t, nv), BF16)],
        compiler_params=_cparams(1),
        name="mla_up",
    )(cq, ckv, kr, wq, wk, wv, cos, sin)


def _mla_attn_kernel(q_ref, *refs, n_seg):
    k_refs, v_refs, o_ref = refs[:n_seg], refs[n_seg:2 * n_seg], refs[2 * n_seg]
    outs = []
    for hd in range(2):
        q = q_ref[0, :, hd * LANES:(hd + 1) * LANES]
        ks = [k.at[:, :, hd * LANES:(hd + 1) * LANES] for k in k_refs]
        outs.append(_attend(q, ks, v_refs))
    lane = lax.broadcasted_iota(jnp.int32, outs[0].shape, 1)
    o_ref[0] = jnp.where(lane < MLA_V_DIM, outs[0], outs[1]).astype(BF16)


def _mla_attn(q, k_arrs, v_arrs, *, tq):
    b, sq, nq = q.shape
    pairs = nq // (2 * LANES)
    n_seg = len(k_arrs)
    in_specs = [pl.BlockSpec((1, tq, 2 * LANES), lambda bb, h, i: (bb, i, h))]
    in_specs += [pl.BlockSpec((1, a.shape[1], 2 * LANES), lambda bb, h, i: (bb, 0, h)) for a in k_arrs]
    in_specs += [pl.BlockSpec((1, a.shape[1], 2 * LANES), lambda bb, h, i: (bb, 0, h)) for a in v_arrs]
    return pl.pallas_call(
        functools.partial(_mla_attn_kernel, n_seg=n_seg),
        grid=(b, pairs, sq // tq),
        in_specs=in_specs,
        out_specs=pl.BlockSpec((1, tq, LANES), lambda bb, h, i: (bb, i, h)),
        out_shape=jax.ShapeDtypeStruct((b, sq, pairs * LANES), BF16),
        compiler_params=_cparams(3),
        name="mla_attn",
    )(q, *k_arrs, *v_arrs)


def _take_cols(w, src):
    src = np.asarray(src)
    return jnp.where(src >= 0, jnp.take(w, np.maximum(src, 0), axis=-1), 0).astype(w.dtype)


def _mla_weights(w_down, w_uq, w_ukv):
    hq = MLA_NOPE_DIM + MLA_ROPE_DIM
    lo = MLA_Q_RANK + MLA_KV_RANK
    kinds = [_mla_lane_dim(l) for l in range(LANES)]
    q_src = np.array([{"nope": dim, "rope": MLA_NOPE_DIM + dim, "pad": -1}[k] for k, dim in kinds])
    k_src = np.array([dim if k == "nope" else -1 for k, dim in kinds])
    r_src = np.array([lo + dim if k == "rope" else -1 for k, dim in kinds])
    wd = jnp.concatenate([w_down[:, :lo], _take_cols(w_down, r_src)], axis=1)
    wq = _take_cols(w_uq.reshape(MLA_Q_RANK, MLA_HEADS, hq), q_src)
    wkv = w_ukv.reshape(MLA_KV_RANK, MLA_HEADS, MLA_NOPE_DIM + MLA_V_DIM)
    wk = _take_cols(wkv, k_src)
    wv = wkv[:, :, MLA_NOPE_DIM:].reshape(MLA_KV_RANK, MLA_HEADS * MLA_V_DIM)
    flat = lambda w: w.reshape(w.shape[0], MLA_HEADS * LANES)
    return wd.astype(BF16), flat(wq).astype(BF16), flat(wk).astype(BF16), wv.astype(BF16)


def _diff_weights(w_qkv):
    d = w_qkv.shape[0]
    src = np.array([sub * DIFF_HEAD_DIM + dim for sub, dim in map(_diff_lane_dim, range(LANES))])
    qk = w_qkv[:, :2 * d].reshape(d, 2 * d // LANES, LANES)
    qk = _take_cols(qk, src).reshape(d, 2 * d)
    return jnp.concatenate([qk, w_qkv[:, 2 * d:]], axis=1).astype(BF16)


def kernel(x, c, ctx, c_ctx, mod_w, mod_b, norm_mix_g, norm_mlp_g, final_g, mlp_w1, mlp_w2, fourier_w, fourier_b, diff_w_qkv, diff_lambda_q1, diff_lambda_k1, diff_lambda_q2, diff_lambda_k2, diff_subln_g, diff_w_o, mla_w_down, mla_q_norm_g, mla_kv_norm_g, mla_w_uq, mla_w_ukv, mla_w_o):
    batch, seq, d = x.shape
    ctx_len = ctx.shape[1]
    depth = mod_w.shape[0]
    rows = seq // GRID_W
    assert seq == FFT_RADIX * FFT_RADIX and d == FOURIER_GROUPS * 256 and ctx_len == 256
    tm = 512

    cc_rows = -(-(batch + 1) // 8) * 8
    cc = jnp.zeros((cc_rows, d), F32).at[:batch].set(c).at[batch].set(c_ctx)
    mods = _modulation(cc, mod_w, mod_b).reshape(depth, cc_rows, 6, d)
    mats = _dft_mats()
    zero_bias = jnp.zeros((1, d), F32)
    final_g2 = final_g.reshape(1, d)

    cos_d, sin_d = _rope_tables(rows, DIFF_HEAD_DIM, lambda l: _diff_lane_dim(l)[1])
    lane_mla = lambda l: _mla_lane_dim(l)[1] if _mla_lane_dim(l)[0] == "rope" else -1
    cos_m, sin_m = _rope_tables(rows, MLA_ROPE_DIM, lane_mla)

    x_lat = x.reshape(batch * seq, d)
    x_ctx = ctx.reshape(batch * ctx_len, d)
    for i in range(depth):
        kind, j = i % N_MIXERS, i // N_MIXERS
        update_ctx = i < depth - 1
        ctx_feeds_mixer = update_ctx or kind != 0
        lat = [mods[i, :batch, k].reshape(batch, 1, d) for k in range(6)]
        cm = [mods[i, batch:batch + 1, k].reshape(1, 1, d) for k in range(6)]
        g_mix = norm_mix_g[i].reshape(1, d)
        g_mlp = norm_mlp_g[i].reshape(1, d)
        a_ctx = None
        if kind == 0:
            w_o = fourier_w[j].astype(BF16)
            b_o = fourier_b[j].reshape(1, d)
            a_lat = _fourier_front(x_lat, g_mix, lat[0], lat[1], mats, batch=batch)
            if update_ctx:
                a_ctx = _fourier_ctx(x_ctx, g_mix, cm[0], cm[1], mats, ctx_len=ctx_len)
        elif kind == 1:
            heads = d // DIFF_V_DIM
            w_qkv = _diff_weights(diff_w_qkv[j])
            w_o = diff_w_o[j].astype(BF16)
            b_o = zero_bias
            lam_init = 0.8 - 0.6 * math.exp(-0.3 * i)
            lams = [v[j].reshape(1, DIFF_HEAD_DIM) for v in
                    (diff_lambda_q1, diff_lambda_k1, diff_lambda_q2, diff_lambda_k2)]
            sg = diff_subln_g[j].reshape(1, DIFF_V_DIM)
            qkv_l = _diff_qkv(x_lat, g_mix, lat[0], lat[1], w_qkv, cos_d, sin_d, rope=True, tm=tm)
            qkv_l = qkv_l.reshape(batch, seq, 4 * d)
            qkv_c = _diff_qkv(x_ctx, g_mix, cm[0], cm[1], w_qkv, cos_d, sin_d, rope=False, tm=ctx_len)
            qkv_c = qkv_c.reshape(batch, ctx_len, 4 * d)
            o_l = _pipe_attn(lams + [sg], qkv_l, [qkv_c, qkv_l], [qkv_c, qkv_l], mode="diff",
                             units=heads, q_width=LANES, k_block=heads, tq=512, lam_init=lam_init)
            a_lat = o_l.reshape(batch * seq, d)
            if update_ctx:
                o_c = _diff_attn(lams, sg, qkv_c, [qkv_c], heads=heads, tq=ctx_len, lam_init=lam_init)
                a_ctx = o_c.reshape(batch * ctx_len, d)
        else:
            wd, wq, wk, wv = _mla_weights(mla_w_down[j], mla_w_uq[j], mla_w_ukv[j])
            w_o = mla_w_o[j].astype(BF16)
            b_o = zero_bias
            qg = mla_q_norm_g[j].reshape(1, MLA_Q_RANK)
            kvg = mla_kv_norm_g[j].reshape(1, MLA_KV_RANK)
            q_l, k_l, v_l = _mla_project(x_lat, g_mix, lat[0], lat[1], wd, qg, kvg, wq, wk, wv,
                                         cos_m, sin_m, rope=True, tm=tm)
            q_c, k_c, v_c = _mla_project(x_ctx, g_mix, cm[0], cm[1], wd, qg, kvg, wq, wk, wv,
                                         cos_m, sin_m, rope=False, tm=ctx_len)
            r3 = lambda a, s: a.reshape(batch, s, a.shape[1])
            o_l = _pipe_attn([], r3(q_l, seq), [r3(k_c, ctx_len), r3(k_l, seq)],
                             [r3(v_c, ctx_len), r3(v_l, seq)], mode="mla", units=MLA_HEADS // 2,
                             q_width=2 * LANES, k_block=0, tq=512)
            a_lat = o_l.reshape(batch * seq, d)
            if update_ctx:
                o_c = _mla_attn(r3(q_c, ctx_len), [r3(k_c, ctx_len)], [r3(v_c, ctx_len)], tq=ctx_len)
                a_ctx = o_c.reshape(batch * ctx_len, d)
        w1 = mlp_w1[i].astype(BF16)
        w2 = mlp_w2[i].astype(BF16)
        last = i == depth - 1
        if kind == 0:
            x_lat = _fourier_tail(a_lat, mats, w_o, b_o, lat[2], x_lat, g_mlp, lat[3], lat[4], lat[5],
                                  w1, w2, final_g2, batch=batch, final_norm=last)
        else:
            x_lat = _block_tail(a_lat, w_o, b_o, lat[2], x_lat, g_mlp, lat[3], lat[4], lat[5], w1, w2,
                                final_g2, final_norm=last, tm=tm)
        if update_ctx:
            x_ctx = _block_tail(a_ctx, w_o, b_o, cm[2], x_ctx, g_mlp, cm[3], cm[4], cm[5], w1, w2,
                                final_g2, final_norm=False, tm=ctx_len)
    return x_lat.reshape(batch, seq, d)
```

```python
import functools
import math

import numpy as np
import jax
import jax.numpy as jnp
from jax import lax
from jax.experimental import pallas as pl
from jax.experimental.pallas import tpu as pltpu

F32 = jnp.float32
BF16 = jnp.bfloat16

GRID_W = 64
RMS_EPS = 1e-6
ROPE_BASE = 10000.0
FOURIER_GROUPS = 4
DIFF_HEAD_DIM = 64
DIFF_V_DIM = 128
MLA_HEADS = 16
MLA_NOPE_DIM = 64
MLA_ROPE_DIM = 32
MLA_V_DIM = 64
MLA_Q_RANK = 256
MLA_KV_RANK = 128
N_MIXERS = 3

LANES = 128
FFT_RADIX = 64
VMEM_LIMIT = 56 * 1024 * 1024
LOG2E = 1.4426950408889634


def _cparams(n_axes):
    return pltpu.CompilerParams(
        dimension_semantics=("arbitrary",) * n_axes, vmem_limit_bytes=VMEM_LIMIT)


def _resident(shape):
    nd = len(shape)
    return pl.BlockSpec(shape, lambda *_: (0,) * nd, pipeline_mode=pl.Buffered(1))


def _norm_mod(x, g, sh, sc):
    r = lax.rsqrt(jnp.mean(x * x, axis=-1, keepdims=True) + RMS_EPS)
    return ((x * r) * g) * (1.0 + sc) + sh


def _rms(x, g):
    r = lax.rsqrt(jnp.mean(x * x, axis=-1, keepdims=True) + RMS_EPS)
    return (x * r) * g


def _rope_tile(x, cos, sin):
    return x * cos + pltpu.roll(x, LANES // 2, 1) * sin


def _diff_lane_dim(lane):
    pair, sub, axis, freq = lane // 64, (lane % 64) // 32, (lane % 32) // 16, lane % 16
    return sub, axis * 32 + pair * 16 + freq


def _mla_lane_dim(lane):
    half, off = lane // 64, lane % 64
    if off < 16:
        return "rope", (off // 8) * 16 + half * 8 + off % 8
    if half == 0:
        return "nope", off - 16
    return ("nope", 48 + off - 16) if off < 32 else ("pad", 0)


def _mod_kernel(c_ref, w_ref, b_ref, o_ref):
    c = c_ref[...]
    s = (c / (1.0 + jnp.exp(-c))).astype(BF16)
    o_ref[0] = jnp.dot(s, w_ref[0].astype(BF16), preferred_element_type=F32) + b_ref[0]


def _modulation(cc, mod_w, mod_b):
    depth, d, n = mod_w.shape
    rows = cc.shape[0]
    tn = n // 4
    return pl.pallas_call(
        _mod_kernel,
        grid=(depth, n // tn),
        in_specs=[
            pl.BlockSpec((rows, d), lambda i, j: (0, 0)),
            pl.BlockSpec((1, d, tn), lambda i, j: (i, 0, j)),
            pl.BlockSpec((1, 1, tn), lambda i, j: (i, 0, j)),
        ],
        out_specs=pl.BlockSpec((1, rows, tn), lambda i, j: (i, 0, j)),
        out_shape=jax.ShapeDtypeStruct((depth, rows, n), F32),
        compiler_params=_cparams(2),
        name="modulation",
    )(cc, mod_w, mod_b.reshape(depth, 1, n))


def _mod_spec(tiles_per_batch, d):
    return pl.BlockSpec((1, 1, d), lambda i: (i // tiles_per_batch, 0, 0))


FF_CHUNK = 1024


def _tail_math(a, x, wo_ref, bo_ref, ga_ref, g_ref, sh_ref, sc_ref, gm_ref, w1_ref, w2_ref, fg_ref,
               final_norm):
    y = jnp.dot(a.astype(BF16), wo_ref[...], preferred_element_type=F32) + bo_ref[...]
    x = x + ga_ref[0] * y
    h = _norm_mod(x, g_ref[...], sh_ref[0], sc_ref[0]).astype(BF16)
    acc = jnp.zeros(x.shape, F32)
    for c in range(w1_ref.shape[1] // FF_CHUNK):
        u = jnp.dot(h, w1_ref[:, c * FF_CHUNK:(c + 1) * FF_CHUNK], preferred_element_type=F32)
        u = jnp.square(jnp.maximum(u, 0.0)).astype(BF16)
        acc = acc + jnp.dot(u, w2_ref[c * FF_CHUNK:(c + 1) * FF_CHUNK, :],
                            preferred_element_type=F32)
    out = x + gm_ref[0] * acc
    if final_norm:
        out = _rms(out, fg_ref[...])
    return out


def _block_tail_kernel(a_ref, x_ref, *refs, final_norm):
    o_ref = refs[-1]
    o_ref[...] = _tail_math(a_ref[...], x_ref[...], *refs[:-1], final_norm=final_norm)


def _block_tail(a, w_o, b_o, gate_a, x, g, sh, sc, gate_m, w1, w2, final_g, *, final_norm, tm):
    t, d = x.shape
    nb = sh.shape[0]
    tpb = t // nb // tm
    row = lambda n: pl.BlockSpec((tm, n), lambda i: (i, 0))
    return pl.pallas_call(
        functools.partial(_block_tail_kernel, final_norm=final_norm),
        grid=(t // tm,),
        in_specs=[
            row(a.shape[1]), row(d),
            _resident(w_o.shape), _resident((1, d)), _mod_spec(tpb, d),
            _resident((1, d)), _mod_spec(tpb, d), _mod_spec(tpb, d), _mod_spec(tpb, d),
            _resident(w1.shape), _resident(w2.shape), _resident((1, d)),
        ],
        out_specs=row(d),
        out_shape=jax.ShapeDtypeStruct((t, d), F32),
        compiler_params=_cparams(1),
        name="block_tail",
    )(a, x, w_o, b_o, gate_a, g, sh, sc, gate_m, w1, w2, final_g)


def _dft_mats():
    cg = 256
    j = np.arange(cg)
    th = 2.0 * np.pi * np.outer(j, j) / cg
    wc = np.concatenate([np.cos(th), -np.sin(th)], axis=1) / 16.0
    ctx_seq = np.concatenate([np.cos(th), np.sin(th)], axis=1) / 16.0
    r = np.arange(FFT_RADIX)
    ph = 2.0 * np.pi * np.outer(r, r) / FFT_RADIX
    dr, di = np.cos(ph) / 8.0, -np.sin(ph) / 8.0
    m1 = np.block([[dr, -di], [di, dr]])
    m2 = np.concatenate([dr, -di], axis=1)
    tw = 2.0 * np.pi * np.outer(r, r) / (FFT_RADIX * FFT_RADIX)
    tr = np.repeat(np.cos(tw)[:, :, None], LANES, axis=2)
    ti = np.repeat(-np.sin(tw)[:, :, None], LANES, axis=2)
    as_bf = lambda a: jnp.asarray(a, F32).astype(BF16)
    return dict(wc=as_bf(wc), ctx_seq=as_bf(ctx_seq), m1=as_bf(m1), m2=as_bf(m2),
                tr=jnp.asarray(tr, F32), ti=jnp.asarray(ti, F32))


def _chan_dft(h, wc_ref):
    cg = wc_ref.shape[0]
    res, ims = [], []
    for grp in range(h.shape[1] // cg):
        z = jnp.dot(h[:, grp * cg:(grp + 1) * cg], wc_ref[...], preferred_element_type=F32)
        res.append(z[:, :cg])
        ims.append(z[:, cg:])
    return jnp.concatenate(res, axis=1), jnp.concatenate(ims, axis=1)


def _pack_complex(re, im):
    bits = lambda v: lax.bitcast_convert_type(v.astype(BF16).astype(F32), jnp.uint32)
    return (bits(re) >> 16) | bits(im)


def _unpack_complex(w):
    re = lax.bitcast_convert_type(w << 16, F32)
    im = lax.bitcast_convert_type(w & jnp.uint32(0xFFFF0000), F32)
    return jnp.concatenate([re, im], axis=0).astype(BF16)


FFT_GROUP = 8


def _digit_spec(d):
    return pl.BlockSpec((None, FFT_RADIX, FFT_GROUP, d), lambda b, j: (b, 0, j, 0))


def _fourier_front_kernel(x_ref, g_ref, sh_ref, sc_ref, wc_ref, tr_ref, ti_ref, w1_ref, a_ref,
                          zr_ref, zi_ref):
    r, kg = FFT_RADIX, FFT_GROUP
    d = x_ref.shape[-1]
    reps = d // LANES
    x = x_ref[...].reshape(r * kg, d)
    h = _norm_mod(x, g_ref[...], sh_ref[0], sc_ref[0]).astype(BF16)
    zr, zi = _chan_dft(h, wc_ref)
    for ct in range(reps):
        zr_ref[ct] = zr[:, ct * LANES:(ct + 1) * LANES]
        zi_ref[ct] = zi[:, ct * LANES:(ct + 1) * LANES]
    for j in range(kg):
        rows = pl.ds(j, r, stride=kg)
        pick = lambda ref: jnp.concatenate([ref[ct, rows, :] for ct in range(reps)], axis=1)
        z = jnp.concatenate([pick(zr_ref), pick(zi_ref)], axis=0).astype(BF16)
        a = jnp.dot(w1_ref[...], z, preferred_element_type=F32)
        ar, ai = a[:r], a[r:]
        tr = jnp.tile(tr_ref[j], (1, reps))
        ti = jnp.tile(ti_ref[j], (1, reps))
        a_ref[0, j] = _pack_complex(ar * tr - ai * ti, ar * ti + ai * tr)


def _fourier_front(x, g, sh, sc, mats, *, batch):
    t, d = x.shape
    r, kg = FFT_RADIX, FFT_GROUP
    const = lambda a: pl.BlockSpec(a.shape, lambda b, j: (0,) * a.ndim)
    mod = pl.BlockSpec((1, 1, d), lambda b, j: (b, 0, 0))
    tw = pl.BlockSpec((kg, r, LANES), lambda b, j: (j, 0, 0))
    return pl.pallas_call(
        _fourier_front_kernel,
        grid=(batch, r // kg),
        in_specs=[_digit_spec(d), const(g), mod, mod, const(mats["wc"]), tw, tw, const(mats["m1"])],
        out_specs=pl.BlockSpec((1, kg, r, d), lambda b, j: (b, j, 0, 0)),
        out_shape=jax.ShapeDtypeStruct((batch, r, r, d), jnp.uint32),
        scratch_shapes=[pltpu.VMEM((d // LANES, r * kg, LANES), F32)] * 2,
        compiler_params=_cparams(2),
        name="fourier_front",
    )(x.reshape(batch, r, r, d), g, sh, sc, mats["wc"], mats["tr"], mats["ti"], mats["m1"])


def _fourier_tail_kernel(a_ref, m2_ref, x_ref, *refs, final_norm):
    r, kg = FFT_RADIX, FFT_GROUP
    d = a_ref.shape[-1]
    reps = d // LANES
    o_ref, aw_ref, f_ref = refs[-3:]
    a = a_ref[...].reshape(r * kg, d)
    for ct in range(reps):
        aw_ref[ct] = a[:, ct * LANES:(ct + 1) * LANES]
    for j in range(kg):
        rows = pl.ds(j, r, stride=kg)
        w = jnp.concatenate([aw_ref[ct, rows, :] for ct in range(reps)], axis=1)
        f = jnp.dot(m2_ref[...], _unpack_complex(w), preferred_element_type=F32)
        for ct in range(reps):
            f_ref[ct, rows, :] = f[:, ct * LANES:(ct + 1) * LANES]
    f = jnp.concatenate([f_ref[ct] for ct in range(reps)], axis=1)
    out = _tail_math(f, x_ref[...].reshape(r * kg, d), *refs[:-3], final_norm=final_norm)
    o_ref[...] = out.reshape(r, kg, d)


def _fourier_tail(a, mats, w_o, b_o, gate_a, x, g, sh, sc, gate_m, w1, w2, final_g, *, batch,
                  final_norm):
    t, d = x.shape
    r, kg = FFT_RADIX, FFT_GROUP
    const = lambda v: pl.BlockSpec(v.shape, lambda b, j: (0,) * v.ndim, pipeline_mode=pl.Buffered(1))
    mod = pl.BlockSpec((1, 1, d), lambda b, j: (b, 0, 0))
    out = pl.pallas_call(
        functools.partial(_fourier_tail_kernel, final_norm=final_norm),
        grid=(batch, r // kg),
        in_specs=[_digit_spec(d), const(mats["m2"]), _digit_spec(d),
                  const(w_o), const(b_o), mod, const(g), mod, mod, mod,
                  const(w1), const(w2), const(final_g)],
        out_specs=_digit_spec(d),
        out_shape=jax.ShapeDtypeStruct((batch, r, r, d), F32),
        scratch_shapes=[pltpu.VMEM((d // LANES, r * kg, LANES), jnp.uint32),
                        pltpu.VMEM((d // LANES, r * kg, LANES), F32)],
        compiler_params=_cparams(2),
        name="fourier_tail",
    )(a, mats["m2"], x.reshape(batch, r, r, d), w_o, b_o, gate_a, g, sh, sc, gate_m, w1, w2, final_g)
    return out.reshape(t, d)


def _fourier_ctx_kernel(x_ref, g_ref, sh_ref, sc_ref, wc_ref, seq_ref, f_ref):
    h = _norm_mod(x_ref[...], g_ref[...], sh_ref[0], sc_ref[0]).astype(BF16)
    zr, zi = _chan_dft(h, wc_ref)
    z = jnp.concatenate([zr, zi], axis=0).astype(BF16)
    f_ref[...] = jnp.dot(seq_ref[...], z, preferred_element_type=F32)


def _fourier_ctx(x, g, sh, sc, mats, *, ctx_len):
    t, d = x.shape
    n_tiles = t // ctx_len
    return pl.pallas_call(
        _fourier_ctx_kernel,
        grid=(n_tiles,),
        in_specs=[
            pl.BlockSpec((ctx_len, d), lambda i: (i, 0)),
            _resident((1, d)), _mod_spec(n_tiles, d), _mod_spec(n_tiles, d),
            _resident(mats["wc"].shape), _resident(mats["ctx_seq"].shape),
        ],
        out_specs=pl.BlockSpec((ctx_len, d), lambda i: (i, 0)),
        out_shape=jax.ShapeDtypeStruct((t, d), F32),
        compiler_params=_cparams(1),
        name="fourier_ctx",
    )(x, g, sh, sc, mats["wc"], mats["ctx_seq"])


def _rope_tables(rows, dim, lane_of_dim):
    n_freq = dim // 4
    inv_freq = ROPE_BASE ** (-jnp.arange(n_freq, dtype=F32) / n_freq)
    t = jnp.arange(rows * GRID_W)
    row = (t // GRID_W).astype(F32)
    col = (t % GRID_W).astype(F32)
    ang = jnp.stack([row[:, None] * inv_freq, col[:, None] * inv_freq], axis=1)
    cos, sin = jnp.cos(ang), jnp.sin(ang)
    dims = np.array([lane_of_dim(l) for l in range(LANES)])
    live = dims >= 0
    dd = np.where(live, dims, 0)
    axis, pair, freq = dd // (2 * n_freq), (dd % (2 * n_freq)) // n_freq, dd % n_freq
    sign = np.where(pair == 0, -1.0, 1.0).astype(np.float32)
    cos_t = jnp.where(live[None, :], cos[:, axis, freq], 1.0)
    sin_t = jnp.where(live[None, :], sin[:, axis, freq] * sign[None, :], 0.0)
    return cos_t.astype(F32), sin_t.astype(F32)


def _split_sub_heads(q):
    lane = lax.broadcasted_iota(jnp.int32, q.shape, 1)
    first = (lane & (DIFF_HEAD_DIM // 2)) == 0
    zero = jnp.zeros_like(q)
    return [jnp.where(first, q, zero), jnp.where(first, zero, q)]


def _scores(q, k_refs):
    nt = (((1,), (1,)), ((), ()))
    return [lax.dot_general(q, k[0], nt, preferred_element_type=F32) for k in k_refs]


def _attend(q, k_refs, vaug_refs):
    parts = _scores(q, k_refs)
    m = functools.reduce(jnp.maximum, [jnp.max(p, axis=-1, keepdims=True) for p in parts])
    acc = None
    for p, v in zip(parts, vaug_refs):
        e = jnp.exp2(p - m).astype(BF16)
        part = jnp.dot(e, v[0], preferred_element_type=F32)
        acc = part if acc is None else acc + part
    return acc[:, :LANES] / acc[:, LANES:]


def _pipe_attn_kernel(*refs, mode, n_seg, lam_init):
    n_par = 5 if mode == "diff" else 0
    par = refs[:n_par]
    q_ref = refs[n_par]
    k_refs = refs[n_par + 1:n_par + 1 + n_seg]
    v_refs = refs[n_par + 1 + n_seg:n_par + 1 + 2 * n_seg]
    o_ref, s_a, s_b, m_a, m_b = refs[n_par + 1 + 2 * n_seg:]
    t = pl.program_id(0)
    nt = (((1,), (1,)), ((), ()))

    def scores(s_w, m_w):
        q = q_ref[0]
        if mode == "diff":
            qs = _split_sub_heads(q)
        else:
            qs = [q[:, :LANES], q[:, LANES:]]
        for c in range(2):
            m, off = None, 0
            for k in k_refs:
                kk = k[0] if mode == "diff" else k[0, :, c * LANES:(c + 1) * LANES]
                p = lax.dot_general(qs[c], kk, nt, preferred_element_type=F32)
                s_w[c, :, off:off + p.shape[1]] = p
                off += p.shape[1]
                pm = jnp.max(p, axis=-1, keepdims=True)
                m = pm if m is None else jnp.maximum(m, pm)
            m_w[c] = jnp.broadcast_to(m, m_w.shape[1:])

    def values(s_r, m_r):
        outs = []
        for c in range(2):
            m = m_r[c][:, :1]
            acc, off = None, 0
            for v in v_refs:
                w = v.shape[1]
                e = jnp.exp2(s_r[c, :, off:off + w] - m).astype(BF16)
                part = jnp.dot(e, v[0], preferred_element_type=F32)
                acc = part if acc is None else acc + part
                off += w
            outs.append(acc[:, :LANES] / acc[:, LANES:])
        if mode == "diff":
            lq1, lk1, lq2, lk2, sg = par
            lam = (jnp.exp(jnp.sum(lq1[...] * lk1[...], axis=-1, keepdims=True))
                   - jnp.exp(jnp.sum(lq2[...] * lk2[...], axis=-1, keepdims=True)) + lam_init)
            o = outs[0] - lam * outs[1]
            o_ref[0] = (_rms(o, sg[...]) * (1.0 - lam_init)).astype(BF16)
        else:
            lane = lax.broadcasted_iota(jnp.int32, outs[0].shape, 1)
            o_ref[0] = jnp.where(lane < MLA_V_DIM, outs[0], outs[1]).astype(BF16)

    @pl.when(t == 0)
    def _():
        s_b[...] = jnp.zeros(s_b.shape, F32)
        m_b[...] = jnp.zeros(m_b.shape, F32)

    @pl.when(t % 2 == 0)
    def _():
        scores(s_a, m_a)
        values(s_b, m_b)

    @pl.when(t % 2 == 1)
    def _():
        scores(s_b, m_b)
        values(s_a, m_a)


def _pipe_attn(params, q_arr, k_arrs, v_arrs, *, mode, units, q_width, k_block, tq, lam_init=0.0):
    b, sq, _ = q_arr.shape
    nq = sq // tq
    n_items = b * units * nq
    n_seg = len(k_arrs)
    sk = sum(a.shape[1] for a in k_arrs)

    def item(t, lag):
        tt = jnp.clip(t - lag, 0, n_items - 1)
        return tt // (units * nq), (tt // nq) % units, tt % nq

    def q_map(t):
        bb, u, i = item(t, 0)
        return bb, i, u

    def k_map(t):
        bb, u, _ = item(t, 0)
        return bb, 0, k_block + u

    def v_map(t):
        bb, u, _ = item(t, 1)
        return bb, 0, k_block + u

    def o_map(t):
        bb, u, i = item(t, 1)
        return bb, i, u

    in_specs = [pl.BlockSpec(p.shape, lambda t: (0, 0)) for p in params]
    in_specs += [pl.BlockSpec((1, tq, q_width), q_map)]
    in_specs += [pl.BlockSpec((1, a.shape[1], q_width), k_map) for a in k_arrs]
    in_specs += [pl.BlockSpec((1, a.shape[1], 2 * LANES), v_map) for a in v_arrs]
    return pl.pallas_call(
        functools.partial(_pipe_attn_kernel, mode=mode, n_seg=n_seg, lam_init=lam_init),
        grid=(n_items + 1,),
        in_specs=in_specs,
        out_specs=pl.BlockSpec((1, tq, LANES), o_map),
        out_shape=jax.ShapeDtypeStruct((b, sq, units * LANES), BF16),
        scratch_shapes=[pltpu.VMEM((2, tq, sk), F32), pltpu.VMEM((2, tq, sk), F32),
                        pltpu.VMEM((2, tq, LANES), F32), pltpu.VMEM((2, tq, LANES), F32)],
        compiler_params=_cparams(1),
        name=mode + "_pipe_attn",
    )(*params, q_arr, *k_arrs, *v_arrs)


def _diff_qkv_kernel(x_ref, g_ref, sh_ref, sc_ref, w_ref, cos_ref, sin_ref, o_ref, *, rope, qscale):
    h = _norm_mod(x_ref[...], g_ref[...], sh_ref[0], sc_ref[0]).astype(BF16)
    d = x_ref.shape[1]
    acc = jnp.dot(h, w_ref[...], preferred_element_type=F32)
    heads = d // LANES
    ones = jnp.ones((x_ref.shape[0], LANES), BF16)
    for tile in range(2 * heads):
        a = acc[:, tile * LANES:(tile + 1) * LANES]
        if rope:
            a = _rope_tile(a, cos_ref[...], sin_ref[...])
        if tile < heads:
            a = a * qscale
        o_ref[:, tile * LANES:(tile + 1) * LANES] = a.astype(BF16)
    for hd in range(heads):
        src = (2 * heads + hd) * LANES
        dst = (2 * heads + 2 * hd) * LANES
        o_ref[:, dst:dst + LANES] = acc[:, src:src + LANES].astype(BF16)
        o_ref[:, dst + LANES:dst + 2 * LANES] = ones


def _diff_qkv(x, g, sh, sc, w, cos, sin, *, rope, tm):
    t, d = x.shape
    nb = sh.shape[0]
    tpb = t // nb // tm
    spb = t // nb // tm
    n = w.shape[1] + d
    qscale = DIFF_HEAD_DIM ** -0.5 * LOG2E
    tab = pl.BlockSpec((tm, LANES), (lambda i: (i % spb, 0)) if rope else (lambda i: (0, 0)))
    return pl.pallas_call(
        functools.partial(_diff_qkv_kernel, rope=rope, qscale=qscale),
        grid=(t // tm,),
        in_specs=[
            pl.BlockSpec((tm, d), lambda i: (i, 0)),
            _resident((1, d)), _mod_spec(tpb, d), _mod_spec(tpb, d),
            _resident(w.shape), tab, tab,
        ],
        out_specs=pl.BlockSpec((tm, n), lambda i: (i, 0)),
        out_shape=jax.ShapeDtypeStruct((t, n), BF16),
        compiler_params=_cparams(1),
        name="diff_qkv",
    )(x, g, sh, sc, w, cos, sin)


def _diff_attn_kernel(lq1_ref, lk1_ref, lq2_ref, lk2_ref, sg_ref, q_ref, *refs, n_seg, lam_init):
    k_refs, v_refs, o_ref = refs[:n_seg], refs[n_seg:2 * n_seg], refs[2 * n_seg]
    lam = (jnp.exp(jnp.sum(lq1_ref[...] * lk1_ref[...], axis=-1, keepdims=True))
           - jnp.exp(jnp.sum(lq2_ref[...] * lk2_ref[...], axis=-1, keepdims=True)) + lam_init)
    for hd in range(q_ref.shape[2] // LANES):
        head = slice(hd * LANES, (hd + 1) * LANES)
        ks = [k.at[:, :, head] for k in k_refs]
        vs = [v.at[:, :, 2 * hd * LANES:2 * (hd + 1) * LANES] for v in v_refs]
        q1, q2 = _split_sub_heads(q_ref[0, :, head])
        o = _attend(q1, ks, vs) - lam * _attend(q2, ks, vs)
        o_ref[0, :, head] = (_rms(o, sg_ref[...]) * (1.0 - lam_init)).astype(BF16)


def _diff_attn(lams, subln_g, q_arr, kv_arrs, *, lam_init):
    b, sq, d4 = q_arr.shape
    d = d4 // 4
    n_seg = len(kv_arrs)
    in_specs = [pl.BlockSpec(l.shape, lambda bb: (0, 0)) for l in lams]
    in_specs += [pl.BlockSpec((1, DIFF_V_DIM), lambda bb: (0, 0)),
                 pl.BlockSpec((1, sq, d), lambda bb: (bb, 0, 0))]
    in_specs += [pl.BlockSpec((1, a.shape[1], d), lambda bb: (bb, 0, 1)) for a in kv_arrs]
    in_specs += [pl.BlockSpec((1, a.shape[1], 2 * d), lambda bb: (bb, 0, 1)) for a in kv_arrs]
    return pl.pallas_call(
        functools.partial(_diff_attn_kernel, n_seg=n_seg, lam_init=lam_init),
        grid=(b,),
        in_specs=in_specs,
        out_specs=pl.BlockSpec((1, sq, d), lambda bb: (bb, 0, 0)),
        out_shape=jax.ShapeDtypeStruct((b, sq, d), BF16),
        compiler_params=_cparams(1),
        name="diff_attn",
    )(*lams, subln_g, q_arr, *kv_arrs, *kv_arrs)


def _mla_down_kernel(x_ref, g_ref, sh_ref, sc_ref, w_ref, qg_ref, kvg_ref, cos_ref, sin_ref,
                     cq_ref, ckv_ref, kr_ref, *, rope):
    h = _norm_mod(x_ref[...], g_ref[...], sh_ref[0], sc_ref[0]).astype(BF16)
    acc = jnp.dot(h, w_ref[...], preferred_element_type=F32)
    cq_ref[...] = _rms(acc[:, :MLA_Q_RANK], qg_ref[...]).astype(BF16)
    ckv_ref[...] = _rms(acc[:, MLA_Q_RANK:MLA_Q_RANK + MLA_KV_RANK], kvg_ref[...]).astype(BF16)
    kr = acc[:, MLA_Q_RANK + MLA_KV_RANK:]
    if rope:
        kr = _rope_tile(kr, cos_ref[...], sin_ref[...])
    kr_ref[...] = kr.astype(BF16)


def _mla_up_kernel(cq_ref, ckv_ref, kr_ref, wq_ref, wk_ref, wv_ref, cos_ref, sin_ref,
                   q_ref, k_ref, v_ref, *, rope, qscale):
    q = jnp.dot(cq_ref[...], wq_ref[...], preferred_element_type=F32)
    for hd in range(MLA_HEADS):
        a = q[:, hd * LANES:(hd + 1) * LANES]
        if rope:
            a = _rope_tile(a, cos_ref[...], sin_ref[...])
        q_ref[:, hd * LANES:(hd + 1) * LANES] = (a * qscale).astype(BF16)
    k = jnp.dot(ckv_ref[...], wk_ref[...], preferred_element_type=F32)
    kr = kr_ref[...].astype(F32)
    for hd in range(MLA_HEADS):
        k_ref[:, hd * LANES:(hd + 1) * LANES] = (k[:, hd * LANES:(hd + 1) * LANES] + kr).astype(BF16)
    v = jnp.dot(ckv_ref[...], wv_ref[...], preferred_element_type=F32)
    ones = jnp.ones((v.shape[0], LANES), BF16)
    for pair in range(MLA_HEADS // 2):
        v_ref[:, 2 * pair * LANES:(2 * pair + 1) * LANES] = v[:, pair * LANES:(pair + 1) * LANES].astype(BF16)
        v_ref[:, (2 * pair + 1) * LANES:(2 * pair + 2) * LANES] = ones


def _mla_project(x, g, sh, sc, w_down, qg, kvg, wq, wk, wv, cos, sin, *, rope, tm):
    t, d = x.shape
    nb = sh.shape[0]
    tpb = t // nb // tm
    tab = pl.BlockSpec((tm, LANES), (lambda i: (i % tpb, 0)) if rope else (lambda i: (0, 0)))
    row = lambda n: pl.BlockSpec((tm, n), lambda i: (i, 0))
    cq, ckv, kr = pl.pallas_call(
        functools.partial(_mla_down_kernel, rope=rope),
        grid=(t // tm,),
        in_specs=[row(d), _resident((1, d)), _mod_spec(tpb, d), _mod_spec(tpb, d),
                  _resident(w_down.shape), _resident(qg.shape), _resident(kvg.shape), tab, tab],
        out_specs=[row(MLA_Q_RANK), row(MLA_KV_RANK), row(LANES)],
        out_shape=[jax.ShapeDtypeStruct((t, MLA_Q_RANK), BF16),
                   jax.ShapeDtypeStruct((t, MLA_KV_RANK), BF16),
                   jax.ShapeDtypeStruct((t, LANES), BF16)],
        compiler_params=_cparams(1),
        name="mla_down",
    )(x, g, sh, sc, w_down, qg, kvg, cos, sin)
    qscale = (MLA_NOPE_DIM + MLA_ROPE_DIM) ** -0.5 * LOG2E
    nq, nv = wq.shape[1], 2 * wv.shape[1]
    return pl.pallas_call(
        functools.partial(_mla_up_kernel, rope=rope, qscale=qscale),
        grid=(t // tm,),
        in_specs=[row(MLA_Q_RANK), row(MLA_KV_RANK), row(LANES),
                  _resident(wq.shape), _resident(wk.shape), _resident(wv.shape), tab, tab],
        out_specs=[row(nq), row(nq), row(nv)],
        out_shape=[jax.ShapeDtypeStruct((t, nq), BF16), jax.ShapeDtypeStruct((t, nq), BF16),
                   jax.ShapeDtypeStruct((t, nv), BF16)],
        compiler_params=_cparams(1),
        name="mla_up",
    )(cq, ckv, kr, wq, wk, wv, cos, sin)


def _mla_attn_kernel(q_ref, *refs, n_seg):
    k_refs, v_refs, o_ref = refs[:n_seg], refs[n_seg:2 * n_seg], refs[2 * n_seg]
    outs = []
    for hd in range(2):
        q = q_ref[0, :, hd * LANES:(hd + 1) * LANES]
        ks = [k.at[:, :, hd * LANES:(hd + 1) * LANES] for k in k_refs]
        outs.append(_attend(q, ks, v_refs))
    lane = lax.broadcasted_iota(jnp.int32, outs[0].shape, 1)
    o_ref[0] = jnp.where(lane < MLA_V_DIM, outs[0], outs[1]).astype(BF16)


def _mla_attn(q, k_arrs, v_arrs, *, tq):
    b, sq, nq = q.shape
    pairs = nq // (2 * LANES)
    n_seg = len(k_arrs)
    in_specs = [pl.BlockSpec((1, tq, 2 * LANES), lambda bb, h, i: (bb, i, h))]
    in_specs += [pl.BlockSpec((1, a.shape[1], 2 * LANES), lambda bb, h, i: (bb, 0, h)) for a in k_arrs]
    in_specs += [pl.BlockSpec((1, a.shape[1], 2 * LANES), lambda bb, h, i: (bb, 0, h)) for a in v_arrs]
    return pl.pallas_call(
        functools.partial(_mla_attn_kernel, n_seg=n_seg),
        grid=(b, pairs, sq // tq),
        in_specs=in_specs,
        out_specs=pl.BlockSpec((1, tq, LANES), lambda bb, h, i: (bb, i, h)),
        out_shape=jax.ShapeDtypeStruct((b, sq, pairs * LANES), BF16),
        compiler_params=_cparams(3),
        name="mla_attn",
    )(q, *k_arrs, *v_arrs)


def _take_cols(w, src):
    src = np.asarray(src)
    return jnp.where(src >= 0, jnp.take(w, np.maximum(src, 0), axis=-1), 0).astype(w.dtype)


def _mla_weights(w_down, w_uq, w_ukv):
    hq = MLA_NOPE_DIM + MLA_ROPE_DIM
    lo = MLA_Q_RANK + MLA_KV_RANK
    kinds = [_mla_lane_dim(l) for l in range(LANES)]
    q_src = np.array([{"nope": dim, "rope": MLA_NOPE_DIM + dim, "pad": -1}[k] for k, dim in kinds])
    k_src = np.array([dim if k == "nope" else -1 for k, dim in kinds])
    r_src = np.array([lo + dim if k == "rope" else -1 for k, dim in kinds])
    wd = jnp.concatenate([w_down[:, :lo], _take_cols(w_down, r_src)], axis=1)
    wq = _take_cols(w_uq.reshape(MLA_Q_RANK, MLA_HEADS, hq), q_src)
    wkv = w_ukv.reshape(MLA_KV_RANK, MLA_HEADS, MLA_NOPE_DIM + MLA_V_DIM)
    wk = _take_cols(wkv, k_src)
    wv = wkv[:, :, MLA_NOPE_DIM:].reshape(MLA_KV_RANK, MLA_HEADS * MLA_V_DIM)
    flat = lambda w: w.reshape(w.shape[0], MLA_HEADS * LANES)
    return wd.astype(BF16), flat(wq).astype(BF16), flat(wk).astype(BF16), wv.astype(BF16)


def _diff_weights(w_qkv):
    d = w_qkv.shape[0]
    src = np.array([sub * DIFF_HEAD_DIM + dim for sub, dim in map(_diff_lane_dim, range(LANES))])
    qk = w_qkv[:, :2 * d].reshape(d, 2 * d // LANES, LANES)
    qk = _take_cols(qk, src).reshape(d, 2 * d)
    return jnp.concatenate([qk, w_qkv[:, 2 * d:]], axis=1).astype(BF16)


def kernel(x, c, ctx, c_ctx, mod_w, mod_b, norm_mix_g, norm_mlp_g, final_g, mlp_w1, mlp_w2, fourier_w, fourier_b, diff_w_qkv, diff_lambda_q1, diff_lambda_k1, diff_lambda_q2, diff_lambda_k2, diff_subln_g, diff_w_o, mla_w_down, mla_q_norm_g, mla_kv_norm_g, mla_w_uq, mla_w_ukv, mla_w_o):
    batch, seq, d = x.shape
    ctx_len = ctx.shape[1]
    depth = mod_w.shape[0]
    rows = seq // GRID_W
    assert seq == FFT_RADIX * FFT_RADIX and d == FOURIER_GROUPS * 256 and ctx_len == 256
    tm = 512

    cc_rows = -(-(batch + 1) // 8) * 8
    cc = jnp.zeros((cc_rows, d), F32).at[:batch].set(c).at[batch].set(c_ctx)
    mods = _modulation(cc, mod_w, mod_b).reshape(depth, cc_rows, 6, d)
    mats = _dft_mats()
    zero_bias = jnp.zeros((1, d), F32)
    final_g2 = final_g.reshape(1, d)

    cos_d, sin_d = _rope_tables(rows, DIFF_HEAD_DIM, lambda l: _diff_lane_dim(l)[1])
    lane_mla = lambda l: _mla_lane_dim(l)[1] if _mla_lane_dim(l)[0] == "rope" else -1
    cos_m, sin_m = _rope_tables(rows, MLA_ROPE_DIM, lane_mla)

    x_lat = x.reshape(batch * seq, d)
    x_ctx = ctx.reshape(batch * ctx_len, d)
    for i in range(depth):
        kind, j = i % N_MIXERS, i // N_MIXERS
        update_ctx = i < depth - 1
        ctx_feeds_mixer = update_ctx or kind != 0
        lat = [mods[i, :batch, k].reshape(batch, 1, d) for k in range(6)]
        cm = [mods[i, batch:batch + 1, k].reshape(1, 1, d) for k in range(6)]
        g_mix = norm_mix_g[i].reshape(1, d)
        g_mlp = norm_mlp_g[i].reshape(1, d)
        a_ctx = None
        if kind == 0:
            w_o = fourier_w[j].astype(BF16)
            b_o = fourier_b[j].reshape(1, d)
            a_lat = _fourier_front(x_lat, g_mix, lat[0], lat[1], mats, batch=batch)
            if update_ctx:
                a_ctx = _fourier_ctx(x_ctx, g_mix, cm[0], cm[1], mats, ctx_len=ctx_len)
        elif kind == 1:
            heads = d // DIFF_V_DIM
            w_qkv = _diff_weights(diff_w_qkv[j])
            w_o = diff_w_o[j].astype(BF16)
            b_o = zero_bias
            lam_init = 0.8 - 0.6 * math.exp(-0.3 * i)
            lams = [v[j].reshape(1, DIFF_HEAD_DIM) for v in
                    (diff_lambda_q1, diff_lambda_k1, diff_lambda_q2, diff_lambda_k2)]
            sg = diff_subln_g[j].reshape(1, DIFF_V_DIM)
            qkv_l = _diff_qkv(x_lat, g_mix, lat[0], lat[1], w_qkv, cos_d, sin_d, rope=True, tm=tm)
            qkv_l = qkv_l.reshape(batch, seq, 4 * d)
            qkv_c = _diff_qkv(x_ctx, g_mix, cm[0], cm[1], w_qkv, cos_d, sin_d, rope=False, tm=ctx_len)
            qkv_c = qkv_c.reshape(batch, ctx_len, 4 * d)
            o_l = _pipe_attn(lams + [sg], qkv_l, [qkv_c, qkv_l], [qkv_c, qkv_l], mode="diff",
                             units=heads, q_width=LANES, k_block=heads, tq=512, lam_init=lam_init)
            a_lat = o_l.reshape(batch * seq, d)
            if update_ctx:
                o_c = _diff_attn(lams, sg, qkv_c, [qkv_c], lam_init=lam_init)
                a_ctx = o_c.reshape(batch * ctx_len, d)
        else:
            wd, wq, wk, wv = _mla_weights(mla_w_down[j], mla_w_uq[j], mla_w_ukv[j])
            w_o = mla_w_o[j].astype(BF16)
            b_o = zero_bias
            qg = mla_q_norm_g[j].reshape(1, MLA_Q_RANK)
            kvg = mla_kv_norm_g[j].reshape(1, MLA_KV_RANK)
            q_l, k_l, v_l = _mla_project(x_lat, g_mix, lat[0], lat[1], wd, qg, kvg, wq, wk, wv,
                                         cos_m, sin_m, rope=True, tm=tm)
            q_c, k_c, v_c = _mla_project(x_ctx, g_mix, cm[0], cm[1], wd, qg, kvg, wq, wk, wv,
                                         cos_m, sin_m, rope=False, tm=ctx_len)
            r3 = lambda a, s: a.reshape(batch, s, a.shape[1])
            o_l = _pipe_attn([], r3(q_l, seq), [r3(k_c, ctx_len), r3(k_l, seq)],
                             [r3(v_c, ctx_len), r3(v_l, seq)], mode="mla", units=MLA_HEADS // 2,
                             q_width=2 * LANES, k_block=0, tq=512)
            a_lat = o_l.reshape(batch * seq, d)
            if update_ctx:
                o_c = _mla_attn(r3(q_c, ctx_len), [r3(k_c, ctx_len)], [r3(v_c, ctx_len)], tq=ctx_len)
                a_ctx = o_c.reshape(batch * ctx_len, d)
        w1 = mlp_w1[i].astype(BF16)
        w2 = mlp_w2[i].astype(BF16)
        last = i == depth - 1
        if kind == 0:
            x_lat = _fourier_tail(a_lat, mats, w_o, b_o, lat[2], x_lat, g_mlp, lat[3], lat[4], lat[5],
                                  w1, w2, final_g2, batch=batch, final_norm=last)
        else:
            x_lat = _block_tail(a_lat, w_o, b_o, lat[2], x_lat, g_mlp, lat[3], lat[4], lat[5], w1, w2,
                                final_g2, final_norm=last, tm=tm)
        if update_ctx:
            x_ctx = _block_tail(a_ctx, w_o, b_o, cm[2], x_ctx, g_mlp, cm[3], cm[4], cm[5], w1, w2,
                                final_g2, final_norm=False, tm=ctx_len)
    return x_lat.reshape(batch, seq, d)
```

```python
import functools
import math

import numpy as np
import jax
import jax.numpy as jnp
from jax import lax
from jax.experimental import pallas as pl
from jax.experimental.pallas import tpu as pltpu

F32 = jnp.float32
BF16 = jnp.bfloat16

GRID_W = 64
RMS_EPS = 1e-6
ROPE_BASE = 10000.0
FOURIER_GROUPS = 4
DIFF_HEAD_DIM = 64
DIFF_V_DIM = 128
MLA_HEADS = 16
MLA_NOPE_DIM = 64
MLA_ROPE_DIM = 32
MLA_V_DIM = 64
MLA_Q_RANK = 256
MLA_KV_RANK = 128
N_MIXERS = 3

LANES = 128
FFT_RADIX = 64
VMEM_LIMIT = 56 * 1024 * 1024
LOG2E = 1.4426950408889634


def _cparams(n_axes):
    return pltpu.CompilerParams(
        dimension_semantics=("arbitrary",) * n_axes, vmem_limit_bytes=VMEM_LIMIT)


def _resident(shape):
    nd = len(shape)
    return pl.BlockSpec(shape, lambda *_: (0,) * nd, pipeline_mode=pl.Buffered(1))


def _norm_mod(x, g, sh, sc):
    r = lax.rsqrt(jnp.mean(x * x, axis=-1, keepdims=True) + RMS_EPS)
    return ((x * r) * g) * (1.0 + sc) + sh


def _rms(x, g):
    r = lax.rsqrt(jnp.mean(x * x, axis=-1, keepdims=True) + RMS_EPS)
    return (x * r) * g


def _rope_tile(x, cos, sin):
    return x * cos + pltpu.roll(x, LANES // 2, 1) * sin


def _diff_lane_dim(lane):
    pair, sub, axis, freq = lane // 64, (lane % 64) // 32, (lane % 32) // 16, lane % 16
    return sub, axis * 32 + pair * 16 + freq


def _mla_lane_dim(lane):
    half, off = lane // 64, lane % 64
    if off < 16:
        return "rope", (off // 8) * 16 + half * 8 + off % 8
    if half == 0:
        return "nope", off - 16
    return ("nope", 48 + off - 16) if off < 32 else ("pad", 0)


def _mod_kernel(c_ref, w_ref, b_ref, o_ref):
    c = c_ref[...]
    s = (c / (1.0 + jnp.exp(-c))).astype(BF16)
    o_ref[0] = jnp.dot(s, w_ref[0].astype(BF16), preferred_element_type=F32) + b_ref[0]


def _modulation(cc, mod_w, mod_b):
    depth, d, n = mod_w.shape
    rows = cc.shape[0]
    tn = n // 4
    return pl.pallas_call(
        _mod_kernel,
        grid=(depth, n // tn),
        in_specs=[
            pl.BlockSpec((rows, d), lambda i, j: (0, 0)),
            pl.BlockSpec((1, d, tn), lambda i, j: (i, 0, j)),
            pl.BlockSpec((1, 1, tn), lambda i, j: (i, 0, j)),
        ],
        out_specs=pl.BlockSpec((1, rows, tn), lambda i, j: (i, 0, j)),
        out_shape=jax.ShapeDtypeStruct((depth, rows, n), F32),
        compiler_params=_cparams(2),
        name="modulation",
    )(cc, mod_w, mod_b.reshape(depth, 1, n))


def _mod_spec(tiles_per_batch, d):
    return pl.BlockSpec((1, 1, d), lambda i: (i // tiles_per_batch, 0, 0))


FF_CHUNK = 1024


def _tail_math(a, x, wo_ref, bo_ref, ga_ref, g_ref, sh_ref, sc_ref, gm_ref, w1_ref, w2_ref, fg_ref,
               final_norm):
    y = jnp.dot(a.astype(BF16), wo_ref[...], preferred_element_type=F32) + bo_ref[...]
    x = x + ga_ref[0] * y
    h = _norm_mod(x, g_ref[...], sh_ref[0], sc_ref[0]).astype(BF16)
    acc = jnp.zeros(x.shape, F32)
    for c in range(w1_ref.shape[1] // FF_CHUNK):
        u = jnp.dot(h, w1_ref[:, c * FF_CHUNK:(c + 1) * FF_CHUNK], preferred_element_type=F32)
        u = jnp.square(jnp.maximum(u, 0.0)).astype(BF16)
        acc = acc + jnp.dot(u, w2_ref[c * FF_CHUNK:(c + 1) * FF_CHUNK, :],
                            preferred_element_type=F32)
    out = x + gm_ref[0] * acc
    if final_norm:
        out = _rms(out, fg_ref[...])
    return out


def _block_tail_kernel(a_ref, x_ref, *refs, final_norm):
    o_ref = refs[-1]
    o_ref[...] = _tail_math(a_ref[...], x_ref[...], *refs[:-1], final_norm=final_norm)


def _block_tail(a, w_o, b_o, gate_a, x, g, sh, sc, gate_m, w1, w2, final_g, *, final_norm, tm):
    t, d = x.shape
    nb = sh.shape[0]
    tpb = t // nb // tm
    row = lambda n: pl.BlockSpec((tm, n), lambda i: (i, 0))
    return pl.pallas_call(
        functools.partial(_block_tail_kernel, final_norm=final_norm),
        grid=(t // tm,),
        in_specs=[
            row(a.shape[1]), row(d),
            _resident(w_o.shape), _resident((1, d)), _mod_spec(tpb, d),
            _resident((1, d)), _mod_spec(tpb, d), _mod_spec(tpb, d), _mod_spec(tpb, d),
            _resident(w1.shape), _resident(w2.shape), _resident((1, d)),
        ],
        out_specs=row(d),
        out_shape=jax.ShapeDtypeStruct((t, d), F32),
        compiler_params=_cparams(1),
        name="block_tail",
    )(a, x, w_o, b_o, gate_a, g, sh, sc, gate_m, w1, w2, final_g)


def _dft_mats():
    cg = 256
    j = np.arange(cg)
    th = 2.0 * np.pi * np.outer(j, j) / cg
    wc = np.concatenate([np.cos(th), -np.sin(th)], axis=1) / 16.0
    ctx_seq = np.concatenate([np.cos(th), np.sin(th)], axis=1) / 16.0
    r = np.arange(FFT_RADIX)
    ph = 2.0 * np.pi * np.outer(r, r) / FFT_RADIX
    dr, di = np.cos(ph) / 8.0, -np.sin(ph) / 8.0
    m1 = np.block([[dr, -di], [di, dr]])
    m2 = np.concatenate([dr, -di], axis=1)
    tw = 2.0 * np.pi * np.outer(r, r) / (FFT_RADIX * FFT_RADIX)
    tr = np.repeat(np.cos(tw)[:, :, None], LANES, axis=2)
    ti = np.repeat(-np.sin(tw)[:, :, None], LANES, axis=2)
    as_bf = lambda a: jnp.asarray(a, F32).astype(BF16)
    return dict(wc=as_bf(wc), ctx_seq=as_bf(ctx_seq), m1=as_bf(m1), m2=as_bf(m2),
                tr=jnp.asarray(tr, F32), ti=jnp.asarray(ti, F32))


def _chan_dft(h, wc_ref):
    cg = wc_ref.shape[0]
    res, ims = [], []
    for grp in range(h.shape[1] // cg):
        z = jnp.dot(h[:, grp * cg:(grp + 1) * cg], wc_ref[...], preferred_element_type=F32)
        res.append(z[:, :cg])
        ims.append(z[:, cg:])
    return jnp.concatenate(res, axis=1), jnp.concatenate(ims, axis=1)


def _pack_complex(re, im):
    bits = lambda v: lax.bitcast_convert_type(v.astype(BF16).astype(F32), jnp.uint32)
    return (bits(re) >> 16) | bits(im)


def _unpack_complex(w):
    re = lax.bitcast_convert_type(w << 16, F32)
    im = lax.bitcast_convert_type(w & jnp.uint32(0xFFFF0000), F32)
    return jnp.concatenate([re, im], axis=0).astype(BF16)


FFT_GROUP = 8


def _digit_spec(d):
    return pl.BlockSpec((None, FFT_RADIX, FFT_GROUP, d), lambda b, j: (b, 0, j, 0))


def _fourier_front_kernel(x_ref, g_ref, sh_ref, sc_ref, wc_ref, tr_ref, ti_ref, w1_ref, a_ref,
                          zr_ref, zi_ref):
    r, kg = FFT_RADIX, FFT_GROUP
    d = x_ref.shape[-1]
    reps = d // LANES
    x = x_ref[...].reshape(r * kg, d)
    h = _norm_mod(x, g_ref[...], sh_ref[0], sc_ref[0]).astype(BF16)
    zr, zi = _chan_dft(h, wc_ref)
    for ct in range(reps):
        zr_ref[ct] = zr[:, ct * LANES:(ct + 1) * LANES]
        zi_ref[ct] = zi[:, ct * LANES:(ct + 1) * LANES]
    for j in range(kg):
        rows = pl.ds(j, r, stride=kg)
        pick = lambda ref: jnp.concatenate([ref[ct, rows, :] for ct in range(reps)], axis=1)
        z = jnp.concatenate([pick(zr_ref), pick(zi_ref)], axis=0).astype(BF16)
        a = jnp.dot(w1_ref[...], z, preferred_element_type=F32)
        ar, ai = a[:r], a[r:]
        tr = jnp.tile(tr_ref[j], (1, reps))
        ti = jnp.tile(ti_ref[j], (1, reps))
        a_ref[0, j] = _pack_complex(ar * tr - ai * ti, ar * ti + ai * tr)


def _fourier_front(x, g, sh, sc, mats, *, batch):
    t, d = x.shape
    r, kg = FFT_RADIX, FFT_GROUP
    const = lambda a: pl.BlockSpec(a.shape, lambda b, j: (0,) * a.ndim)
    mod = pl.BlockSpec((1, 1, d), lambda b, j: (b, 0, 0))
    tw = pl.BlockSpec((kg, r, LANES), lambda b, j: (j, 0, 0))
    return pl.pallas_call(
        _fourier_front_kernel,
        grid=(batch, r // kg),
        in_specs=[_digit_spec(d), const(g), mod, mod, const(mats["wc"]), tw, tw, const(mats["m1"])],
        out_specs=pl.BlockSpec((1, kg, r, d), lambda b, j: (b, j, 0, 0)),
        out_shape=jax.ShapeDtypeStruct((batch, r, r, d), jnp.uint32),
        scratch_shapes=[pltpu.VMEM((d // LANES, r * kg, LANES), F32)] * 2,
        compiler_params=_cparams(2),
        name="fourier_front",
    )(x.reshape(batch, r, r, d), g, sh, sc, mats["wc"], mats["tr"], mats["ti"], mats["m1"])


def _fourier_tail_kernel(a_ref, m2_ref, x_ref, *refs, final_norm):
    r, kg = FFT_RADIX, FFT_GROUP
    d = a_ref.shape[-1]
    reps = d // LANES
    o_ref, aw_ref, f_ref = refs[-3:]
    a = a_ref[...].reshape(r * kg, d)
    for ct in range(reps):
        aw_ref[ct] = a[:, ct * LANES:(ct + 1) * LANES]
    for j in range(kg):
        rows = pl.ds(j, r, stride=kg)
        w = jnp.concatenate([aw_ref[ct, rows, :] for ct in range(reps)], axis=1)
        f = jnp.dot(m2_ref[...], _unpack_complex(w), preferred_element_type=F32)
        for ct in range(reps):
            f_ref[ct, rows, :] = f[:, ct * LANES:(ct + 1) * LANES]
    f = jnp.concatenate([f_ref[ct] for ct in range(reps)], axis=1)
    out = _tail_math(f, x_ref[...].reshape(r * kg, d), *refs[:-3], final_norm=final_norm)
    o_ref[...] = out.reshape(r, kg, d)


def _fourier_tail(a, mats, w_o, b_o, gate_a, x, g, sh, sc, gate_m, w1, w2, final_g, *, batch,
                  final_norm):
    t, d = x.shape
    r, kg = FFT_RADIX, FFT_GROUP
    const = lambda v: pl.BlockSpec(v.shape, lambda b, j: (0,) * v.ndim, pipeline_mode=pl.Buffered(1))
    mod = pl.BlockSpec((1, 1, d), lambda b, j: (b, 0, 0))
    out = pl.pallas_call(
        functools.partial(_fourier_tail_kernel, final_norm=final_norm),
        grid=(batch, r // kg),
        in_specs=[_digit_spec(d), const(mats["m2"]), _digit_spec(d),
                  const(w_o), const(b_o), mod, const(g), mod, mod, mod,
                  const(w1), const(w2), const(final_g)],
        out_specs=_digit_spec(d),
        out_shape=jax.ShapeDtypeStruct((batch, r, r, d), F32),
        scratch_shapes=[pltpu.VMEM((d // LANES, r * kg, LANES), jnp.uint32),
                        pltpu.VMEM((d // LANES, r * kg, LANES), F32)],
        compiler_params=_cparams(2),
        name="fourier_tail",
    )(a, mats["m2"], x.reshape(batch, r, r, d), w_o, b_o, gate_a, g, sh, sc, gate_m, w1, w2, final_g)
    return out.reshape(t, d)


def _fourier_ctx_kernel(x_ref, g_ref, sh_ref, sc_ref, wc_ref, seq_ref, f_ref):
    h = _norm_mod(x_ref[...], g_ref[...], sh_ref[0], sc_ref[0]).astype(BF16)
    zr, zi = _chan_dft(h, wc_ref)
    z = jnp.concatenate([zr, zi], axis=0).astype(BF16)
    f_ref[...] = jnp.dot(seq_ref[...], z, preferred_element_type=F32)


def _fourier_ctx(x, g, sh, sc, mats, *, ctx_len):
    t, d = x.shape
    n_tiles = t // ctx_len
    return pl.pallas_call(
        _fourier_ctx_kernel,
        grid=(n_tiles,),
        in_specs=[
            pl.BlockSpec((ctx_len, d), lambda i: (i, 0)),
            _resident((1, d)), _mod_spec(n_tiles, d), _mod_spec(n_tiles, d),
            _resident(mats["wc"].shape), _resident(mats["ctx_seq"].shape),
        ],
        out_specs=pl.BlockSpec((ctx_len, d), lambda i: (i, 0)),
        out_shape=jax.ShapeDtypeStruct((t, d), F32),
        compiler_params=_cparams(1),
        name="fourier_ctx",
    )(x, g, sh, sc, mats["wc"], mats["ctx_seq"])


def _rope_tables(rows, dim, lane_of_dim):
    n_freq = dim // 4
    inv_freq = ROPE_BASE ** (-jnp.arange(n_freq, dtype=F32) / n_freq)
    t = jnp.arange(rows * GRID_W)
    row = (t // GRID_W).astype(F32)
    col = (t % GRID_W).astype(F32)
    ang = jnp.stack([row[:, None] * inv_freq, col[:, None] * inv_freq], axis=1)
    cos, sin = jnp.cos(ang), jnp.sin(ang)
    dims = np.array([lane_of_dim(l) for l in range(LANES)])
    live = dims >= 0
    dd = np.where(live, dims, 0)
    axis, pair, freq = dd // (2 * n_freq), (dd % (2 * n_freq)) // n_freq, dd % n_freq
    sign = np.where(pair == 0, -1.0, 1.0).astype(np.float32)
    cos_t = jnp.where(live[None, :], cos[:, axis, freq], 1.0)
    sin_t = jnp.where(live[None, :], sin[:, axis, freq] * sign[None, :], 0.0)
    return cos_t.astype(F32), sin_t.astype(F32)


def _split_sub_heads(q):
    lane = lax.broadcasted_iota(jnp.int32, q.shape, 1)
    first = (lane & (DIFF_HEAD_DIM // 2)) == 0
    zero = jnp.zeros_like(q)
    return [jnp.where(first, q, zero), jnp.where(first, zero, q)]


def _scores(q, k_refs):
    nt = (((1,), (1,)), ((), ()))
    return [lax.dot_general(q, k[0], nt, preferred_element_type=F32) for k in k_refs]


def _attend(q, k_refs, vaug_refs):
    parts = _scores(q, k_refs)
    m = functools.reduce(jnp.maximum, [jnp.max(p, axis=-1, keepdims=True) for p in parts])
    acc = None
    for p, v in zip(parts, vaug_refs):
        e = jnp.exp2(p - m).astype(BF16)
        part = jnp.dot(e, v[0], preferred_element_type=F32)
        acc = part if acc is None else acc + part
    return acc[:, :LANES] / acc[:, LANES:]


def _pipe_attn_kernel(*refs, mode, n_seg, lam_init):
    n_par = 5 if mode == "diff" else 0
    par = refs[:n_par]
    q_ref = refs[n_par]
    k_refs = refs[n_par + 1:n_par + 1 + n_seg]
    v_refs = refs[n_par + 1 + n_seg:n_par + 1 + 2 * n_seg]
    o_ref, s_ref, m_ref = refs[n_par + 1 + 2 * n_seg:]
    t = pl.program_id(0)
    nt = (((1,), (1,)), ((), ()))

    def scores(s_w, m_w):
        q = q_ref[0]
        if mode == "diff":
            qs = _split_sub_heads(q)
        else:
            qs = [q[:, :LANES], q[:, LANES:]]
        for c in range(2):
            m, off = None, 0
            for k in k_refs:
                kk = k[0] if mode == "diff" else k[0, :, c * LANES:(c + 1) * LANES]
                p = lax.dot_general(qs[c], kk, nt, preferred_element_type=F32)
                s_w[c, :, off:off + p.shape[1]] = p
                off += p.shape[1]
                pm = jnp.max(p, axis=-1, keepdims=True)
                m = pm if m is None else jnp.maximum(m, pm)
            m_w[c] = jnp.broadcast_to(m, m_w.shape[1:])

    def values(s_r, m_r):
        outs = []
        for c in range(2):
            m = m_r[c][:, :1]
            acc, off = None, 0
            for v in v_refs:
                w = v.shape[1]
                e = jnp.exp2(s_r[c, :, off:off + w] - m).astype(BF16)
                part = jnp.dot(e, v[0], preferred_element_type=F32)
                acc = part if acc is None else acc + part
                off += w
            outs.append(acc[:, :LANES] / acc[:, LANES:])
        if mode == "diff":
            lq1, lk1, lq2, lk2, sg = par
            lam = (jnp.exp(jnp.sum(lq1[...] * lk1[...], axis=-1, keepdims=True))
                   - jnp.exp(jnp.sum(lq2[...] * lk2[...], axis=-1, keepdims=True)) + lam_init)
            o = outs[0] - lam * outs[1]
            o_ref[0] = (_rms(o, sg[...]) * (1.0 - lam_init)).astype(BF16)
        else:
            lane = lax.broadcasted_iota(jnp.int32, outs[0].shape, 1)
            o_ref[0] = jnp.where(lane < MLA_V_DIM, outs[0], outs[1]).astype(BF16)

    @pl.when(t == 0)
    def _():
        s_ref[...] = jnp.zeros(s_ref.shape, F32)
        m_ref[...] = jnp.zeros(m_ref.shape, F32)

    values(s_ref, m_ref)
    scores(s_ref, m_ref)


def _pipe_attn(params, q_arr, k_arrs, v_arrs, *, mode, units, q_width, k_block, tq, lam_init=0.0):
    b, sq, _ = q_arr.shape
    nq = sq // tq
    n_items = b * units * nq
    n_seg = len(k_arrs)
    sk = sum(a.shape[1] for a in k_arrs)

    def item(t, lag):
        tt = jnp.clip(t - lag, 0, n_items - 1)
        return tt // (units * nq), (tt // nq) % units, tt % nq

    def q_map(t):
        bb, u, i = item(t, 0)
        return bb, i, u

    def k_map(t):
        bb, u, _ = item(t, 0)
        return bb, 0, k_block + u

    def v_map(t):
        bb, u, _ = item(t, 1)
        return bb, 0, k_block + u

    def o_map(t):
        bb, u, i = item(t, 1)
        return bb, i, u

    in_specs = [pl.BlockSpec(p.shape, lambda t: (0, 0)) for p in params]
    in_specs += [pl.BlockSpec((1, tq, q_width), q_map)]
    in_specs += [pl.BlockSpec((1, a.shape[1], q_width), k_map) for a in k_arrs]
    in_specs += [pl.BlockSpec((1, a.shape[1], 2 * LANES), v_map) for a in v_arrs]
    return pl.pallas_call(
        functools.partial(_pipe_attn_kernel, mode=mode, n_seg=n_seg, lam_init=lam_init),
        grid=(n_items + 1,),
        in_specs=in_specs,
        out_specs=pl.BlockSpec((1, tq, LANES), o_map),
        out_shape=jax.ShapeDtypeStruct((b, sq, units * LANES), BF16),
        scratch_shapes=[pltpu.VMEM((2, tq, sk), F32), pltpu.VMEM((2, tq, LANES), F32)],
        compiler_params=_cparams(1),
        name=mode + "_pipe_attn",
    )(*params, q_arr, *k_arrs, *v_arrs)


def _diff_qkv_kernel(x_ref, g_ref, sh_ref, sc_ref, w_ref, cos_ref, sin_ref, o_ref, *, rope, qscale):
    h = _norm_mod(x_ref[...], g_ref[...], sh_ref[0], sc_ref[0]).astype(BF16)
    d = x_ref.shape[1]
    acc = jnp.dot(h, w_ref[...], preferred_element_type=F32)
    heads = d // LANES
    ones = jnp.ones((x_ref.shape[0], LANES), BF16)
    for tile in range(2 * heads):
        a = acc[:, tile * LANES:(tile + 1) * LANES]
        if rope:
            a = _rope_tile(a, cos_ref[...], sin_ref[...])
        if tile < heads:
            a = a * qscale
        o_ref[:, tile * LANES:(tile + 1) * LANES] = a.astype(BF16)
    for hd in range(heads):
        src = (2 * heads + hd) * LANES
        dst = (2 * heads + 2 * hd) * LANES
        o_ref[:, dst:dst + LANES] = acc[:, src:src + LANES].astype(BF16)
        o_ref[:, dst + LANES:dst + 2 * LANES] = ones


def _diff_qkv(x, g, sh, sc, w, cos, sin, *, rope, tm):
    t, d = x.shape
    nb = sh.shape[0]
    tpb = t // nb // tm
    spb = t // nb // tm
    n = w.shape[1] + d
    qscale = DIFF_HEAD_DIM ** -0.5 * LOG2E
    tab = pl.BlockSpec((tm, LANES), (lambda i: (i % spb, 0)) if rope else (lambda i: (0, 0)))
    return pl.pallas_call(
        functools.partial(_diff_qkv_kernel, rope=rope, qscale=qscale),
        grid=(t // tm,),
        in_specs=[
            pl.BlockSpec((tm, d), lambda i: (i, 0)),
            _resident((1, d)), _mod_spec(tpb, d), _mod_spec(tpb, d),
            _resident(w.shape), tab, tab,
        ],
        out_specs=pl.BlockSpec((tm, n), lambda i: (i, 0)),
        out_shape=jax.ShapeDtypeStruct((t, n), BF16),
        compiler_params=_cparams(1),
        name="diff_qkv",
    )(x, g, sh, sc, w, cos, sin)


def _diff_attn_kernel(lq1_ref, lk1_ref, lq2_ref, lk2_ref, sg_ref, q_ref, *refs, n_seg, lam_init):
    k_refs, v_refs, o_ref = refs[:n_seg], refs[n_seg:2 * n_seg], refs[2 * n_seg]
    lam = (jnp.exp(jnp.sum(lq1_ref[...] * lk1_ref[...], axis=-1, keepdims=True))
           - jnp.exp(jnp.sum(lq2_ref[...] * lk2_ref[...], axis=-1, keepdims=True)) + lam_init)
    for hd in range(q_ref.shape[2] // LANES):
        head = slice(hd * LANES, (hd + 1) * LANES)
        ks = [k.at[:, :, head] for k in k_refs]
        vs = [v.at[:, :, 2 * hd * LANES:2 * (hd + 1) * LANES] for v in v_refs]
        q1, q2 = _split_sub_heads(q_ref[0, :, head])
        o = _attend(q1, ks, vs) - lam * _attend(q2, ks, vs)
        o_ref[0, :, head] = (_rms(o, sg_ref[...]) * (1.0 - lam_init)).astype(BF16)


def _diff_attn(lams, subln_g, q_arr, kv_arrs, *, lam_init):
    b, sq, d4 = q_arr.shape
    d = d4 // 4
    n_seg = len(kv_arrs)
    in_specs = [pl.BlockSpec(l.shape, lambda bb: (0, 0)) for l in lams]
    in_specs += [pl.BlockSpec((1, DIFF_V_DIM), lambda bb: (0, 0)),
                 pl.BlockSpec((1, sq, d), lambda bb: (bb, 0, 0))]
    in_specs += [pl.BlockSpec((1, a.shape[1], d), lambda bb: (bb, 0, 1)) for a in kv_arrs]
    in_specs += [pl.BlockSpec((1, a.shape[1], 2 * d), lambda bb: (bb, 0, 1)) for a in kv_arrs]
    return pl.pallas_call(
        functools.partial(_diff_attn_kernel, n_seg=n_seg, lam_init=lam_init),
        grid=(b,),
        in_specs=in_specs,
        out_specs=pl.BlockSpec((1, sq, d), lambda bb: (bb, 0, 0)),
        out_shape=jax.ShapeDtypeStruct((b, sq, d), BF16),
        compiler_params=_cparams(1),
        name="diff_attn",
    )(*lams, subln_g, q_arr, *kv_arrs, *kv_arrs)


def _mla_down_kernel(x_ref, g_ref, sh_ref, sc_ref, w_ref, qg_ref, kvg_ref, cos_ref, sin_ref,
                     cq_ref, ckv_ref, kr_ref, *, rope):
    h = _norm_mod(x_ref[...], g_ref[...], sh_ref[0], sc_ref[0]).astype(BF16)
    acc = jnp.dot(h, w_ref[...], preferred_element_type=F32)
    cq_ref[...] = _rms(acc[:, :MLA_Q_RANK], qg_ref[...]).astype(BF16)
    ckv_ref[...] = _rms(acc[:, MLA_Q_RANK:MLA_Q_RANK + MLA_KV_RANK], kvg_ref[...]).astype(BF16)
    kr = acc[:, MLA_Q_RANK + MLA_KV_RANK:]
    if rope:
        kr = _rope_tile(kr, cos_ref[...], sin_ref[...])
    kr_ref[...] = kr.astype(BF16)


def _mla_up_kernel(cq_ref, ckv_ref, kr_ref, wq_ref, wk_ref, wv_ref, cos_ref, sin_ref,
                   q_ref, k_ref, v_ref, *, rope, qscale):
    q = jnp.dot(cq_ref[...], wq_ref[...], preferred_element_type=F32)
    for hd in range(MLA_HEADS):
        a = q[:, hd * LANES:(hd + 1) * LANES]
        if rope:
            a = _rope_tile(a, cos_ref[...], sin_ref[...])
        q_ref[:, hd * LANES:(hd + 1) * LANES] = (a * qscale).astype(BF16)
    k = jnp.dot(ckv_ref[...], wk_ref[...], preferred_element_type=F32)
    kr = kr_ref[...].astype(F32)
    for hd in range(MLA_HEADS):
        k_ref[:, hd * LANES:(hd + 1) * LANES] = (k[:, hd * LANES:(hd + 1) * LANES] + kr).astype(BF16)
    v = jnp.dot(ckv_ref[...], wv_ref[...], preferred_element_type=F32)
    ones = jnp.ones((v.shape[0], LANES), BF16)
    for pair in range(MLA_HEADS // 2):
        v_ref[:, 2 * pair * LANES:(2 * pair + 1) * LANES] = v[:, pair * LANES:(pair + 1) * LANES].astype(BF16)
        v_ref[:, (2 * pair + 1) * LANES:(2 * pair + 2) * LANES] = ones


def _mla_project(x, g, sh, sc, w_down, qg, kvg, wq, wk, wv, cos, sin, *, rope, tm):
    t, d = x.shape
    nb = sh.shape[0]
    tpb = t // nb // tm
    tab = pl.BlockSpec((tm, LANES), (lambda i: (i % tpb, 0)) if rope else (lambda i: (0, 0)))
    row = lambda n: pl.BlockSpec((tm, n), lambda i: (i, 0))
    cq, ckv, kr = pl.pallas_call(
        functools.partial(_mla_down_kernel, rope=rope),
        grid=(t // tm,),
        in_specs=[row(d), _resident((1, d)), _mod_spec(tpb, d), _mod_spec(tpb, d),
                  _resident(w_down.shape), _resident(qg.shape), _resident(kvg.shape), tab, tab],
        out_specs=[row(MLA_Q_RANK), row(MLA_KV_RANK), row(LANES)],
        out_shape=[jax.ShapeDtypeStruct((t, MLA_Q_RANK), BF16),
                   jax.ShapeDtypeStruct((t, MLA_KV_RANK), BF16),
                   jax.ShapeDtypeStruct((t, LANES), BF16)],
        compiler_params=_cparams(1),
        name="mla_down",
    )(x, g, sh, sc, w_down, qg, kvg, cos, sin)
    qscale = (MLA_NOPE_DIM + MLA_ROPE_DIM) ** -0.5 * LOG2E
    nq, nv = wq.shape[1], 2 * wv.shape[1]
    return pl.pallas_call(
        functools.partial(_mla_up_kernel, rope=rope, qscale=qscale),
        grid=(t // tm,),
        in_specs=[row(MLA_Q_RANK), row(MLA_KV_RANK), row(LANES),
                  _resident(wq.shape), _resident(wk.shape), _resident(wv.shape), tab, tab],
        out_specs=[row(nq), row(nq), row(nv)],
        out_shape=[jax.ShapeDtypeStruct((t, nq), BF16), jax.ShapeDtypeStruct((t, nq), BF16),
                   jax.ShapeDtypeStruct((t, nv), BF16)],
        compiler_params=_cparams(1),
        name="mla_up",
    )(cq, ckv, kr, wq, wk, wv, cos, sin)


def _mla_attn_kernel(q_ref, *refs, n_seg):
    k_refs, v_refs, o_ref = refs[:n_seg], refs[n_seg:2 * n_seg], refs[2 * n_seg]
    outs = []
    for hd in range(2):
        q = q_ref[0, :, hd * LANES:(hd + 1) * LANES]
        ks = [k.at[:, :, hd * LANES:(hd + 1) * LANES] for k in k_refs]
        outs.append(_attend(q, ks, v_refs))
    lane = lax.broadcasted_iota(jnp.int32, outs[0].shape, 1)
    o_ref[0] = jnp.where(lane < MLA_V_DIM, outs[0], outs[1]).astype(BF16)


def _mla_attn(q, k_arrs, v_arrs, *, tq):
    b, sq, nq = q.shape
    pairs = nq // (2 * LANES)
    n_seg = len(k_arrs)
    in_specs = [pl.BlockSpec((1, tq, 2 * LANES), lambda bb, h, i: (bb, i, h))]
    in_specs += [pl.BlockSpec((1, a.shape[1], 2 * LANES), lambda bb, h, i: (bb, 0, h)) for a in k_arrs]
    in_specs += [pl.BlockSpec((1, a.shape[1], 2 * LANES), lambda bb, h, i: (bb, 0, h)) for a in v_arrs]
    return pl.pallas_call(
        functools.partial(_mla_attn_kernel, n_seg=n_seg),
        grid=(b, pairs, sq // tq),
        in_specs=in_specs,
        out_specs=pl.BlockSpec((1, tq, LANES), lambda bb, h, i: (bb, i, h)),
        out_shape=jax.ShapeDtypeStruct((b, sq, pairs * LANES), BF16),
        compiler_params=_cparams(3),
        name="mla_attn",
    )(q, *k_arrs, *v_arrs)


def _take_cols(w, src):
    src = np.asarray(src)
    return jnp.where(src >= 0, jnp.take(w, np.maximum(src, 0), axis=-1), 0).astype(w.dtype)


def _mla_weights(w_down, w_uq, w_ukv):
    hq = MLA_NOPE_DIM + MLA_ROPE_DIM
    lo = MLA_Q_RANK + MLA_KV_RANK
    kinds = [_mla_lane_dim(l) for l in range(LANES)]
    q_src = np.array([{"nope": dim, "rope": MLA_NOPE_DIM + dim, "pad": -1}[k] for k, dim in kinds])
    k_src = np.array([dim if k == "nope" else -1 for k, dim in kinds])
    r_src = np.array([lo + dim if k == "rope" else -1 for k, dim in kinds])
    wd = jnp.concatenate([w_down[:, :lo], _take_cols(w_down, r_src)], axis=1)
    wq = _take_cols(w_uq.reshape(MLA_Q_RANK, MLA_HEADS, hq), q_src)
    wkv = w_ukv.reshape(MLA_KV_RANK, MLA_HEADS, MLA_NOPE_DIM + MLA_V_DIM)
    wk = _take_cols(wkv, k_src)
    wv = wkv[:, :, MLA_NOPE_DIM:].reshape(MLA_KV_RANK, MLA_HEADS * MLA_V_DIM)
    flat = lambda w: w.reshape(w.shape[0], MLA_HEADS * LANES)
    return wd.astype(BF16), flat(wq).astype(BF16), flat(wk).astype(BF16), wv.astype(BF16)


def _diff_weights(w_qkv):
    d = w_qkv.shape[0]
    src = np.array([sub * DIFF_HEAD_DIM + dim for sub, dim in map(_diff_lane_dim, range(LANES))])
    qk = w_qkv[:, :2 * d].reshape(d, 2 * d // LANES, LANES)
    qk = _take_cols(qk, src).reshape(d, 2 * d)
    return jnp.concatenate([qk, w_qkv[:, 2 * d:]], axis=1).astype(BF16)


def kernel(x, c, ctx, c_ctx, mod_w, mod_b, norm_mix_g, norm_mlp_g, final_g, mlp_w1, mlp_w2, fourier_w, fourier_b, diff_w_qkv, diff_lambda_q1, diff_lambda_k1, diff_lambda_q2, diff_lambda_k2, diff_subln_g, diff_w_o, mla_w_down, mla_q_norm_g, mla_kv_norm_g, mla_w_uq, mla_w_ukv, mla_w_o):
    batch, seq, d = x.shape
    ctx_len = ctx.shape[1]
    depth = mod_w.shape[0]
    rows = seq // GRID_W
    assert seq == FFT_RADIX * FFT_RADIX and d == FOURIER_GROUPS * 256 and ctx_len == 256
    tm = 512

    cc_rows = -(-(batch + 1) // 8) * 8
    cc = jnp.zeros((cc_rows, d), F32).at[:batch].set(c).at[batch].set(c_ctx)
    mods = _modulation(cc, mod_w, mod_b).reshape(depth, cc_rows, 6, d)
    mats = _dft_mats()
    zero_bias = jnp.zeros((1, d), F32)
    final_g2 = final_g.reshape(1, d)

    cos_d, sin_d = _rope_tables(rows, DIFF_HEAD_DIM, lambda l: _diff_lane_dim(l)[1])
    lane_mla = lambda l: _mla_lane_dim(l)[1] if _mla_lane_dim(l)[0] == "rope" else -1
    cos_m, sin_m = _rope_tables(rows, MLA_ROPE_DIM, lane_mla)

    x_lat = x.reshape(batch * seq, d)
    x_ctx = ctx.reshape(batch * ctx_len, d)
    for i in range(depth):
        kind, j = i % N_MIXERS, i // N_MIXERS
        update_ctx = i < depth - 1
        ctx_feeds_mixer = update_ctx or kind != 0
        lat = [mods[i, :batch, k].reshape(batch, 1, d) for k in range(6)]
        cm = [mods[i, batch:batch + 1, k].reshape(1, 1, d) for k in range(6)]
        g_mix = norm_mix_g[i].reshape(1, d)
        g_mlp = norm_mlp_g[i].reshape(1, d)
        a_ctx = None
        if kind == 0:
            w_o = fourier_w[j].astype(BF16)
            b_o = fourier_b[j].reshape(1, d)
            a_lat = _fourier_front(x_lat, g_mix, lat[0], lat[1], mats, batch=batch)
            if update_ctx:
                a_ctx = _fourier_ctx(x_ctx, g_mix, cm[0], cm[1], mats, ctx_len=ctx_len)
        elif kind == 1:
            heads = d // DIFF_V_DIM
            w_qkv = _diff_weights(diff_w_qkv[j])
            w_o = diff_w_o[j].astype(BF16)
            b_o = zero_bias
            lam_init = 0.8 - 0.6 * math.exp(-0.3 * i)
            lams = [v[j].reshape(1, DIFF_HEAD_DIM) for v in
                    (diff_lambda_q1, diff_lambda_k1, diff_lambda_q2, diff_lambda_k2)]
            sg = diff_subln_g[j].reshape(1, DIFF_V_DIM)
            qkv_l = _diff_qkv(x_lat, g_mix, lat[0], lat[1], w_qkv, cos_d, sin_d, rope=True, tm=tm)
            qkv_l = qkv_l.reshape(batch, seq, 4 * d)
            qkv_c = _diff_qkv(x_ctx, g_mix, cm[0], cm[1], w_qkv, cos_d, sin_d, rope=False, tm=ctx_len)
            qkv_c = qkv_c.reshape(batch, ctx_len, 4 * d)
            o_l = _pipe_attn(lams + [sg], qkv_l, [qkv_c, qkv_l], [qkv_c, qkv_l], mode="diff",
                             units=heads, q_width=LANES, k_block=heads, tq=1024, lam_init=lam_init)
            a_lat = o_l.reshape(batch * seq, d)
            if update_ctx:
                o_c = _diff_attn(lams, sg, qkv_c, [qkv_c], lam_init=lam_init)
                a_ctx = o_c.reshape(batch * ctx_len, d)
        else:
            wd, wq, wk, wv = _mla_weights(mla_w_down[j], mla_w_uq[j], mla_w_ukv[j])
            w_o = mla_w_o[j].astype(BF16)
            b_o = zero_bias
            qg = mla_q_norm_g[j].reshape(1, MLA_Q_RANK)
            kvg = mla_kv_norm_g[j].reshape(1, MLA_KV_RANK)
            q_l, k_l, v_l = _mla_project(x_lat, g_mix, lat[0], lat[1], wd, qg, kvg, wq, wk, wv,
                                         cos_m, sin_m, rope=True, tm=tm)
            q_c, k_c, v_c = _mla_project(x_ctx, g_mix, cm[0], cm[1], wd, qg, kvg, wq, wk, wv,
                                         cos_m, sin_m, rope=False, tm=ctx_len)
            r3 = lambda a, s: a.reshape(batch, s, a.shape[1])
            o_l = _pipe_attn([], r3(q_l, seq), [r3(k_c, ctx_len), r3(k_l, seq)],
                             [r3(v_c, ctx_len), r3(v_l, seq)], mode="mla", units=MLA_HEADS // 2,
                             q_width=2 * LANES, k_block=0, tq=1024)
            a_lat = o_l.reshape(batch * seq, d)
            if update_ctx:
                o_c = _mla_attn(r3(q_c, ctx_len), [r3(k_c, ctx_len)], [r3(v_c, ctx_len)], tq=ctx_len)
                a_ctx = o_c.reshape(batch * ctx_len, d)
        w1 = mlp_w1[i].astype(BF16)
        w2 = mlp_w2[i].astype(BF16)
        last = i == depth - 1
        if kind == 0:
            x_lat = _fourier_tail(a_lat, mats, w_o, b_o, lat[2], x_lat, g_mlp, lat[3], lat[4], lat[5],
                                  w1, w2, final_g2, batch=batch, final_norm=last)
        else:
            x_lat = _block_tail(a_lat, w_o, b_o, lat[2], x_lat, g_mlp, lat[3], lat[4], lat[5], w1, w2,
                                final_g2, final_norm=last, tm=tm)
        if update_ctx:
            x_ctx = _block_tail(a_ctx, w_o, b_o, cm[2], x_ctx, g_mlp, cm[3], cm[4], cm[5], w1, w2,
                                final_g2, final_norm=False, tm=ctx_len)
    return x_lat.reshape(batch, seq, d)
```

```python
import functools
import math

import numpy as np
import jax
import jax.numpy as jnp
from jax import lax
from jax.experimental import pallas as pl
from jax.experimental.pallas import tpu as pltpu

F32 = jnp.float32
BF16 = jnp.bfloat16

GRID_W = 64
RMS_EPS = 1e-6
ROPE_BASE = 10000.0
FOURIER_GROUPS = 4
DIFF_HEAD_DIM = 64
DIFF_V_DIM = 128
MLA_HEADS = 16
MLA_NOPE_DIM = 64
MLA_ROPE_DIM = 32
MLA_V_DIM = 64
MLA_Q_RANK = 256
MLA_KV_RANK = 128
N_MIXERS = 3

LANES = 128
FFT_RADIX = 64
VMEM_LIMIT = 56 * 1024 * 1024
LOG2E = 1.4426950408889634


def _cparams(n_axes):
    return pltpu.CompilerParams(
        dimension_semantics=("arbitrary",) * n_axes, vmem_limit_bytes=VMEM_LIMIT)


def _resident(shape):
    nd = len(shape)
    return pl.BlockSpec(shape, lambda *_: (0,) * nd, pipeline_mode=pl.Buffered(1))


def _norm_mod(x, g, sh, sc):
    r = lax.rsqrt(jnp.mean(x * x, axis=-1, keepdims=True) + RMS_EPS)
    return ((x * r) * g) * (1.0 + sc) + sh


def _rms(x, g):
    r = lax.rsqrt(jnp.mean(x * x, axis=-1, keepdims=True) + RMS_EPS)
    return (x * r) * g


def _rope_tile(x, cos, sin):
    return x * cos + pltpu.roll(x, LANES // 2, 1) * sin


def _diff_lane_dim(lane):
    pair, sub, axis, freq = lane // 64, (lane % 64) // 32, (lane % 32) // 16, lane % 16
    return sub, axis * 32 + pair * 16 + freq


def _mla_lane_dim(lane):
    half, off = lane // 64, lane % 64
    if off < 16:
        return "rope", (off // 8) * 16 + half * 8 + off % 8
    if half == 0:
        return "nope", off - 16
    return ("nope", 48 + off - 16) if off < 32 else ("pad", 0)


def _mod_kernel(c_ref, w_ref, b_ref, o_ref):
    c = c_ref[...]
    s = (c / (1.0 + jnp.exp(-c))).astype(BF16)
    o_ref[0] = jnp.dot(s, w_ref[0].astype(BF16), preferred_element_type=F32) + b_ref[0]


def _modulation(cc, mod_w, mod_b):
    depth, d, n = mod_w.shape
    rows = cc.shape[0]
    tn = n // 4
    return pl.pallas_call(
        _mod_kernel,
        grid=(depth, n // tn),
        in_specs=[
            pl.BlockSpec((rows, d), lambda i, j: (0, 0)),
            pl.BlockSpec((1, d, tn), lambda i, j: (i, 0, j)),
            pl.BlockSpec((1, 1, tn), lambda i, j: (i, 0, j)),
        ],
        out_specs=pl.BlockSpec((1, rows, tn), lambda i, j: (i, 0, j)),
        out_shape=jax.ShapeDtypeStruct((depth, rows, n), F32),
        compiler_params=_cparams(2),
        name="modulation",
    )(cc, mod_w, mod_b.reshape(depth, 1, n))


def _mod_spec(tiles_per_batch, d):
    return pl.BlockSpec((1, 1, d), lambda i: (i // tiles_per_batch, 0, 0))


FF_CHUNK = 1024


def _tail_math(a, x, wo_ref, bo_ref, ga_ref, g_ref, sh_ref, sc_ref, gm_ref, w1_ref, w2_ref, fg_ref,
               final_norm):
    y = jnp.dot(a.astype(BF16), wo_ref[...], preferred_element_type=F32) + bo_ref[...]
    x = x + ga_ref[0] * y
    h = _norm_mod(x, g_ref[...], sh_ref[0], sc_ref[0]).astype(BF16)
    acc = jnp.zeros(x.shape, F32)
    for c in range(w1_ref.shape[1] // FF_CHUNK):
        u = jnp.dot(h, w1_ref[:, c * FF_CHUNK:(c + 1) * FF_CHUNK], preferred_element_type=F32)
        u = jnp.square(jnp.maximum(u, 0.0)).astype(BF16)
        acc = acc + jnp.dot(u, w2_ref[c * FF_CHUNK:(c + 1) * FF_CHUNK, :],
                            preferred_element_type=F32)
    out = x + gm_ref[0] * acc
    if final_norm:
        out = _rms(out, fg_ref[...])
    return out


def _block_tail_kernel(a_ref, x_ref, *refs, final_norm):
    o_ref = refs[-1]
    o_ref[...] = _tail_math(a_ref[...], x_ref[...], *refs[:-1], final_norm=final_norm)


def _block_tail(a, w_o, b_o, gate_a, x, g, sh, sc, gate_m, w1, w2, final_g, *, final_norm, tm):
    t, d = x.shape
    nb = sh.shape[0]
    tpb = t // nb // tm
    row = lambda n: pl.BlockSpec((tm, n), lambda i: (i, 0))
    return pl.pallas_call(
        functools.partial(_block_tail_kernel, final_norm=final_norm),
        grid=(t // tm,),
        in_specs=[
            row(a.shape[1]), row(d),
            _resident(w_o.shape), _resident((1, d)), _mod_spec(tpb, d),
            _resident((1, d)), _mod_spec(tpb, d), _mod_spec(tpb, d), _mod_spec(tpb, d),
            _resident(w1.shape), _resident(w2.shape), _resident((1, d)),
        ],
        out_specs=row(d),
        out_shape=jax.ShapeDtypeStruct((t, d), F32),
        compiler_params=_cparams(1),
        name="block_tail",
    )(a, x, w_o, b_o, gate_a, g, sh, sc, gate_m, w1, w2, final_g)


def _dft_mats():
    cg = 256
    j = np.arange(cg)
    th = 2.0 * np.pi * np.outer(j, j) / cg
    wc = np.concatenate([np.cos(th), -np.sin(th)], axis=1) / 16.0
    ctx_seq = np.concatenate([np.cos(th), np.sin(th)], axis=1) / 16.0
    r = np.arange(FFT_RADIX)
    ph = 2.0 * np.pi * np.outer(r, r) / FFT_RADIX
    dr, di = np.cos(ph) / 8.0, -np.sin(ph) / 8.0
    m1 = np.block([[dr, -di], [di, dr]])
    m2 = np.concatenate([dr, -di], axis=1)
    tw = 2.0 * np.pi * np.outer(r, r) / (FFT_RADIX * FFT_RADIX)
    tr = np.repeat(np.cos(tw)[:, :, None], LANES, axis=2)
    ti = np.repeat(-np.sin(tw)[:, :, None], LANES, axis=2)
    as_bf = lambda a: jnp.asarray(a, F32).astype(BF16)
    return dict(wc=as_bf(wc), ctx_seq=as_bf(ctx_seq), m1=as_bf(m1), m2=as_bf(m2),
                tr=jnp.asarray(tr, F32), ti=jnp.asarray(ti, F32))


def _chan_dft(h, wc_ref):
    cg = wc_ref.shape[0]
    res, ims = [], []
    for grp in range(h.shape[1] // cg):
        z = jnp.dot(h[:, grp * cg:(grp + 1) * cg], wc_ref[...], preferred_element_type=F32)
        res.append(z[:, :cg])
        ims.append(z[:, cg:])
    return jnp.concatenate(res, axis=1), jnp.concatenate(ims, axis=1)


def _pack_complex(re, im):
    bits = lambda v: lax.bitcast_convert_type(v.astype(BF16).astype(F32), jnp.uint32)
    return (bits(re) >> 16) | bits(im)


def _unpack_complex(w):
    re = lax.bitcast_convert_type(w << 16, F32)
    im = lax.bitcast_convert_type(w & jnp.uint32(0xFFFF0000), F32)
    return jnp.concatenate([re, im], axis=0).astype(BF16)


FFT_GROUP = 8


def _digit_spec(d):
    return pl.BlockSpec((None, FFT_RADIX, FFT_GROUP, d), lambda b, j: (b, 0, j, 0))


def _fourier_front_kernel(x_ref, g_ref, sh_ref, sc_ref, wc_ref, tr_ref, ti_ref, w1_ref, a_ref,
                          zr_ref, zi_ref):
    r, kg = FFT_RADIX, FFT_GROUP
    d = x_ref.shape[-1]
    reps = d // LANES
    x = x_ref[...].reshape(r * kg, d)
    h = _norm_mod(x, g_ref[...], sh_ref[0], sc_ref[0]).astype(BF16)
    zr, zi = _chan_dft(h, wc_ref)
    for ct in range(reps):
        zr_ref[ct] = zr[:, ct * LANES:(ct + 1) * LANES]
        zi_ref[ct] = zi[:, ct * LANES:(ct + 1) * LANES]
    for j in range(kg):
        rows = pl.ds(j, r, stride=kg)
        pick = lambda ref: jnp.concatenate([ref[ct, rows, :] for ct in range(reps)], axis=1)
        z = jnp.concatenate([pick(zr_ref), pick(zi_ref)], axis=0).astype(BF16)
        a = jnp.dot(w1_ref[...], z, preferred_element_type=F32)
        ar, ai = a[:r], a[r:]
        tr = jnp.tile(tr_ref[j], (1, reps))
        ti = jnp.tile(ti_ref[j], (1, reps))
        a_ref[0, j] = _pack_complex(ar * tr - ai * ti, ar * ti + ai * tr)


def _fourier_front(x, g, sh, sc, mats, *, batch):
    t, d = x.shape
    r, kg = FFT_RADIX, FFT_GROUP
    const = lambda a: pl.BlockSpec(a.shape, lambda b, j: (0,) * a.ndim)
    mod = pl.BlockSpec((1, 1, d), lambda b, j: (b, 0, 0))
    tw = pl.BlockSpec((kg, r, LANES), lambda b, j: (j, 0, 0))
    return pl.pallas_call(
        _fourier_front_kernel,
        grid=(batch, r // kg),
        in_specs=[_digit_spec(d), const(g), mod, mod, const(mats["wc"]), tw, tw, const(mats["m1"])],
        out_specs=pl.BlockSpec((1, kg, r, d), lambda b, j: (b, j, 0, 0)),
        out_shape=jax.ShapeDtypeStruct((batch, r, r, d), jnp.uint32),
        scratch_shapes=[pltpu.VMEM((d // LANES, r * kg, LANES), F32)] * 2,
        compiler_params=_cparams(2),
        name="fourier_front",
    )(x.reshape(batch, r, r, d), g, sh, sc, mats["wc"], mats["tr"], mats["ti"], mats["m1"])


def _fourier_tail_kernel(a_ref, m2_ref, x_ref, *refs, final_norm):
    r, kg = FFT_RADIX, FFT_GROUP
    d = a_ref.shape[-1]
    reps = d // LANES
    o_ref, aw_ref, f_ref = refs[-3:]
    a = a_ref[...].reshape(r * kg, d)
    for ct in range(reps):
        aw_ref[ct] = a[:, ct * LANES:(ct + 1) * LANES]
    for j in range(kg):
        rows = pl.ds(j, r, stride=kg)
        w = jnp.concatenate([aw_ref[ct, rows, :] for ct in range(reps)], axis=1)
        f = jnp.dot(m2_ref[...], _unpack_complex(w), preferred_element_type=F32)
        for ct in range(reps):
            f_ref[ct, rows, :] = f[:, ct * LANES:(ct + 1) * LANES]
    f = jnp.concatenate([f_ref[ct] for ct in range(reps)], axis=1)
    out = _tail_math(f, x_ref[...].reshape(r * kg, d), *refs[:-3], final_norm=final_norm)
    o_ref[...] = out.reshape(r, kg, d)


def _fourier_tail(a, mats, w_o, b_o, gate_a, x, g, sh, sc, gate_m, w1, w2, final_g, *, batch,
                  final_norm):
    t, d = x.shape
    r, kg = FFT_RADIX, FFT_GROUP
    const = lambda v: pl.BlockSpec(v.shape, lambda b, j: (0,) * v.ndim, pipeline_mode=pl.Buffered(1))
    mod = pl.BlockSpec((1, 1, d), lambda b, j: (b, 0, 0))
    out = pl.pallas_call(
        functools.partial(_fourier_tail_kernel, final_norm=final_norm),
        grid=(batch, r // kg),
        in_specs=[_digit_spec(d), const(mats["m2"]), _digit_spec(d),
                  const(w_o), const(b_o), mod, const(g), mod, mod, mod,
                  const(w1), const(w2), const(final_g)],
        out_specs=_digit_spec(d),
        out_shape=jax.ShapeDtypeStruct((batch, r, r, d), F32),
        scratch_shapes=[pltpu.VMEM((d // LANES, r * kg, LANES), jnp.uint32),
                        pltpu.VMEM((d // LANES, r * kg, LANES), F32)],
        compiler_params=_cparams(2),
        name="fourier_tail",
    )(a, mats["m2"], x.reshape(batch, r, r, d), w_o, b_o, gate_a, g, sh, sc, gate_m, w1, w2, final_g)
    return out.reshape(t, d)


def _fourier_ctx_kernel(x_ref, g_ref, sh_ref, sc_ref, wc_ref, seq_ref, f_ref):
    h = _norm_mod(x_ref[...], g_ref[...], sh_ref[0], sc_ref[0]).astype(BF16)
    zr, zi = _chan_dft(h, wc_ref)
    z = jnp.concatenate([zr, zi], axis=0).astype(BF16)
    f_ref[...] = jnp.dot(seq_ref[...], z, preferred_element_type=F32)


def _fourier_ctx(x, g, sh, sc, mats, *, ctx_len):
    t, d = x.shape
    n_tiles = t // ctx_len
    return pl.pallas_call(
        _fourier_ctx_kernel,
        grid=(n_tiles,),
        in_specs=[
            pl.BlockSpec((ctx_len, d), lambda i: (i, 0)),
            _resident((1, d)), _mod_spec(n_tiles, d), _mod_spec(n_tiles, d),
            _resident(mats["wc"].shape), _resident(mats["ctx_seq"].shape),
        ],
        out_specs=pl.BlockSpec((ctx_len, d), lambda i: (i, 0)),
        out_shape=jax.ShapeDtypeStruct((t, d), F32),
        compiler_params=_cparams(1),
        name="fourier_ctx",
    )(x, g, sh, sc, mats["wc"], mats["ctx_seq"])


def _rope_tables(rows, dim, lane_of_dim):
    n_freq = dim // 4
    inv_freq = ROPE_BASE ** (-jnp.arange(n_freq, dtype=F32) / n_freq)
    t = jnp.arange(rows * GRID_W)
    row = (t // GRID_W).astype(F32)
    col = (t % GRID_W).astype(F32)
    ang = jnp.stack([row[:, None] * inv_freq, col[:, None] * inv_freq], axis=1)
    cos, sin = jnp.cos(ang), jnp.sin(ang)
    dims = np.array([lane_of_dim(l) for l in range(LANES)])
    live = dims >= 0
    dd = np.where(live, dims, 0)
    axis, pair, freq = dd // (2 * n_freq), (dd % (2 * n_freq)) // n_freq, dd % n_freq
    sign = np.where(pair == 0, -1.0, 1.0).astype(np.float32)
    cos_t = jnp.where(live[None, :], cos[:, axis, freq], 1.0)
    sin_t = jnp.where(live[None, :], sin[:, axis, freq] * sign[None, :], 0.0)
    return cos_t.astype(F32), sin_t.astype(F32)


def _split_sub_heads(q):
    lane = lax.broadcasted_iota(jnp.int32, q.shape, 1)
    first = (lane & (DIFF_HEAD_DIM // 2)) == 0
    zero = jnp.zeros_like(q)
    return [jnp.where(first, q, zero), jnp.where(first, zero, q)]


def _scores(q, k_refs):
    nt = (((1,), (1,)), ((), ()))
    return [lax.dot_general(q, k[0], nt, preferred_element_type=F32) for k in k_refs]


def _attend(q, k_refs, vaug_refs):
    parts = _scores(q, k_refs)
    m = functools.reduce(jnp.maximum, [jnp.max(p, axis=-1, keepdims=True) for p in parts])
    acc = None
    for p, v in zip(parts, vaug_refs):
        e = jnp.exp2(p - m).astype(BF16)
        part = jnp.dot(e, v[0], preferred_element_type=F32)
        acc = part if acc is None else acc + part
    return acc[:, :LANES] / acc[:, LANES:]


def _pipe_attn_kernel(*refs, mode, n_seg, lam_init):
    n_par = 5 if mode == "diff" else 0
    par = refs[:n_par]
    q_ref = refs[n_par]
    k_refs = refs[n_par + 1:n_par + 1 + n_seg]
    v_refs = refs[n_par + 1 + n_seg:n_par + 1 + 2 * n_seg]
    o_ref, s_ref, m_ref = refs[n_par + 1 + 2 * n_seg:]
    t = pl.program_id(0)
    nt = (((1,), (1,)), ((), ()))

    def scores(s_w, m_w):
        q = q_ref[0]
        if mode == "diff":
            qs = _split_sub_heads(q)
        else:
            qs = [q[:, :LANES], q[:, LANES:]]
        for c in range(2):
            m, off = None, 0
            for k in k_refs:
                kk = k[0] if mode == "diff" else k[0, :, c * LANES:(c + 1) * LANES]
                p = lax.dot_general(qs[c], kk, nt, preferred_element_type=F32)
                s_w[c, :, off:off + p.shape[1]] = p
                off += p.shape[1]
                pm = jnp.max(p, axis=-1, keepdims=True)
                m = pm if m is None else jnp.maximum(m, pm)
            m_w[c] = jnp.broadcast_to(m, m_w.shape[1:])

    def values(s_r, m_r):
        outs = []
        for c in range(2):
            m = m_r[c][:, :1]
            acc, off = None, 0
            for v in v_refs:
                w = v.shape[1]
                e = jnp.exp2(s_r[c, :, off:off + w] - m).astype(BF16)
                part = jnp.dot(e, v[0], preferred_element_type=F32)
                acc = part if acc is None else acc + part
                off += w
            outs.append(acc[:, :LANES] / acc[:, LANES:])
        if mode == "diff":
            lq1, lk1, lq2, lk2, sg = par
            lam = (jnp.exp(jnp.sum(lq1[...] * lk1[...], axis=-1, keepdims=True))
                   - jnp.exp(jnp.sum(lq2[...] * lk2[...], axis=-1, keepdims=True)) + lam_init)
            o = outs[0] - lam * outs[1]
            o_ref[0] = (_rms(o, sg[...]) * (1.0 - lam_init)).astype(BF16)
        else:
            lane = lax.broadcasted_iota(jnp.int32, outs[0].shape, 1)
            o_ref[0] = jnp.where(lane < MLA_V_DIM, outs[0], outs[1]).astype(BF16)

    @pl.when(t == 0)
    def _():
        s_ref[...] = jnp.zeros(s_ref.shape, F32)
        m_ref[...] = jnp.zeros(m_ref.shape, F32)

    values(s_ref, m_ref)
    scores(s_ref, m_ref)


def _pipe_attn(params, q_arr, k_arrs, v_arrs, *, mode, units, q_width, k_block, tq, lam_init=0.0):
    b, sq, _ = q_arr.shape
    nq = sq // tq
    n_items = b * units * nq
    n_seg = len(k_arrs)
    sk = sum(a.shape[1] for a in k_arrs)

    def item(t, lag):
        tt = jnp.clip(t - lag, 0, n_items - 1)
        return tt // (units * nq), (tt // nq) % units, tt % nq

    def q_map(t):
        bb, u, i = item(t, 0)
        return bb, i, u

    def k_map(t):
        bb, u, _ = item(t, 0)
        return bb, 0, k_block + u

    def v_map(t):
        bb, u, _ = item(t, 1)
        return bb, 0, k_block + u

    def o_map(t):
        bb, u, i = item(t, 1)
        return bb, i, u

    in_specs = [pl.BlockSpec(p.shape, lambda t: (0, 0)) for p in params]
    in_specs += [pl.BlockSpec((1, tq, q_width), q_map)]
    in_specs += [pl.BlockSpec((1, a.shape[1], q_width), k_map) for a in k_arrs]
    in_specs += [pl.BlockSpec((1, a.shape[1], 2 * LANES), v_map) for a in v_arrs]
    return pl.pallas_call(
        functools.partial(_pipe_attn_kernel, mode=mode, n_seg=n_seg, lam_init=lam_init),
        grid=(n_items + 1,),
        in_specs=in_specs,
        out_specs=pl.BlockSpec((1, tq, LANES), o_map),
        out_shape=jax.ShapeDtypeStruct((b, sq, units * LANES), BF16),
        scratch_shapes=[pltpu.VMEM((2, tq, sk), F32), pltpu.VMEM((2, tq, LANES), F32)],
        compiler_params=_cparams(1),
        name=mode + "_pipe_attn",
    )(*params, q_arr, *k_arrs, *v_arrs)


def _diff_qkv_kernel(x_ref, g_ref, sh_ref, sc_ref, w_ref, cos_ref, sin_ref, o_ref, *, rope, qscale):
    h = _norm_mod(x_ref[...], g_ref[...], sh_ref[0], sc_ref[0]).astype(BF16)
    d = x_ref.shape[1]
    acc = jnp.dot(h, w_ref[...], preferred_element_type=F32)
    heads = d // LANES
    ones = jnp.ones((x_ref.shape[0], LANES), BF16)
    for tile in range(2 * heads):
        a = acc[:, tile * LANES:(tile + 1) * LANES]
        if rope:
            a = _rope_tile(a, cos_ref[...], sin_ref[...])
        if tile < heads:
            a = a * qscale
        o_ref[:, tile * LANES:(tile + 1) * LANES] = a.astype(BF16)
    for hd in range(heads):
        src = (2 * heads + hd) * LANES
        dst = (2 * heads + 2 * hd) * LANES
        o_ref[:, dst:dst + LANES] = acc[:, src:src + LANES].astype(BF16)
        o_ref[:, dst + LANES:dst + 2 * LANES] = ones


def _diff_qkv(x, g, sh, sc, w, cos, sin, *, rope, tm):
    t, d = x.shape
    nb = sh.shape[0]
    tpb = t // nb // tm
    spb = t // nb // tm
    n = w.shape[1] + d
    qscale = DIFF_HEAD_DIM ** -0.5 * LOG2E
    tab = pl.BlockSpec((tm, LANES), (lambda i: (i % spb, 0)) if rope else (lambda i: (0, 0)))
    return pl.pallas_call(
        functools.partial(_diff_qkv_kernel, rope=rope, qscale=qscale),
        grid=(t // tm,),
        in_specs=[
            pl.BlockSpec((tm, d), lambda i: (i, 0)),
            _resident((1, d)), _mod_spec(tpb, d), _mod_spec(tpb, d),
            _resident(w.shape), tab, tab,
        ],
        out_specs=pl.BlockSpec((tm, n), lambda i: (i, 0)),
        out_shape=jax.ShapeDtypeStruct((t, n), BF16),
        compiler_params=_cparams(1),
        name="diff_qkv",
    )(x, g, sh, sc, w, cos, sin)


def _diff_attn_kernel(lq1_ref, lk1_ref, lq2_ref, lk2_ref, sg_ref, q_ref, *refs, n_seg, lam_init):
    k_refs, v_refs, o_ref = refs[:n_seg], refs[n_seg:2 * n_seg], refs[2 * n_seg]
    lam = (jnp.exp(jnp.sum(lq1_ref[...] * lk1_ref[...], axis=-1, keepdims=True))
           - jnp.exp(jnp.sum(lq2_ref[...] * lk2_ref[...], axis=-1, keepdims=True)) + lam_init)
    for hd in range(q_ref.shape[2] // LANES):
        head = slice(hd * LANES, (hd + 1) * LANES)
        ks = [k.at[:, :, head] for k in k_refs]
        vs = [v.at[:, :, 2 * hd * LANES:2 * (hd + 1) * LANES] for v in v_refs]
        q1, q2 = _split_sub_heads(q_ref[0, :, head])
        o = _attend(q1, ks, vs) - lam * _attend(q2, ks, vs)
        o_ref[0, :, head] = (_rms(o, sg_ref[...]) * (1.0 - lam_init)).astype(BF16)


def _diff_attn(lams, subln_g, q_arr, kv_arrs, *, lam_init):
    b, sq, d4 = q_arr.shape
    d = d4 // 4
    n_seg = len(kv_arrs)
    in_specs = [pl.BlockSpec(l.shape, lambda bb: (0, 0)) for l in lams]
    in_specs += [pl.BlockSpec((1, DIFF_V_DIM), lambda bb: (0, 0)),
                 pl.BlockSpec((1, sq, d), lambda bb: (bb, 0, 0))]
    in_specs += [pl.BlockSpec((1, a.shape[1], d), lambda bb: (bb, 0, 1)) for a in kv_arrs]
    in_specs += [pl.BlockSpec((1, a.shape[1], 2 * d), lambda bb: (bb, 0, 1)) for a in kv_arrs]
    return pl.pallas_call(
        functools.partial(_diff_attn_kernel, n_seg=n_seg, lam_init=lam_init),
        grid=(b,),
        in_specs=in_specs,
        out_specs=pl.BlockSpec((1, sq, d), lambda bb: (bb, 0, 0)),
        out_shape=jax.ShapeDtypeStruct((b, sq, d), BF16),
        compiler_params=_cparams(1),
        name="diff_attn",
    )(*lams, subln_g, q_arr, *kv_arrs, *kv_arrs)


def _mla_proj_kernel(x_ref, g_ref, sh_ref, sc_ref, w_ref, qg_ref, kvg_ref, wq_ref, wk_ref, wv_ref,
                     cos_ref, sin_ref, q_ref, k_ref, v_ref, *, rope, qscale):
    h = _norm_mod(x_ref[...], g_ref[...], sh_ref[0], sc_ref[0]).astype(BF16)
    acc = jnp.dot(h, w_ref[...], preferred_element_type=F32)
    cq = _rms(acc[:, :MLA_Q_RANK], qg_ref[...]).astype(BF16)
    ckv = _rms(acc[:, MLA_Q_RANK:MLA_Q_RANK + MLA_KV_RANK], kvg_ref[...]).astype(BF16)
    kr = acc[:, MLA_Q_RANK + MLA_KV_RANK:]
    if rope:
        kr = _rope_tile(kr, cos_ref[...], sin_ref[...])
    q = jnp.dot(cq, wq_ref[...], preferred_element_type=F32)
    for hd in range(MLA_HEADS):
        a = q[:, hd * LANES:(hd + 1) * LANES]
        if rope:
            a = _rope_tile(a, cos_ref[...], sin_ref[...])
        q_ref[:, hd * LANES:(hd + 1) * LANES] = (a * qscale).astype(BF16)
    k = jnp.dot(ckv, wk_ref[...], preferred_element_type=F32)
    for hd in range(MLA_HEADS):
        k_ref[:, hd * LANES:(hd + 1) * LANES] = (k[:, hd * LANES:(hd + 1) * LANES] + kr).astype(BF16)
    v = jnp.dot(ckv, wv_ref[...], preferred_element_type=F32)
    ones = jnp.ones((v.shape[0], LANES), BF16)
    for pair in range(MLA_HEADS // 2):
        v_ref[:, 2 * pair * LANES:(2 * pair + 1) * LANES] = v[:, pair * LANES:(pair + 1) * LANES].astype(BF16)
        v_ref[:, (2 * pair + 1) * LANES:(2 * pair + 2) * LANES] = ones


def _mla_project(x, g, sh, sc, w_down, qg, kvg, wq, wk, wv, cos, sin, *, rope, tm):
    t, d = x.shape
    nb = sh.shape[0]
    tpb = t // nb // tm
    tab = pl.BlockSpec((tm, LANES), (lambda i: (i % tpb, 0)) if rope else (lambda i: (0, 0)))
    row = lambda n: pl.BlockSpec((tm, n), lambda i: (i, 0))
    qscale = (MLA_NOPE_DIM + MLA_ROPE_DIM) ** -0.5 * LOG2E
    nq, nv = wq.shape[1], 2 * wv.shape[1]
    return pl.pallas_call(
        functools.partial(_mla_proj_kernel, rope=rope, qscale=qscale),
        grid=(t // tm,),
        in_specs=[row(d), _resident((1, d)), _mod_spec(tpb, d), _mod_spec(tpb, d),
                  _resident(w_down.shape), _resident(qg.shape), _resident(kvg.shape),
                  _resident(wq.shape), _resident(wk.shape), _resident(wv.shape), tab, tab],
        out_specs=[row(nq), row(nq), row(nv)],
        out_shape=[jax.ShapeDtypeStruct((t, nq), BF16), jax.ShapeDtypeStruct((t, nq), BF16),
                   jax.ShapeDtypeStruct((t, nv), BF16)],
        compiler_params=_cparams(1),
        name="mla_proj",
    )(x, g, sh, sc, w_down, qg, kvg, wq, wk, wv, cos, sin)


def _mla_attn_kernel(q_ref, *refs, n_seg):
    k_refs, v_refs, o_ref = refs[:n_seg], refs[n_seg:2 * n_seg], refs[2 * n_seg]
    outs = []
    for hd in range(2):
        q = q_ref[0, :, hd * LANES:(hd + 1) * LANES]
        ks = [k.at[:, :, hd * LANES:(hd + 1) * LANES] for k in k_refs]
        outs.append(_attend(q, ks, v_refs))
    lane = lax.broadcasted_iota(jnp.int32, outs[0].shape, 1)
    o_ref[0] = jnp.where(lane < MLA_V_DIM, outs[0], outs[1]).astype(BF16)


def _mla_attn(q, k_arrs, v_arrs, *, tq):
    b, sq, nq = q.shape
    pairs = nq // (2 * LANES)
    n_seg = len(k_arrs)
    in_specs = [pl.BlockSpec((1, tq, 2 * LANES), lambda bb, h, i: (bb, i, h))]
    in_specs += [pl.BlockSpec((1, a.shape[1], 2 * LANES), lambda bb, h, i: (bb, 0, h)) for a in k_arrs]
    in_specs += [pl.BlockSpec((1, a.shape[1], 2 * LANES), lambda bb, h, i: (bb, 0, h)) for a in v_arrs]
    return pl.pallas_call(
        functools.partial(_mla_attn_kernel, n_seg=n_seg),
        grid=(b, pairs, sq // tq),
        in_specs=in_specs,
        out_specs=pl.BlockSpec((1, tq, LANES), lambda bb, h, i: (bb, i, h)),
        out_shape=jax.ShapeDtypeStruct((b, sq, pairs * LANES), BF16),
        compiler_params=_cparams(3),
        name="mla_attn",
    )(q, *k_arrs, *v_arrs)


def _take_cols(w, src):
    src = np.asarray(src)
    return jnp.where(src >= 0, jnp.take(w, np.maximum(src, 0), axis=-1), 0).astype(w.dtype)


def _mla_weights(w_down, w_uq, w_ukv):
    hq = MLA_NOPE_DIM + MLA_ROPE_DIM
    lo = MLA_Q_RANK + MLA_KV_RANK
    kinds = [_mla_lane_dim(l) for l in range(LANES)]
    q_src = np.array([{"nope": dim, "rope": MLA_NOPE_DIM + dim, "pad": -1}[k] for k, dim in kinds])
    k_src = np.array([dim if k == "nope" else -1 for k, dim in kinds])
    r_src = np.array([lo + dim if k == "rope" else -1 for k, dim in kinds])
    wd = jnp.concatenate([w_down[:, :lo], _take_cols(w_down, r_src)], axis=1)
    wq = _take_cols(w_uq.reshape(MLA_Q_RANK, MLA_HEADS, hq), q_src)
    wkv = w_ukv.reshape(MLA_KV_RANK, MLA_HEADS, MLA_NOPE_DIM + MLA_V_DIM)
    wk = _take_cols(wkv, k_src)
    wv = wkv[:, :, MLA_NOPE_DIM:].reshape(MLA_KV_RANK, MLA_HEADS * MLA_V_DIM)
    flat = lambda w: w.reshape(w.shape[0], MLA_HEADS * LANES)
    return wd.astype(BF16), flat(wq).astype(BF16), flat(wk).astype(BF16), wv.astype(BF16)


def _diff_weights(w_qkv):
    d = w_qkv.shape[0]
    src = np.array([sub * DIFF_HEAD_DIM + dim for sub, dim in map(_diff_lane_dim, range(LANES))])
    qk = w_qkv[:, :2 * d].reshape(d, 2 * d // LANES, LANES)
    qk = _take_cols(qk, src).reshape(d, 2 * d)
    return jnp.concatenate([qk, w_qkv[:, 2 * d:]], axis=1).astype(BF16)


def kernel(x, c, ctx, c_ctx, mod_w, mod_b, norm_mix_g, norm_mlp_g, final_g, mlp_w1, mlp_w2, fourier_w, fourier_b, diff_w_qkv, diff_lambda_q1, diff_lambda_k1, diff_lambda_q2, diff_lambda_k2, diff_subln_g, diff_w_o, mla_w_down, mla_q_norm_g, mla_kv_norm_g, mla_w_uq, mla_w_ukv, mla_w_o):
    batch, seq, d = x.shape
    ctx_len = ctx.shape[1]
    depth = mod_w.shape[0]
    rows = seq // GRID_W
    assert seq == FFT_RADIX * FFT_RADIX and d == FOURIER_GROUPS * 256 and ctx_len == 256
    tm = 512

    cc_rows = -(-(batch + 1) // 8) * 8
    cc = jnp.zeros((cc_rows, d), F32).at[:batch].set(c).at[batch].set(c_ctx)
    mods = _modulation(cc, mod_w, mod_b).reshape(depth, cc_rows, 6, d)
    mats = _dft_mats()
    zero_bias = jnp.zeros((1, d), F32)
    final_g2 = final_g.reshape(1, d)

    cos_d, sin_d = _rope_tables(rows, DIFF_HEAD_DIM, lambda l: _diff_lane_dim(l)[1])
    lane_mla = lambda l: _mla_lane_dim(l)[1] if _mla_lane_dim(l)[0] == "rope" else -1
    cos_m, sin_m = _rope_tables(rows, MLA_ROPE_DIM, lane_mla)

    x_lat = x.reshape(batch * seq, d)
    x_ctx = ctx.reshape(batch * ctx_len, d)
    for i in range(depth):
        kind, j = i % N_MIXERS, i // N_MIXERS
        update_ctx = i < depth - 1
        ctx_feeds_mixer = update_ctx or kind != 0
        lat = [mods[i, :batch, k].reshape(batch, 1, d) for k in range(6)]
        cm = [mods[i, batch:batch + 1, k].reshape(1, 1, d) for k in range(6)]
        g_mix = norm_mix_g[i].reshape(1, d)
        g_mlp = norm_mlp_g[i].reshape(1, d)
        a_ctx = None
        if kind == 0:
            w_o = fourier_w[j].astype(BF16)
            b_o = fourier_b[j].reshape(1, d)
            a_lat = _fourier_front(x_lat, g_mix, lat[0], lat[1], mats, batch=batch)
            if update_ctx:
                a_ctx = _fourier_ctx(x_ctx, g_mix, cm[0], cm[1], mats, ctx_len=ctx_len)
        elif kind == 1:
            heads = d // DIFF_V_DIM
            w_qkv = _diff_weights(diff_w_qkv[j])
            w_o = diff_w_o[j].astype(BF16)
            b_o = zero_bias
            lam_init = 0.8 - 0.6 * math.exp(-0.3 * i)
            lams = [v[j].reshape(1, DIFF_HEAD_DIM) for v in
                    (diff_lambda_q1, diff_lambda_k1, diff_lambda_q2, diff_lambda_k2)]
            sg = diff_subln_g[j].reshape(1, DIFF_V_DIM)
            qkv_l = _diff_qkv(x_lat, g_mix, lat[0], lat[1], w_qkv, cos_d, sin_d, rope=True, tm=tm)
            qkv_l = qkv_l.reshape(batch, seq, 4 * d)
            qkv_c = _diff_qkv(x_ctx, g_mix, cm[0], cm[1], w_qkv, cos_d, sin_d, rope=False, tm=ctx_len)
            qkv_c = qkv_c.reshape(batch, ctx_len, 4 * d)
            o_l = _pipe_attn(lams + [sg], qkv_l, [qkv_c, qkv_l], [qkv_c, qkv_l], mode="diff",
                             units=heads, q_width=LANES, k_block=heads, tq=1024, lam_init=lam_init)
            a_lat = o_l.reshape(batch * seq, d)
            if update_ctx:
                o_c = _diff_attn(lams, sg, qkv_c, [qkv_c], lam_init=lam_init)
                a_ctx = o_c.reshape(batch * ctx_len, d)
        else:
            wd, wq, wk, wv = _mla_weights(mla_w_down[j], mla_w_uq[j], mla_w_ukv[j])
            w_o = mla_w_o[j].astype(BF16)
            b_o = zero_bias
            qg = mla_q_norm_g[j].reshape(1, MLA_Q_RANK)
            kvg = mla_kv_norm_g[j].reshape(1, MLA_KV_RANK)
            q_l, k_l, v_l = _mla_project(x_lat, g_mix, lat[0], lat[1], wd, qg, kvg, wq, wk, wv,
                                         cos_m, sin_m, rope=True, tm=tm)
            q_c, k_c, v_c = _mla_project(x_ctx, g_mix, cm[0], cm[1], wd, qg, kvg, wq, wk, wv,
                                         cos_m, sin_m, rope=False, tm=ctx_len)
            r3 = lambda a, s: a.reshape(batch, s, a.shape[1])
            o_l = _pipe_attn([], r3(q_l, seq), [r3(k_c, ctx_len), r3(k_l, seq)],
                             [r3(v_c, ctx_len), r3(v_l, seq)], mode="mla", units=MLA_HEADS // 2,
                             q_width=2 * LANES, k_block=0, tq=1024)
            a_lat = o_l.reshape(batch * seq, d)
            if update_ctx:
                o_c = _mla_attn(r3(q_c, ctx_len), [r3(k_c, ctx_len)], [r3(v_c, ctx_len)], tq=ctx_len)
                a_ctx = o_c.reshape(batch * ctx_len, d)
        w1 = mlp_w1[i].astype(BF16)
        w2 = mlp_w2[i].astype(BF16)
        last = i == depth - 1
        if kind == 0:
            x_lat = _fourier_tail(a_lat, mats, w_o, b_o, lat[2], x_lat, g_mlp, lat[3], lat[4], lat[5],
                                  w1, w2, final_g2, batch=batch, final_norm=last)
        else:
            x_lat = _block_tail(a_lat, w_o, b_o, lat[2], x_lat, g_mlp, lat[3], lat[4], lat[5], w1, w2,
                                final_g2, final_norm=last, tm=tm)
        if update_ctx:
            x_ctx = _block_tail(a_ctx, w_o, b_o, cm[2], x_ctx, g_mlp, cm[3], cm[4], cm[5], w1, w2,
                                final_g2, final_norm=False, tm=ctx_len)
    return x_lat.reshape(batch, seq, d)
```

```python
import functools
import math

import numpy as np
import jax
import jax.numpy as jnp
from jax import lax
from jax.experimental import pallas as pl
from jax.experimental.pallas import tpu as pltpu

F32 = jnp.float32
BF16 = jnp.bfloat16

GRID_W = 64
RMS_EPS = 1e-6
ROPE_BASE = 10000.0
FOURIER_GROUPS = 4
DIFF_HEAD_DIM = 64
DIFF_V_DIM = 128
MLA_HEADS = 16
MLA_NOPE_DIM = 64
MLA_ROPE_DIM = 32
MLA_V_DIM = 64
MLA_Q_RANK = 256
MLA_KV_RANK = 128
N_MIXERS = 3

LANES = 128
FFT_RADIX = 64
VMEM_LIMIT = 56 * 1024 * 1024
LOG2E = 1.4426950408889634


def _cparams(n_axes):
    return pltpu.CompilerParams(
        dimension_semantics=("arbitrary",) * n_axes, vmem_limit_bytes=VMEM_LIMIT)


def _resident(shape):
    nd = len(shape)
    return pl.BlockSpec(shape, lambda *_: (0,) * nd, pipeline_mode=pl.Buffered(1))


def _norm_mod(x, g, sh, sc):
    r = lax.rsqrt(jnp.mean(x * x, axis=-1, keepdims=True) + RMS_EPS)
    return ((x * r) * g) * (1.0 + sc) + sh


def _rms(x, g):
    r = lax.rsqrt(jnp.mean(x * x, axis=-1, keepdims=True) + RMS_EPS)
    return (x * r) * g


def _rope_tile(x, cos, sin):
    return x * cos + pltpu.roll(x, LANES // 2, 1) * sin


def _diff_lane_dim(lane):
    pair, sub, axis, freq = lane // 64, (lane % 64) // 32, (lane % 32) // 16, lane % 16
    return sub, axis * 32 + pair * 16 + freq


def _mla_lane_dim(lane):
    half, off = lane // 64, lane % 64
    if off < 16:
        return "rope", (off // 8) * 16 + half * 8 + off % 8
    if half == 0:
        return "nope", off - 16
    return ("nope", 48 + off - 16) if off < 32 else ("pad", 0)


def _mod_kernel(c_ref, w_ref, b_ref, o_ref):
    c = c_ref[...]
    s = (c / (1.0 + jnp.exp(-c))).astype(BF16)
    o_ref[0] = jnp.dot(s, w_ref[0].astype(BF16), preferred_element_type=F32) + b_ref[0]


def _modulation(cc, mod_w, mod_b):
    depth, d, n = mod_w.shape
    rows = cc.shape[0]
    tn = n // 4
    return pl.pallas_call(
        _mod_kernel,
        grid=(depth, n // tn),
        in_specs=[
            pl.BlockSpec((rows, d), lambda i, j: (0, 0)),
            pl.BlockSpec((1, d, tn), lambda i, j: (i, 0, j)),
            pl.BlockSpec((1, 1, tn), lambda i, j: (i, 0, j)),
        ],
        out_specs=pl.BlockSpec((1, rows, tn), lambda i, j: (i, 0, j)),
        out_shape=jax.ShapeDtypeStruct((depth, rows, n), F32),
        compiler_params=_cparams(2),
        name="modulation",
    )(cc, mod_w, mod_b.reshape(depth, 1, n))


def _mod_spec(tiles_per_batch, d):
    return pl.BlockSpec((1, 1, d), lambda i: (i // tiles_per_batch, 0, 0))


FF_CHUNK = 1024


def _tail_math(a, x, wo_ref, bo_ref, ga_ref, g_ref, sh_ref, sc_ref, gm_ref, w1_ref, w2_ref, fg_ref,
               final_norm):
    y = jnp.dot(a.astype(BF16), wo_ref[...], preferred_element_type=F32) + bo_ref[...]
    x = x + ga_ref[0] * y
    h = _norm_mod(x, g_ref[...], sh_ref[0], sc_ref[0]).astype(BF16)
    acc = jnp.zeros(x.shape, F32)
    for c in range(w1_ref.shape[1] // FF_CHUNK):
        u = jnp.dot(h, w1_ref[:, c * FF_CHUNK:(c + 1) * FF_CHUNK], preferred_element_type=F32)
        u = jnp.square(jnp.maximum(u, 0.0)).astype(BF16)
        acc = acc + jnp.dot(u, w2_ref[c * FF_CHUNK:(c + 1) * FF_CHUNK, :],
                            preferred_element_type=F32)
    out = x + gm_ref[0] * acc
    if final_norm:
        out = _rms(out, fg_ref[...])
    return out


def _block_tail_kernel(a_ref, x_ref, *refs, final_norm):
    o_ref = refs[-1]
    o_ref[...] = _tail_math(a_ref[...], x_ref[...], *refs[:-1], final_norm=final_norm)


def _block_tail(a, w_o, b_o, gate_a, x, g, sh, sc, gate_m, w1, w2, final_g, *, final_norm, tm):
    t, d = x.shape
    nb = sh.shape[0]
    tpb = t // nb // tm
    row = lambda n: pl.BlockSpec((tm, n), lambda i: (i, 0))
    return pl.pallas_call(
        functools.partial(_block_tail_kernel, final_norm=final_norm),
        grid=(t // tm,),
        in_specs=[
            row(a.shape[1]), row(d),
            _resident(w_o.shape), _resident((1, d)), _mod_spec(tpb, d),
            _resident((1, d)), _mod_spec(tpb, d), _mod_spec(tpb, d), _mod_spec(tpb, d),
            _resident(w1.shape), _resident(w2.shape), _resident((1, d)),
        ],
        out_specs=row(d),
        out_shape=jax.ShapeDtypeStruct((t, d), F32),
        compiler_params=_cparams(1),
        name="block_tail",
    )(a, x, w_o, b_o, gate_a, g, sh, sc, gate_m, w1, w2, final_g)


def _dft_mats():
    cg = 256
    j = np.arange(cg)
    th = 2.0 * np.pi * np.outer(j, j) / cg
    wc = np.concatenate([np.cos(th), -np.sin(th)], axis=1) / 16.0
    ctx_seq = np.concatenate([np.cos(th), np.sin(th)], axis=1) / 16.0
    r = np.arange(FFT_RADIX)
    ph = 2.0 * np.pi * np.outer(r, r) / FFT_RADIX
    dr, di = np.cos(ph) / 8.0, -np.sin(ph) / 8.0
    m2 = np.concatenate([dr, -di], axis=1)
    tw = 2.0 * np.pi * np.outer(r, r) / (FFT_RADIX * FFT_RADIX)
    tr, ti = np.cos(tw)[:, :, None], -np.sin(tw)[:, :, None]
    er, ei = tr * dr - ti * di, tr * di + ti * dr
    m1 = np.concatenate([np.concatenate([er, -ei], axis=2), np.concatenate([ei, er], axis=2)], axis=1)
    as_bf = lambda a: jnp.asarray(a, F32).astype(BF16)
    return dict(wc=as_bf(wc), ctx_seq=as_bf(ctx_seq), m1=as_bf(m1), m2=as_bf(m2))


def _chan_dft(h, wc_ref):
    cg = wc_ref.shape[0]
    res, ims = [], []
    for grp in range(h.shape[1] // cg):
        z = jnp.dot(h[:, grp * cg:(grp + 1) * cg], wc_ref[...], preferred_element_type=F32)
        res.append(z[:, :cg])
        ims.append(z[:, cg:])
    return jnp.concatenate(res, axis=1), jnp.concatenate(ims, axis=1)


def _pack_complex(re, im):
    bits = lambda v: lax.bitcast_convert_type(v.astype(BF16).astype(F32), jnp.uint32)
    return (bits(re) >> 16) | bits(im)


def _unpack_complex(w):
    re = lax.bitcast_convert_type(w << 16, F32)
    im = lax.bitcast_convert_type(w & jnp.uint32(0xFFFF0000), F32)
    return jnp.concatenate([re, im], axis=0).astype(BF16)


FFT_GROUP = 8


def _digit_spec(d):
    return pl.BlockSpec((None, FFT_RADIX, FFT_GROUP, d), lambda b, j: (b, 0, j, 0))


def _fourier_front_kernel(x_ref, g_ref, sh_ref, sc_ref, wc_ref, w1_ref, a_ref, zr_ref, zi_ref):
    r, kg = FFT_RADIX, FFT_GROUP
    d = x_ref.shape[-1]
    reps = d // LANES
    x = x_ref[...].reshape(r * kg, d)
    h = _norm_mod(x, g_ref[...], sh_ref[0], sc_ref[0]).astype(BF16)
    zr, zi = _chan_dft(h, wc_ref)
    for ct in range(reps):
        zr_ref[ct] = zr[:, ct * LANES:(ct + 1) * LANES]
        zi_ref[ct] = zi[:, ct * LANES:(ct + 1) * LANES]
    for j in range(kg):
        rows = pl.ds(j, r, stride=kg)
        pick = lambda ref: jnp.concatenate([ref[ct, rows, :] for ct in range(reps)], axis=1)
        z = jnp.concatenate([pick(zr_ref), pick(zi_ref)], axis=0).astype(BF16)
        a = jnp.dot(w1_ref[j], z, preferred_element_type=F32)
        a_ref[0, j] = _pack_complex(a[:r], a[r:])


def _fourier_front(x, g, sh, sc, mats, *, batch):
    t, d = x.shape
    r, kg = FFT_RADIX, FFT_GROUP
    const = lambda a: pl.BlockSpec(a.shape, lambda b, j: (0,) * a.ndim)
    mod = pl.BlockSpec((1, 1, d), lambda b, j: (b, 0, 0))
    m1 = pl.BlockSpec((kg,) + mats["m1"].shape[1:], lambda b, j: (j, 0, 0))
    return pl.pallas_call(
        _fourier_front_kernel,
        grid=(batch, r // kg),
        in_specs=[_digit_spec(d), const(g), mod, mod, const(mats["wc"]), m1],
        out_specs=pl.BlockSpec((1, kg, r, d), lambda b, j: (b, j, 0, 0)),
        out_shape=jax.ShapeDtypeStruct((batch, r, r, d), jnp.uint32),
        scratch_shapes=[pltpu.VMEM((d // LANES, r * kg, LANES), F32)] * 2,
        compiler_params=_cparams(2),
        name="fourier_front",
    )(x.reshape(batch, r, r, d), g, sh, sc, mats["wc"], mats["m1"])


def _fourier_tail_kernel(a_ref, m2_ref, x_ref, *refs, final_norm):
    r, kg = FFT_RADIX, FFT_GROUP
    d = a_ref.shape[-1]
    reps = d // LANES
    o_ref, aw_ref, f_ref = refs[-3:]
    a = a_ref[...].reshape(r * kg, d)
    for ct in range(reps):
        aw_ref[ct] = a[:, ct * LANES:(ct + 1) * LANES]
    for j in range(kg):
        rows = pl.ds(j, r, stride=kg)
        w = jnp.concatenate([aw_ref[ct, rows, :] for ct in range(reps)], axis=1)
        f = jnp.dot(m2_ref[...], _unpack_complex(w), preferred_element_type=F32)
        for ct in range(reps):
            f_ref[ct, rows, :] = f[:, ct * LANES:(ct + 1) * LANES]
    f = jnp.concatenate([f_ref[ct] for ct in range(reps)], axis=1)
    out = _tail_math(f, x_ref[...].reshape(r * kg, d), *refs[:-3], final_norm=final_norm)
    o_ref[...] = out.reshape(r, kg, d)


def _fourier_tail(a, mats, w_o, b_o, gate_a, x, g, sh, sc, gate_m, w1, w2, final_g, *, batch,
                  final_norm):
    t, d = x.shape
    r, kg = FFT_RADIX, FFT_GROUP
    const = lambda v: pl.BlockSpec(v.shape, lambda b, j: (0,) * v.ndim, pipeline_mode=pl.Buffered(1))
    mod = pl.BlockSpec((1, 1, d), lambda b, j: (b, 0, 0))
    out = pl.pallas_call(
        functools.partial(_fourier_tail_kernel, final_norm=final_norm),
        grid=(batch, r // kg),
        in_specs=[_digit_spec(d), const(mats["m2"]), _digit_spec(d),
                  const(w_o), const(b_o), mod, const(g), mod, mod, mod,
                  const(w1), const(w2), const(final_g)],
        out_specs=_digit_spec(d),
        out_shape=jax.ShapeDtypeStruct((batch, r, r, d), F32),
        scratch_shapes=[pltpu.VMEM((d // LANES, r * kg, LANES), jnp.uint32),
                        pltpu.VMEM((d // LANES, r * kg, LANES), F32)],
        compiler_params=_cparams(2),
        name="fourier_tail",
    )(a, mats["m2"], x.reshape(batch, r, r, d), w_o, b_o, gate_a, g, sh, sc, gate_m, w1, w2, final_g)
    return out.reshape(t, d)


def _fourier_ctx_kernel(x_ref, g_ref, sh_ref, sc_ref, wc_ref, seq_ref, f_ref):
    h = _norm_mod(x_ref[...], g_ref[...], sh_ref[0], sc_ref[0]).astype(BF16)
    zr, zi = _chan_dft(h, wc_ref)
    z = jnp.concatenate([zr, zi], axis=0).astype(BF16)
    f_ref[...] = jnp.dot(seq_ref[...], z, preferred_element_type=F32)


def _fourier_ctx(x, g, sh, sc, mats, *, ctx_len):
    t, d = x.shape
    n_tiles = t // ctx_len
    return pl.pallas_call(
        _fourier_ctx_kernel,
        grid=(n_tiles,),
        in_specs=[
            pl.BlockSpec((ctx_len, d), lambda i: (i, 0)),
            _resident((1, d)), _mod_spec(n_tiles, d), _mod_spec(n_tiles, d),
            _resident(mats["wc"].shape), _resident(mats["ctx_seq"].shape),
        ],
        out_specs=pl.BlockSpec((ctx_len, d), lambda i: (i, 0)),
        out_shape=jax.ShapeDtypeStruct((t, d), F32),
        compiler_params=_cparams(1),
        name="fourier_ctx",
    )(x, g, sh, sc, mats["wc"], mats["ctx_seq"])


def _rope_tables(rows, dim, lane_of_dim):
    n_freq = dim // 4
    inv_freq = ROPE_BASE ** (-jnp.arange(n_freq, dtype=F32) / n_freq)
    t = jnp.arange(rows * GRID_W)
    row = (t // GRID_W).astype(F32)
    col = (t % GRID_W).astype(F32)
    ang = jnp.stack([row[:, None] * inv_freq, col[:, None] * inv_freq], axis=1)
    cos, sin = jnp.cos(ang), jnp.sin(ang)
    dims = np.array([lane_of_dim(l) for l in range(LANES)])
    live = dims >= 0
    dd = np.where(live, dims, 0)
    axis, pair, freq = dd // (2 * n_freq), (dd % (2 * n_freq)) // n_freq, dd % n_freq
    sign = np.where(pair == 0, -1.0, 1.0).astype(np.float32)
    cos_t = jnp.where(live[None, :], cos[:, axis, freq], 1.0)
    sin_t = jnp.where(live[None, :], sin[:, axis, freq] * sign[None, :], 0.0)
    return cos_t.astype(F32), sin_t.astype(F32)


def _split_sub_heads(q):
    lane = lax.broadcasted_iota(jnp.int32, q.shape, 1)
    first = (lane & (DIFF_HEAD_DIM // 2)) == 0
    zero = jnp.zeros_like(q)
    return [jnp.where(first, q, zero), jnp.where(first, zero, q)]


def _scores(q, k_refs):
    nt = (((1,), (1,)), ((), ()))
    return [lax.dot_general(q, k[0], nt, preferred_element_type=F32) for k in k_refs]


def _attend(q, k_refs, vaug_refs):
    parts = _scores(q, k_refs)
    m = functools.reduce(jnp.maximum, [jnp.max(p, axis=-1, keepdims=True) for p in parts])
    acc = None
    for p, v in zip(parts, vaug_refs):
        e = jnp.exp2(p - m).astype(BF16)
        part = jnp.dot(e, v[0], preferred_element_type=F32)
        acc = part if acc is None else acc + part
    return acc[:, :LANES] / acc[:, LANES:]


def _pipe_attn_kernel(*refs, mode, n_seg, lam_init):
    n_par = 5 if mode == "diff" else 0
    par = refs[:n_par]
    q_ref = refs[n_par]
    k_refs = refs[n_par + 1:n_par + 1 + n_seg]
    v_refs = refs[n_par + 1 + n_seg:n_par + 1 + 2 * n_seg]
    o_ref, s_ref, m_ref = refs[n_par + 1 + 2 * n_seg:]
    t = pl.program_id(0)
    nt = (((1,), (1,)), ((), ()))

    def scores(s_w, m_w):
        q = q_ref[0]
        if mode == "diff":
            qs = _split_sub_heads(q)
        else:
            qs = [q[:, :LANES], q[:, LANES:]]
        for c in range(2):
            m, off = None, 0
            for k in k_refs:
                kk = k[0] if mode == "diff" else k[0, :, c * LANES:(c + 1) * LANES]
                p = lax.dot_general(qs[c], kk, nt, preferred_element_type=F32)
                s_w[c, :, off:off + p.shape[1]] = p
                off += p.shape[1]
                pm = jnp.max(p, axis=-1, keepdims=True)
                m = pm if m is None else jnp.maximum(m, pm)
            m_w[c] = jnp.broadcast_to(m, m_w.shape[1:])

    def values(s_r, m_r):
        outs = []
        for c in range(2):
            m = m_r[c][:, :1]
            acc, off = None, 0
            for v in v_refs:
                w = v.shape[1]
                e = jnp.exp2(s_r[c, :, off:off + w] - m).astype(BF16)
                part = jnp.dot(e, v[0], preferred_element_type=F32)
                acc = part if acc is None else acc + part
                off += w
            outs.append(acc[:, :LANES] / acc[:, LANES:])
        if mode == "diff":
            lq1, lk1, lq2, lk2, sg = par
            lam = (jnp.exp(jnp.sum(lq1[...] * lk1[...], axis=-1, keepdims=True))
                   - jnp.exp(jnp.sum(lq2[...] * lk2[...], axis=-1, keepdims=True)) + lam_init)
            o = outs[0] - lam * outs[1]
            o_ref[0] = (_rms(o, sg[...]) * (1.0 - lam_init)).astype(BF16)
        else:
            lane = lax.broadcasted_iota(jnp.int32, outs[0].shape, 1)
            o_ref[0] = jnp.where(lane < MLA_V_DIM, outs[0], outs[1]).astype(BF16)

    @pl.when(t == 0)
    def _():
        s_ref[...] = jnp.zeros(s_ref.shape, F32)
        m_ref[...] = jnp.zeros(m_ref.shape, F32)

    values(s_ref, m_ref)
    scores(s_ref, m_ref)


def _pipe_attn(params, q_arr, k_arrs, v_arrs, *, mode, units, q_width, k_block, tq, lam_init=0.0):
    b, sq, _ = q_arr.shape
    nq = sq // tq
    n_items = b * units * nq
    n_seg = len(k_arrs)
    sk = sum(a.shape[1] for a in k_arrs)

    def item(t, lag):
        tt = jnp.clip(t - lag, 0, n_items - 1)
        return tt // (units * nq), (tt // nq) % units, tt % nq

    def q_map(t):
        bb, u, i = item(t, 0)
        return bb, i, u

    def k_map(t):
        bb, u, _ = item(t, 0)
        return bb, 0, k_block + u

    def v_map(t):
        bb, u, _ = item(t, 1)
        return bb, 0, k_block + u

    def o_map(t):
        bb, u, i = item(t, 1)
        return bb, i, u

    in_specs = [pl.BlockSpec(p.shape, lambda t: (0, 0)) for p in params]
    in_specs += [pl.BlockSpec((1, tq, q_width), q_map)]
    in_specs += [pl.BlockSpec((1, a.shape[1], q_width), k_map) for a in k_arrs]
    in_specs += [pl.BlockSpec((1, a.shape[1], 2 * LANES), v_map) for a in v_arrs]
    return pl.pallas_call(
        functools.partial(_pipe_attn_kernel, mode=mode, n_seg=n_seg, lam_init=lam_init),
        grid=(n_items + 1,),
        in_specs=in_specs,
        out_specs=pl.BlockSpec((1, tq, LANES), o_map),
        out_shape=jax.ShapeDtypeStruct((b, sq, units * LANES), BF16),
        scratch_shapes=[pltpu.VMEM((2, tq, sk), F32), pltpu.VMEM((2, tq, LANES), F32)],
        compiler_params=_cparams(1),
        name=mode + "_pipe_attn",
    )(*params, q_arr, *k_arrs, *v_arrs)


def _diff_qkv_kernel(x_ref, g_ref, sh_ref, sc_ref, w_ref, cos_ref, sin_ref, o_ref, *, rope, qscale):
    h = _norm_mod(x_ref[...], g_ref[...], sh_ref[0], sc_ref[0]).astype(BF16)
    d = x_ref.shape[1]
    acc = jnp.dot(h, w_ref[...], preferred_element_type=F32)
    heads = d // LANES
    ones = jnp.ones((x_ref.shape[0], LANES), BF16)
    for tile in range(2 * heads):
        a = acc[:, tile * LANES:(tile + 1) * LANES]
        if rope:
            a = _rope_tile(a, cos_ref[...], sin_ref[...])
        if tile < heads:
            a = a * qscale
        o_ref[:, tile * LANES:(tile + 1) * LANES] = a.astype(BF16)
    for hd in range(heads):
        src = (2 * heads + hd) * LANES
        dst = (2 * heads + 2 * hd) * LANES
        o_ref[:, dst:dst + LANES] = acc[:, src:src + LANES].astype(BF16)
        o_ref[:, dst + LANES:dst + 2 * LANES] = ones


def _diff_qkv(x, g, sh, sc, w, cos, sin, *, rope, tm):
    t, d = x.shape
    nb = sh.shape[0]
    tpb = t // nb // tm
    spb = t // nb // tm
    n = w.shape[1] + d
    qscale = DIFF_HEAD_DIM ** -0.5 * LOG2E
    tab = pl.BlockSpec((tm, LANES), (lambda i: (i % spb, 0)) if rope else (lambda i: (0, 0)))
    return pl.pallas_call(
        functools.partial(_diff_qkv_kernel, rope=rope, qscale=qscale),
        grid=(t // tm,),
        in_specs=[
            pl.BlockSpec((tm, d), lambda i: (i, 0)),
            _resident((1, d)), _mod_spec(tpb, d), _mod_spec(tpb, d),
            _resident(w.shape), tab, tab,
        ],
        out_specs=pl.BlockSpec((tm, n), lambda i: (i, 0)),
        out_shape=jax.ShapeDtypeStruct((t, n), BF16),
        compiler_params=_cparams(1),
        name="diff_qkv",
    )(x, g, sh, sc, w, cos, sin)


def _diff_attn_kernel(lq1_ref, lk1_ref, lq2_ref, lk2_ref, sg_ref, q_ref, *refs, n_seg, lam_init):
    k_refs, v_refs, o_ref = refs[:n_seg], refs[n_seg:2 * n_seg], refs[2 * n_seg]
    lam = (jnp.exp(jnp.sum(lq1_ref[...] * lk1_ref[...], axis=-1, keepdims=True))
           - jnp.exp(jnp.sum(lq2_ref[...] * lk2_ref[...], axis=-1, keepdims=True)) + lam_init)
    for hd in range(q_ref.shape[2] // LANES):
        head = slice(hd * LANES, (hd + 1) * LANES)
        ks = [k.at[:, :, head] for k in k_refs]
        vs = [v.at[:, :, 2 * hd * LANES:2 * (hd + 1) * LANES] for v in v_refs]
        q1, q2 = _split_sub_heads(q_ref[0, :, head])
        o = _attend(q1, ks, vs) - lam * _attend(q2, ks, vs)
        o_ref[0, :, head] = (_rms(o, sg_ref[...]) * (1.0 - lam_init)).astype(BF16)


def _diff_attn(lams, subln_g, q_arr, kv_arrs, *, lam_init):
    b, sq, d4 = q_arr.shape
    d = d4 // 4
    n_seg = len(kv_arrs)
    in_specs = [pl.BlockSpec(l.shape, lambda bb: (0, 0)) for l in lams]
    in_specs += [pl.BlockSpec((1, DIFF_V_DIM), lambda bb: (0, 0)),
                 pl.BlockSpec((1, sq, d), lambda bb: (bb, 0, 0))]
    in_specs += [pl.BlockSpec((1, a.shape[1], d), lambda bb: (bb, 0, 1)) for a in kv_arrs]
    in_specs += [pl.BlockSpec((1, a.shape[1], 2 * d), lambda bb: (bb, 0, 1)) for a in kv_arrs]
    return pl.pallas_call(
        functools.partial(_diff_attn_kernel, n_seg=n_seg, lam_init=lam_init),
        grid=(b,),
        in_specs=in_specs,
        out_specs=pl.BlockSpec((1, sq, d), lambda bb: (bb, 0, 0)),
        out_shape=jax.ShapeDtypeStruct((b, sq, d), BF16),
        compiler_params=_cparams(1),
        name="diff_attn",
    )(*lams, subln_g, q_arr, *kv_arrs, *kv_arrs)


def _mla_proj_kernel(x_ref, g_ref, sh_ref, sc_ref, w_ref, qg_ref, kvg_ref, wq_ref, wk_ref, wv_ref,
                     cos_ref, sin_ref, q_ref, k_ref, v_ref, *, rope, qscale):
    h = _norm_mod(x_ref[...], g_ref[...], sh_ref[0], sc_ref[0]).astype(BF16)
    acc = jnp.dot(h, w_ref[...], preferred_element_type=F32)
    cq = _rms(acc[:, :MLA_Q_RANK], qg_ref[...]).astype(BF16)
    ckv = _rms(acc[:, MLA_Q_RANK:MLA_Q_RANK + MLA_KV_RANK], kvg_ref[...]).astype(BF16)
    kr = acc[:, MLA_Q_RANK + MLA_KV_RANK:]
    if rope:
        kr = _rope_tile(kr, cos_ref[...], sin_ref[...])
    q = jnp.dot(cq, wq_ref[...], preferred_element_type=F32)
    for hd in range(MLA_HEADS):
        a = q[:, hd * LANES:(hd + 1) * LANES]
        if rope:
            a = _rope_tile(a, cos_ref[...], sin_ref[...])
        q_ref[:, hd * LANES:(hd + 1) * LANES] = (a * qscale).astype(BF16)
    k = jnp.dot(ckv, wk_ref[...], preferred_element_type=F32)
    for hd in range(MLA_HEADS):
        k_ref[:, hd * LANES:(hd + 1) * LANES] = (k[:, hd * LANES:(hd + 1) * LANES] + kr).astype(BF16)
    v = jnp.dot(ckv, wv_ref[...], preferred_element_type=F32)
    ones = jnp.ones((v.shape[0], LANES), BF16)
    for pair in range(MLA_HEADS // 2):
        v_ref[:, 2 * pair * LANES:(2 * pair + 1) * LANES] = v[:, pair * LANES:(pair + 1) * LANES].astype(BF16)
        v_ref[:, (2 * pair + 1) * LANES:(2 * pair + 2) * LANES] = ones


def _mla_project(x, g, sh, sc, w_down, qg, kvg, wq, wk, wv, cos, sin, *, rope, tm):
    t, d = x.shape
    nb = sh.shape[0]
    tpb = t // nb // tm
    tab = pl.BlockSpec((tm, LANES), (lambda i: (i % tpb, 0)) if rope else (lambda i: (0, 0)))
    row = lambda n: pl.BlockSpec((tm, n), lambda i: (i, 0))
    qscale = (MLA_NOPE_DIM + MLA_ROPE_DIM) ** -0.5 * LOG2E
    nq, nv = wq.shape[1], 2 * wv.shape[1]
    return pl.pallas_call(
        functools.partial(_mla_proj_kernel, rope=rope, qscale=qscale),
        grid=(t // tm,),
        in_specs=[row(d), _resident((1, d)), _mod_spec(tpb, d), _mod_spec(tpb, d),
                  _resident(w_down.shape), _resident(qg.shape), _resident(kvg.shape),
                  _resident(wq.shape), _resident(wk.shape), _resident(wv.shape), tab, tab],
        out_specs=[row(nq), row(nq), row(nv)],
        out_shape=[jax.ShapeDtypeStruct((t, nq), BF16), jax.ShapeDtypeStruct((t, nq), BF16),
                   jax.ShapeDtypeStruct((t, nv), BF16)],
        compiler_params=_cparams(1),
        name="mla_proj",
    )(x, g, sh, sc, w_down, qg, kvg, wq, wk, wv, cos, sin)


def _mla_attn_kernel(q_ref, *refs, n_seg):
    k_refs, v_refs, o_ref = refs[:n_seg], refs[n_seg:2 * n_seg], refs[2 * n_seg]
    outs = []
    for hd in range(2):
        q = q_ref[0, :, hd * LANES:(hd + 1) * LANES]
        ks = [k.at[:, :, hd * LANES:(hd + 1) * LANES] for k in k_refs]
        outs.append(_attend(q, ks, v_refs))
    lane = lax.broadcasted_iota(jnp.int32, outs[0].shape, 1)
    o_ref[0] = jnp.where(lane < MLA_V_DIM, outs[0], outs[1]).astype(BF16)


def _mla_attn(q, k_arrs, v_arrs, *, tq):
    b, sq, nq = q.shape
    pairs = nq // (2 * LANES)
    n_seg = len(k_arrs)
    in_specs = [pl.BlockSpec((1, tq, 2 * LANES), lambda bb, h, i: (bb, i, h))]
    in_specs += [pl.BlockSpec((1, a.shape[1], 2 * LANES), lambda bb, h, i: (bb, 0, h)) for a in k_arrs]
    in_specs += [pl.BlockSpec((1, a.shape[1], 2 * LANES), lambda bb, h, i: (bb, 0, h)) for a in v_arrs]
    return pl.pallas_call(
        functools.partial(_mla_attn_kernel, n_seg=n_seg),
        grid=(b, pairs, sq // tq),
        in_specs=in_specs,
        out_specs=pl.BlockSpec((1, tq, LANES), lambda bb, h, i: (bb, i, h)),
        out_shape=jax.ShapeDtypeStruct((b, sq, pairs * LANES), BF16),
        compiler_params=_cparams(3),
        name="mla_attn",
    )(q, *k_arrs, *v_arrs)


def _take_cols(w, src):
    src = np.asarray(src)
    return jnp.where(src >= 0, jnp.take(w, np.maximum(src, 0), axis=-1), 0).astype(w.dtype)


def _mla_weights(w_down, w_uq, w_ukv):
    hq = MLA_NOPE_DIM + MLA_ROPE_DIM
    lo = MLA_Q_RANK + MLA_KV_RANK
    kinds = [_mla_lane_dim(l) for l in range(LANES)]
    q_src = np.array([{"nope": dim, "rope": MLA_NOPE_DIM + dim, "pad": -1}[k] for k, dim in kinds])
    k_src = np.array([dim if k == "nope" else -1 for k, dim in kinds])
    r_src = np.array([lo + dim if k == "rope" else -1 for k, dim in kinds])
    wd = jnp.concatenate([w_down[:, :lo], _take_cols(w_down, r_src)], axis=1)
    wq = _take_cols(w_uq.reshape(MLA_Q_RANK, MLA_HEADS, hq), q_src)
    wkv = w_ukv.reshape(MLA_KV_RANK, MLA_HEADS, MLA_NOPE_DIM + MLA_V_DIM)
    wk = _take_cols(wkv, k_src)
    wv = wkv[:, :, MLA_NOPE_DIM:].reshape(MLA_KV_RANK, MLA_HEADS * MLA_V_DIM)
    flat = lambda w: w.reshape(w.shape[0], MLA_HEADS * LANES)
    return wd.astype(BF16), flat(wq).astype(BF16), flat(wk).astype(BF16), wv.astype(BF16)


def _diff_weights(w_qkv):
    d = w_qkv.shape[0]
    src = np.array([sub * DIFF_HEAD_DIM + dim for sub, dim in map(_diff_lane_dim, range(LANES))])
    qk = w_qkv[:, :2 * d].reshape(d, 2 * d // LANES, LANES)
    qk = _take_cols(qk, src).reshape(d, 2 * d)
    return jnp.concatenate([qk, w_qkv[:, 2 * d:]], axis=1).astype(BF16)


def kernel(x, c, ctx, c_ctx, mod_w, mod_b, norm_mix_g, norm_mlp_g, final_g, mlp_w1, mlp_w2, fourier_w, fourier_b, diff_w_qkv, diff_lambda_q1, diff_lambda_k1, diff_lambda_q2, diff_lambda_k2, diff_subln_g, diff_w_o, mla_w_down, mla_q_norm_g, mla_kv_norm_g, mla_w_uq, mla_w_ukv, mla_w_o):
    batch, seq, d = x.shape
    ctx_len = ctx.shape[1]
    depth = mod_w.shape[0]
    rows = seq // GRID_W
    assert seq == FFT_RADIX * FFT_RADIX and d == FOURIER_GROUPS * 256 and ctx_len == 256
    tm = 512

    cc_rows = -(-(batch + 1) // 8) * 8
    cc = jnp.zeros((cc_rows, d), F32).at[:batch].set(c).at[batch].set(c_ctx)
    mods = _modulation(cc, mod_w, mod_b).reshape(depth, cc_rows, 6, d)
    mats = _dft_mats()
    zero_bias = jnp.zeros((1, d), F32)
    final_g2 = final_g.reshape(1, d)

    cos_d, sin_d = _rope_tables(rows, DIFF_HEAD_DIM, lambda l: _diff_lane_dim(l)[1])
    lane_mla = lambda l: _mla_lane_dim(l)[1] if _mla_lane_dim(l)[0] == "rope" else -1
    cos_m, sin_m = _rope_tables(rows, MLA_ROPE_DIM, lane_mla)

    x_lat = x.reshape(batch * seq, d)
    x_ctx = ctx.reshape(batch * ctx_len, d)
    for i in range(depth):
        kind, j = i % N_MIXERS, i // N_MIXERS
        update_ctx = i < depth - 1
        ctx_feeds_mixer = update_ctx or kind != 0
        lat = [mods[i, :batch, k].reshape(batch, 1, d) for k in range(6)]
        cm = [mods[i, batch:batch + 1, k].reshape(1, 1, d) for k in range(6)]
        g_mix = norm_mix_g[i].reshape(1, d)
        g_mlp = norm_mlp_g[i].reshape(1, d)
        a_ctx = None
        if kind == 0:
            w_o = fourier_w[j].astype(BF16)
            b_o = fourier_b[j].reshape(1, d)
            a_lat = _fourier_front(x_lat, g_mix, lat[0], lat[1], mats, batch=batch)
            if update_ctx:
                a_ctx = _fourier_ctx(x_ctx, g_mix, cm[0], cm[1], mats, ctx_len=ctx_len)
        elif kind == 1:
            heads = d // DIFF_V_DIM
            w_qkv = _diff_weights(diff_w_qkv[j])
            w_o = diff_w_o[j].astype(BF16)
            b_o = zero_bias
            lam_init = 0.8 - 0.6 * math.exp(-0.3 * i)
            lams = [v[j].reshape(1, DIFF_HEAD_DIM) for v in
                    (diff_lambda_q1, diff_lambda_k1, diff_lambda_q2, diff_lambda_k2)]
            sg = diff_subln_g[j].reshape(1, DIFF_V_DIM)
            qkv_l = _diff_qkv(x_lat, g_mix, lat[0], lat[1], w_qkv, cos_d, sin_d, rope=True, tm=tm)
            qkv_l = qkv_l.reshape(batch, seq, 4 * d)
            qkv_c = _diff_qkv(x_ctx, g_mix, cm[0], cm[1], w_qkv, cos_d, sin_d, rope=False, tm=ctx_len)
            qkv_c = qkv_c.reshape(batch, ctx_len, 4 * d)
            o_l = _pipe_attn(lams + [sg], qkv_l, [qkv_c, qkv_l], [qkv_c, qkv_l], mode="diff",
                             units=heads, q_width=LANES, k_block=heads, tq=1024, lam_init=lam_init)
            a_lat = o_l.reshape(batch * seq, d)
            if update_ctx:
                o_c = _diff_attn(lams, sg, qkv_c, [qkv_c], lam_init=lam_init)
                a_ctx = o_c.reshape(batch * ctx_len, d)
        else:
            wd, wq, wk, wv = _mla_weights(mla_w_down[j], mla_w_uq[j], mla_w_ukv[j])
            w_o = mla_w_o[j].astype(BF16)
            b_o = zero_bias
            qg = mla_q_norm_g[j].reshape(1, MLA_Q_RANK)
            kvg = mla_kv_norm_g[j].reshape(1, MLA_KV_RANK)
            q_l, k_l, v_l = _mla_project(x_lat, g_mix, lat[0], lat[1], wd, qg, kvg, wq, wk, wv,
                                         cos_m, sin_m, rope=True, tm=tm)
            q_c, k_c, v_c = _mla_project(x_ctx, g_mix, cm[0], cm[1], wd, qg, kvg, wq, wk, wv,
                                         cos_m, sin_m, rope=False, tm=ctx_len)
            r3 = lambda a, s: a.reshape(batch, s, a.shape[1])
            o_l = _pipe_attn([], r3(q_l, seq), [r3(k_c, ctx_len), r3(k_l, seq)],
                             [r3(v_c, ctx_len), r3(v_l, seq)], mode="mla", units=MLA_HEADS // 2,
                             q_width=2 * LANES, k_block=0, tq=1024)
            a_lat = o_l.reshape(batch * seq, d)
            if update_ctx:
                o_c = _mla_attn(r3(q_c, ctx_len), [r3(k_c, ctx_len)], [r3(v_c, ctx_len)], tq=ctx_len)
                a_ctx = o_c.reshape(batch * ctx_len, d)
        w1 = mlp_w1[i].astype(BF16)
        w2 = mlp_w2[i].astype(BF16)
        last = i == depth - 1
        if kind == 0:
            x_lat = _fourier_tail(a_lat, mats, w_o, b_o, lat[2], x_lat, g_mlp, lat[3], lat[4], lat[5],
                                  w1, w2, final_g2, batch=batch, final_norm=last)
        else:
            x_lat = _block_tail(a_lat, w_o, b_o, lat[2], x_lat, g_mlp, lat[3], lat[4], lat[5], w1, w2,
                                final_g2, final_norm=last, tm=tm)
        if update_ctx:
            x_ctx = _block_tail(a_ctx, w_o, b_o, cm[2], x_ctx, g_mlp, cm[3], cm[4], cm[5], w1, w2,
                                final_g2, final_norm=False, tm=ctx_len)
    return x_lat.reshape(batch, seq, d)
```
